```python
import math
import jax, jax.numpy as jnp
from jax import lax
import numpy as np

D_MODEL = 1024
BATCH = 8
SEQ = 4096
DEPTH = 2

CTX_LEN = 256
GRID_W = 64
D_MIX = D_MODEL
D_A = 512
D_B = D_MIX - D_A
A_HEADS = 4
A_HEAD_DIM = D_A // A_HEADS
CHUNK = 64
SHORT_K = 3
HY_EMB = 33
HY_BANDS = (HY_EMB - 1) // 2
HY_WIDTH = 64
HY_MIN_DECAY = math.log(1e-2) / 1.5
HY_MAX_DECAY = math.log(1e-2) / 0.3
N_IN = 5 * D_A + 4 * D_B
EPS = 1e-6

kernel_name = "hymba_hgrn2_hyena_prefix_dit"

F32 = jnp.float32


def rmsnorm(x, w):
    xf = x.astype(F32)
    y = xf * lax.rsqrt(jnp.mean(xf * xf, axis=-1, keepdims=True) + EPS)
    return y.astype(x.dtype) * w


def _chunk_scan(q, k, v, g, S0):
    Bn, H, L, DK = q.shape
    DV = v.shape[-1]
    N = L // CHUNK

    def to_chunks(a):
        return jnp.moveaxis(a.reshape(Bn, H, N, CHUNK, a.shape[-1]), 2, 0)

    mask = jnp.tril(jnp.ones((CHUNK, CHUNK), dtype=bool))[:, :, None]

    def step(S, inp):
        qc, kc, vc, gc = inp
        bcum = jnp.cumsum(gc, axis=2)
        o_inter = jnp.einsum('bhck,bhkv->bhcv', qc * jnp.exp(bcum), S)
        diff = bcum[:, :, :, None, :] - bcum[:, :, None, :, :]
        decay = jnp.exp(jnp.where(mask, diff, -jnp.inf))
        scores = jnp.einsum('bhik,bhijk,bhjk->bhij', qc, decay, kc)
        o = o_inter + jnp.einsum('bhij,bhjv->bhiv', scores, vc)
        b_last = bcum[:, :, -1:, :]
        S_new = jnp.exp(b_last[:, :, 0, :])[..., None] * S + jnp.einsum(
            'bhck,bhcv->bhkv', kc * jnp.exp(b_last - bcum), vc)
        return S_new, o

    S_fin, o = lax.scan(step, S0, (to_chunks(q), to_chunks(k), to_chunks(v), to_chunks(g)))
    o = jnp.moveaxis(o, 0, 2).reshape(Bn, H, L, DV)
    return o, S_fin


def hgrn2_mixer(q_raw, ff_raw, fb_raw, i_raw, lb, S0_f, S0_b):
    Bn, L, _ = q_raw.shape

    def heads(a):
        return a.astype(F32).reshape(Bn, L, A_HEADS, A_HEAD_DIM).transpose(0, 2, 1, 3)

    q = jax.nn.silu(heads(q_raw))
    v = heads(i_raw)

    def gates(f_raw, lb_dir):
        f = heads(f_raw)
        lbh = lb_dir.reshape(A_HEADS, 1, A_HEAD_DIM)
        g = jnp.logaddexp(jnp.log(lbh), jnp.log1p(-lbh) + jax.nn.log_sigmoid(f))
        k = (1.0 - lbh) * jax.nn.sigmoid(-f)
        return k, g

    k_f, g_f = gates(ff_raw, lb[0])
    k_b, g_b = gates(fb_raw, lb[1])
    o_f, S_f = _chunk_scan(q, k_f, v, g_f, S0_f)

    def flip(a):
        return jnp.flip(a, axis=2)

    o_b, S_b = _chunk_scan(flip(q), flip(k_b), flip(v), flip(g_b), S0_b)
    return o_f + flip(o_b), S_f, S_b


def head_rmsnorm(o, w):
    Bn, H, L, DV = o.shape
    o = o * lax.rsqrt(jnp.mean(o * o, axis=-1, keepdims=True) + EPS)
    return o.transpose(0, 2, 1, 3).reshape(Bn, L, H * DV) * w.astype(F32)


def pos_features(L):
    t = jnp.linspace(0.0, 1.0, L, dtype=F32)[:, None]
    w = 2.0 * math.pi * jnp.arange(L, dtype=F32)[:, None] / L
    f = jnp.linspace(1e-4, HY_BANDS - 1, HY_BANDS, dtype=F32)[None, :]
    return jnp.concatenate([t, jnp.cos(f * w), -jnp.sin(f * w)], axis=-1)


def hyena_filter(L, w1, b1, freq, w2, b2, w3, b3, w4):
    z = pos_features(L)
    fr = freq.astype(F32)
    h = jnp.sin(fr * (z @ w1.astype(F32) + b1.astype(F32)))
    h = jnp.sin(fr * (h @ w2.astype(F32) + b2.astype(F32)))
    h = jnp.sin(fr * (h @ w3.astype(F32) + b3.astype(F32)))
    h = h @ w4.astype(F32)
    deltas = jnp.abs(jnp.linspace(HY_MIN_DECAY, HY_MAX_DECAY, D_B, dtype=F32))
    decay = jnp.exp(-z[:, :1] * deltas)
    h_f = h[:, :D_B] * decay
    h_b = h[:, D_B:] * decay
    k = jnp.concatenate([h_f, jnp.zeros((1, D_B), F32), jnp.flip(h_b[1:], axis=0)], axis=0)
    return k / jnp.sum(jnp.abs(k), axis=0, keepdims=True)


def long_conv(v, k):
    L = v.shape[1]
    vf = jnp.fft.rfft(v.astype(F32), n=2 * L, axis=1)
    kf = jnp.fft.rfft(k, n=2 * L, axis=0)
    return jnp.fft.irfft(vf * kf[None], n=2 * L, axis=1)[:, :L].astype(v.dtype)


def short_conv(u, w, b):
    L = u.shape[1]
    pad = SHORT_K // 2
    up = jnp.pad(u, ((0, 0), (pad, SHORT_K - 1 - pad), (0, 0)))
    out = b
    for j in range(SHORT_K):
        out = out + up[:, j:j + L] * w[j]
    return out


def hyena_mixer(u, conv_w, conv_b, filt, hy_bias):
    uc = short_conv(u, conv_w, conv_b)
    x0, x1, v = jnp.split(uc, 3, axis=-1)
    v = v * x1
    y = long_conv(v, filt) + v * hy_bias
    return y * x0


def mixer_outputs(p, lb, g_norm_w, conv_w, conv_b, filt, hy_bias, S0_f, S0_b):
    q, ff, fb, iv, gA = [p[..., j * D_A:(j + 1) * D_A] for j in range(5)]
    uB = p[..., 5 * D_A:5 * D_A + 3 * D_B]
    gB = p[..., 5 * D_A + 3 * D_B:]
    oA, S_f, S_b = hgrn2_mixer(q, ff, fb, iv, lb, S0_f, S0_b)
    oA = head_rmsnorm(oA, g_norm_w).astype(p.dtype) * jax.nn.silu(gA)
    oB = hyena_mixer(uB, conv_w, conv_b, filt, hy_bias) * jax.nn.silu(gB)
    return jnp.concatenate([oA, oB], axis=-1), S_f, S_b


def setup_inputs(seed: int = 0) -> dict:
    key = jax.random.key(seed)
    ks = jax.random.split(key, 24)
    nrm = jax.random.normal
    D = D_MODEL
    return {
        "x": nrm(ks[0], (BATCH, SEQ, D), F32),
        "c": nrm(ks[1], (BATCH, D), F32),
        "ctx": nrm(ks[2], (BATCH, CTX_LEN, D), F32),
        "c_ctx": nrm(ks[3], (D,), F32),
        "norm_w": 1.0 + 0.05 * nrm(ks[4], (DEPTH, D), F32),
        "w_ada": 0.5 * D ** -0.5 * nrm(ks[5], (DEPTH, D, 3 * D), F32),
        "b_ada": 0.02 * nrm(ks[6], (DEPTH, 3 * D), F32),
        "w_in": D ** -0.5 * nrm(ks[7], (DEPTH, D, N_IN), F32),
        "w_out": D_MIX ** -0.5 * nrm(ks[8], (DEPTH, D_MIX, D), F32),
        "lb_logits": 0.5 * nrm(ks[9], (DEPTH, 2, D_A), F32),
        "g_norm_w": 1.0 + 0.05 * nrm(ks[10], (DEPTH, D_A), F32),
        "conv_w": 0.5 * nrm(ks[11], (DEPTH, SHORT_K, 3 * D_B), F32),
        "conv_b": 0.02 * nrm(ks[12], (DEPTH, 3 * D_B), F32),
        "hy_w1": HY_EMB ** -0.5 * nrm(ks[13], (DEPTH, HY_EMB, HY_WIDTH), F32),
        "hy_b1": 0.1 * nrm(ks[14], (DEPTH, HY_WIDTH), F32),
        "hy_freq": 1.0 + 0.05 * nrm(ks[15], (DEPTH, HY_WIDTH), F32),
        "hy_w2": HY_WIDTH ** -0.5 * nrm(ks[16], (DEPTH, HY_WIDTH, HY_WIDTH), F32),
        "hy_b2": 0.1 * nrm(ks[17], (DEPTH, HY_WIDTH), F32),
        "hy_w3": HY_WIDTH ** -0.5 * nrm(ks[18], (DEPTH, HY_WIDTH, HY_WIDTH), F32),
        "hy_b3": 0.1 * nrm(ks[19], (DEPTH, HY_WIDTH), F32),
        "hy_w4": HY_WIDTH ** -0.5 * nrm(ks[20], (DEPTH, HY_WIDTH, 2 * D_B), F32),
        "hy_bias": nrm(ks[21], (DEPTH, D_B), F32),
        "final_norm_w": 1.0 + 0.05 * nrm(ks[22], (D,), F32),
    }


def reference(x, c, ctx, c_ctx, norm_w, w_ada, b_ada, w_in, w_out, lb_logits, g_norm_w,
              conv_w, conv_b, hy_w1, hy_b1, hy_freq, hy_w2, hy_b2, hy_w3, hy_b3, hy_w4,
              hy_bias, final_norm_w):
    Bn, L_lat, _ = x.shape
    L_ctx = ctx.shape[1]
    p_lb = jax.nn.softmax(lb_logits.astype(F32), axis=0)
    lbs = jnp.cumsum(p_lb, axis=0)
    lbs = lbs - lbs[0:1]
    zero_state = jnp.zeros((Bn, A_HEADS, A_HEAD_DIM, A_HEAD_DIM), F32)

    for l in range(DEPTH):
        last = l == DEPTH - 1
        filt_args = (hy_w1[l], hy_b1[l], hy_freq[l], hy_w2[l], hy_b2[l], hy_w3[l], hy_b3[l], hy_w4[l])
        mod_x = jax.nn.silu(c) @ w_ada[l] + b_ada[l]
        sh_x, sc_x, gt_x = jnp.split(mod_x[:, None, :], 3, axis=-1)
        mod_c = jax.nn.silu(c_ctx) @ w_ada[l] + b_ada[l]
        sh_c, sc_c, gt_c = jnp.split(mod_c, 3, axis=-1)

        hc = rmsnorm(ctx, norm_w[l]) * (1.0 + sc_c) + sh_c
        if last:
            pA = hc @ w_in[l, :, :4 * D_A]
            q, ff, fb, iv = [pA[..., j * D_A:(j + 1) * D_A] for j in range(4)]
            _, S_f, S_b = hgrn2_mixer(q, ff, fb, iv, lbs[l], zero_state, zero_state)
        else:
            pc = hc @ w_in[l]
            filt_c = hyena_filter(L_ctx, *filt_args)
            oc, S_f, S_b = mixer_outputs(pc, lbs[l], g_norm_w[l], conv_w[l], conv_b[l],
                                         filt_c, hy_bias[l], zero_state, zero_state)
            ctx = ctx + gt_c * (oc @ w_out[l])

        hx = rmsnorm(x, norm_w[l]) * (1.0 + sc_x) + sh_x
        px = hx @ w_in[l]
        filt_x = hyena_filter(L_lat, *filt_args)
        ox, _, _ = mixer_outputs(px, lbs[l], g_norm_w[l], conv_w[l], conv_b[l],
                                 filt_x, hy_bias[l], S_f, S_b)
        x = x + gt_x * (ox @ w_out[l])

    return rmsnorm(x, final_norm_w)
```

```python
import functools
import math

import numpy as np
import jax
import jax.numpy as jnp
from jax import lax
from jax.experimental import pallas as pl
from jax.experimental.pallas import tpu as pltpu

F32 = jnp.float32
BF16 = jnp.bfloat16
HIGHEST = lax.Precision.HIGHEST

D_MODEL = 1024
DEPTH = 2
D_A = 512
D_B = 512
A_HEADS = 4
HEAD_DIM = 128
N_IN = 5 * D_A + 4 * D_B
HY_EMB = 33
HY_BANDS = 16
HY_WIDTH = 64
HY_MIN_DECAY = math.log(1e-2) / 1.5
HY_MAX_DECAY = math.log(1e-2) / 0.3
EPS = 1e-6

LANES = 128
SUBLANES = 8
SCAN_CHUNK = 64
FFT_MINOR = 64
VMEM_LIMIT = 56 * 1024 * 1024


def _cparams(*sem):
    return pltpu.CompilerParams(dimension_semantics=sem, vmem_limit_bytes=VMEM_LIMIT)


def _mod_kernel(c_ref, w_ref, b_ref, o_ref):
    cc = c_ref[...]
    s = cc * jax.nn.sigmoid(cc)
    o_ref[...] = jnp.dot(s, w_ref[...], precision=HIGHEST,
                         preferred_element_type=F32) + b_ref[...]


def _modulation(cc, w_ada, b_ada):
    R, D = cc.shape
    N = w_ada.shape[-1]
    tn = 768
    return pl.pallas_call(
        _mod_kernel,
        grid=(DEPTH, N // tn),
        in_specs=[
            pl.BlockSpec((R, D), lambda l, j: (0, 0)),
            pl.BlockSpec((None, D, tn), lambda l, j: (l, 0, j)),
            pl.BlockSpec((None, 1, tn), lambda l, j: (l, 0, j)),
        ],
        out_specs=pl.BlockSpec((None, R, tn), lambda l, j: (l, 0, j)),
        out_shape=jax.ShapeDtypeStruct((DEPTH, R, N), F32),
        compiler_params=_cparams("arbitrary", "arbitrary"),
        name="modulation",
    )(cc, w_ada, b_ada.reshape(DEPTH, 1, N))


def _inproj_kernel(x_ref, nw_ref, sc_ref, sh_ref, w_ref, o_ref, hx_ref):
    @pl.when(pl.program_id(2) == 0)
    def _():
        x = x_ref[...]
        ms = jnp.mean(x * x, axis=-1, keepdims=True)
        y = x * lax.rsqrt(ms + EPS) * nw_ref[...]
        hx_ref[...] = (y * (1.0 + sc_ref[...]) + sh_ref[...]).astype(BF16)

    o_ref[...] = jnp.dot(hx_ref[...], w_ref[...], preferred_element_type=F32)


def _in_proj(x, norm_w, sc, sh, w_bf16, tn):
    B, L, D = x.shape
    N = w_bf16.shape[1]
    tm = min(L, 512)
    return pl.pallas_call(
        _inproj_kernel,
        grid=(B, L // tm, N // tn),
        in_specs=[
            pl.BlockSpec((None, tm, D), lambda b, i, j: (b, i, 0)),
            pl.BlockSpec((1, D), lambda b, i, j: (0, 0)),
            pl.BlockSpec((None, 1, D), lambda b, i, j: (b, 0, 0)),
            pl.BlockSpec((None, 1, D), lambda b, i, j: (b, 0, 0)),
            pl.BlockSpec((D, tn), lambda b, i, j: (0, j)),
        ],
        out_specs=pl.BlockSpec((None, tm, tn), lambda b, i, j: (b, i, j)),
        out_shape=jax.ShapeDtypeStruct((B, L, N), F32),
        scratch_shapes=[pltpu.VMEM((tm, D), BF16)],
        compiler_params=_cparams("arbitrary", "arbitrary", "arbitrary"),
        name="in_proj",
    )(x, norm_w.reshape(1, D), sc, sh, w_bf16)


def _scan_consts(C, forward):
    idx = np.arange(C)
    i = idx[:, None]
    t = idx[None, :]
    if forward:
        mats = [t <= i, t > i]
    else:
        mats = [t >= i, t < i]
    masks = [i == t]
    h = 1
    while h < C:
        P = 2 * h
        p = i % P
        m = i - p + h
        upper = p >= h
        same = (i // P) == (t // P)
        if forward:
            mats.append(np.where(upper, (t >= m) & (t <= i), (t >= i + 1) & (t <= m - 1)))
            masks.append(same & upper & ((t % P) < h))
        else:
            mats.append(np.where(upper, (t >= m) & (t <= i - 1), (t >= i) & (t <= m - 1)))
            masks.append(same & (~upper) & ((t % P) >= h))
        h = P
    big = np.concatenate([m_.astype(np.float32) for m_ in mats], axis=0)
    msk = np.stack([m_.astype(np.float32) for m_ in masks], axis=0)
    return big, msk


def _dot_nt(a, b):
    return lax.dot_general(a, b, (((1,), (1,)), ((), ())), preferred_element_type=F32)


def _dot_tn(a, b):
    return lax.dot_general(a, b, (((0,), (0,)), ((), ())), preferred_element_type=F32)


def _split3(a):
    hi = a.astype(BF16)
    r1 = a - hi.astype(F32)
    mid = r1.astype(BF16)
    lo = (r1 - mid.astype(F32)).astype(BF16)
    return hi, mid, lo


def _scan_chunk(q, f, v, par, big_ref, msk_ref, st, C, forward):
    loglb = par[0:1, :]
    log1mlb = par[1:2, :]
    onemlb = par[2:3, :]
    e = jnp.exp(-jnp.abs(f))
    ls = jnp.minimum(f, 0.0) - jnp.log1p(e)
    b = log1mlb + ls
    g = jnp.maximum(loglb, b) + jnp.log1p(jnp.exp(-jnp.abs(loglb - b)))
    k = onemlb * jax.nn.sigmoid(-f)
    ng = -g
    hi, mid, lo = _split3(ng)
    big = big_ref[...]
    psum = (jnp.dot(big, hi, preferred_element_type=F32)
            + jnp.dot(big, mid, preferred_element_type=F32)
            + jnp.dot(big, lo, preferred_element_type=F32))
    ex = jnp.exp(-psum)
    e_in = ex[0:C]
    e_out = ex[C:2 * C]
    nlev = msk_ref.shape[0] - 1
    o = _dot_nt((q * e_in).astype(BF16), st.astype(BF16))
    scores = msk_ref[0] * _dot_nt(q.astype(BF16), k.astype(BF16))
    for l in range(nlev):
        el = ex[(2 + l) * C:(3 + l) * C]
        scores = scores + msk_ref[1 + l] * _dot_nt((q * el).astype(BF16), (k * el).astype(BF16))
    o = o + jnp.dot(scores.astype(BF16), v.astype(BF16), preferred_element_type=F32)
    e_all = e_in[C - 1:C, :] if forward else e_in[0:1, :]
    st_new = st * e_all + _dot_tn(v.astype(BF16), (k * e_out).astype(BF16))
    return o, st_new


def _hgrn_kernel(*refs, L, C, with_out):
    if with_out:
        (q_ref, ff_ref, fb_ref, iv_ref, ga_ref, par_ref, bigf_ref, mskf_ref, bigb_ref, mskb_ref,
         s0f_ref, s0b_ref, o_ref, sf_ref, sb_ref, of_scr, ob_scr) = refs
    else:
        (q_ref, ff_ref, fb_ref, iv_ref, par_ref, bigf_ref, mskf_ref, bigb_ref, mskb_ref,
         s0f_ref, s0b_ref, sf_ref, sb_ref) = refs
    n = L // C
    sf_ref[...] = s0f_ref[...]
    sb_ref[...] = s0b_ref[...]
    par = par_ref[...]

    def body(c, carry):
        rf = pl.ds(pl.multiple_of(c * C, C), C)
        rb = pl.ds(pl.multiple_of((n - 1 - c) * C, C), C)
        qf = q_ref[rf, :]
        qf = qf * jax.nn.sigmoid(qf)
        o_f, st_f = _scan_chunk(qf, ff_ref[rf, :], iv_ref[rf, :], par[0:3], bigf_ref, mskf_ref,
                                sf_ref[...], C, True)
        sf_ref[...] = st_f
        qb = q_ref[rb, :]
        qb = qb * jax.nn.sigmoid(qb)
        o_b, st_b = _scan_chunk(qb, fb_ref[rb, :], iv_ref[rb, :], par[3:6], bigb_ref, mskb_ref,
                                sb_ref[...], C, False)
        sb_ref[...] = st_b
        if with_out:
            of_scr[rf, :] = o_f
            ob_scr[rb, :] = o_b
        return carry

    lax.fori_loop(0, n, body, 0)

    if with_out:
        gw = par[6:7, :]
        R = min(L, 256)

        def fin(r, carry):
            rows = pl.ds(pl.multiple_of(r * R, R), R)
            o = of_scr[rows, :] + ob_scr[rows, :]
            o = o * lax.rsqrt(jnp.mean(o * o, axis=-1, keepdims=True) + EPS) * gw
            ga = ga_ref[rows, :]
            o_ref[rows, :] = o * (ga * jax.nn.sigmoid(ga))
            return carry

        lax.fori_loop(0, L // R, fin, 0)


def _hgrn(px, par, s0f, s0b, with_out):
    B, L, _ = px.shape
    C = SCAN_CHUNK
    H = A_HEADS
    bigf, mskf = _scan_consts(C, True)
    bigb, mskb = _scan_consts(C, False)
    bigf, bigb = jnp.asarray(bigf, BF16), jnp.asarray(bigb, BF16)
    mskf, mskb = jnp.asarray(mskf, F32), jnp.asarray(mskb, F32)

    def col(j):
        return pl.BlockSpec((None, L, LANES), lambda b, h, j=j: (b, 0, j * H + h))

    def const(a):
        return pl.BlockSpec(a.shape, lambda b, h, nd=a.ndim: (0,) * nd)

    st_spec = pl.BlockSpec((None, None, HEAD_DIM, HEAD_DIM), lambda b, h: (b, h, 0, 0))
    st_shape = jax.ShapeDtypeStruct((B, H, HEAD_DIM, HEAD_DIM), F32)
    in_specs = [col(0), col(1), col(2), col(3)]
    args = [px, px, px, px]
    if with_out:
        in_specs.append(col(4))
        args.append(px)
    in_specs += [pl.BlockSpec((None, SUBLANES, LANES), lambda b, h: (h, 0, 0)),
                 const(bigf), const(mskf), const(bigb), const(mskb), st_spec, st_spec]
    args += [par, bigf, mskf, bigb, mskb, s0f, s0b]
    out_specs = [st_spec, st_spec]
    out_shape = [st_shape, st_shape]
    scratch = []
    if with_out:
        out_specs = [pl.BlockSpec((None, L, LANES), lambda b, h: (b, 0, h))] + out_specs
        out_shape = [jax.ShapeDtypeStruct((B, L, D_A), F32)] + out_shape
        scratch = [pltpu.VMEM((L, LANES), F32), pltpu.VMEM((L, LANES), F32)]
    res = pl.pallas_call(
        functools.partial(_hgrn_kernel, L=L, C=C, with_out=with_out),
        grid=(B, H),
        in_specs=in_specs,
        out_specs=out_specs,
        out_shape=out_shape,
        scratch_shapes=scratch,
        compiler_params=_cparams("arbitrary", "arbitrary"),
        name="hgrn2",
    )(*args)
    if with_out:
        return res[0], res[1], res[2]
    return None, res[0], res[1]


def _conv3(ref, r0, R, L, cp):
    cur = ref[pl.ds(r0, R), :]
    prev_tile = ref[pl.ds(pl.multiple_of(jnp.maximum(r0 - SUBLANES, 0), SUBLANES), SUBLANES), :]
    next_tile = ref[pl.ds(pl.multiple_of(jnp.minimum(r0 + R, L - SUBLANES), SUBLANES), SUBLANES), :]
    prow = jnp.where(r0 > 0, prev_tile[SUBLANES - 1:SUBLANES, :], 0.0)
    nrow = jnp.where(r0 + R < L, next_tile[0:1, :], 0.0)
    rid = lax.broadcasted_iota(jnp.int32, (R, LANES), 0)
    up = jnp.where(rid == 0, prow, pltpu.roll(cur, 1, 0))
    dn = jnp.where(rid == R - 1, nrow, pltpu.roll(cur, R - 1, 0))
    return cp[3:4, :] + up * cp[0:1, :] + cur * cp[1:2, :] + dn * cp[2:3, :]


def _hyena_pre_kernel(x0_ref, x1_ref, v_ref, gb_ref, c0_ref, c1_ref, c2_ref, vin_ref, m_ref, *, L):
    R = min(L, 256)
    c0 = c0_ref[...]
    c1 = c1_ref[...]
    c2 = c2_ref[...]

    def body(r, carry):
        r0 = pl.multiple_of(r * R, R)
        rows = pl.ds(r0, R)
        x0 = _conv3(x0_ref, r0, R, L, c0)
        x1 = _conv3(x1_ref, r0, R, L, c1)
        vv = _conv3(v_ref, r0, R, L, c2)
        gb = gb_ref[rows, :]
        vin_ref[rows, :] = vv * x1
        m_ref[rows, :] = x0 * (gb * jax.nn.sigmoid(gb))
        return carry

    lax.fori_loop(0, L // R, body, 0)


def _hyena_pre(px, conv_par):
    B, L, _ = px.shape
    nb = D_B // LANES
    base = 5 * D_A // LANES

    def col(j):
        return pl.BlockSpec((None, L, LANES), lambda b, c, j=j: (b, 0, base + j * nb + c))

    def cpar(j):
        return pl.BlockSpec((SUBLANES, LANES), lambda b, c, j=j: (0, j * nb + c))

    out_spec = pl.BlockSpec((None, L, LANES), lambda b, c: (b, 0, c))
    out_shape = jax.ShapeDtypeStruct((B, L, D_B), F32)
    return pl.pallas_call(
        functools.partial(_hyena_pre_kernel, L=L),
        grid=(B, nb),
        in_specs=[col(0), col(1), col(2), col(3), cpar(0), cpar(1), cpar(2)],
        out_specs=[out_spec, out_spec],
        out_shape=[out_shape, out_shape],
        compiler_params=_cparams("arbitrary", "arbitrary"),
        name="hyena_pre",
    )(px, px, px, px, conv_par, conv_par, conv_par)


def _pos_rows(L, order):
    t = jnp.linspace(0.0, 1.0, L, dtype=F32)[:, None]
    w = 2.0 * math.pi * jnp.arange(L, dtype=F32)[:, None] / L
    f = jnp.linspace(1e-4, HY_BANDS - 1, HY_BANDS, dtype=F32)[None, :]
    z = jnp.concatenate([t, jnp.cos(f * w), -jnp.sin(f * w)], axis=-1)
    lag = np.arange(2 * L)
    src = np.where(lag < L, lag, 2 * L - lag) % L
    fwd = (lag < L).astype(np.float32)
    valid = (lag != L).astype(np.float32)
    rows = jnp.concatenate(
        [z[src], jnp.asarray(fwd)[:, None], jnp.asarray(valid)[:, None],
         jnp.zeros((2 * L, HY_WIDTH - HY_EMB - 2), F32)], axis=-1)
    return rows[order]


def _filter_kernel(z_ref, w1_ref, w2_ref, w3_ref, vec_ref, w4f_ref, w4b_ref, dl_ref, k_ref, *, n_rows):
    R = min(n_rows, 512)
    vec = vec_ref[...]
    b1, b2, b3, fr = vec[0:1], vec[1:2], vec[2:3], vec[3:4]
    dl = dl_ref[...]

    def body(r, acc):
        rows = pl.ds(pl.multiple_of(r * R, R), R)
        z = z_ref[rows, :]
        h = jnp.sin(fr * (jnp.dot(z, w1_ref[...], precision=HIGHEST, preferred_element_type=F32) + b1))
        h = jnp.sin(fr * (jnp.dot(h, w2_ref[...], precision=HIGHEST, preferred_element_type=F32) + b2))
        h = jnp.sin(fr * (jnp.dot(h, w3_ref[...], precision=HIGHEST, preferred_element_type=F32) + b3))
        hf = jnp.dot(h, w4f_ref[...], precision=HIGHEST, preferred_element_type=F32)
        hb = jnp.dot(h, w4b_ref[...], precision=HIGHEST, preferred_element_type=F32)
        fwd = z[:, HY_EMB:HY_EMB + 1]
        valid = z[:, HY_EMB + 1:HY_EMB + 2]
        k = jnp.where(fwd > 0.5, hf, hb) * jnp.exp(-z[:, 0:1] * dl) * valid
        k_ref[rows, :] = k
        return acc + jnp.sum(jnp.abs(k), axis=0, keepdims=True)

    tot = lax.fori_loop(0, n_rows // R, body, jnp.zeros((1, LANES), F32))

    def scale(r, carry):
        rows = pl.ds(pl.multiple_of(r * R, R), R)
        k_ref[rows, :] = k_ref[rows, :] / tot
        return carry

    lax.fori_loop(0, n_rows // R, scale, 0)


def _hyena_filter(zrows, w1, b1, freq, w2, b2, w3, b3, w4):
    n_rows = zrows.shape[0]
    w1p = jnp.concatenate([w1, jnp.zeros((HY_WIDTH - HY_EMB, HY_WIDTH), F32)], axis=0)
    vec = jnp.concatenate([b1[None], b2[None], b3[None], freq[None],
                           jnp.zeros((SUBLANES - 4, HY_WIDTH), F32)], axis=0)
    deltas = jnp.abs(jnp.linspace(HY_MIN_DECAY, HY_MAX_DECAY, D_B, dtype=F32))[None, :]
    nb = D_B // LANES

    def const(shape):
        return pl.BlockSpec(shape, lambda c, nd=len(shape): (0,) * nd)

    return pl.pallas_call(
        functools.partial(_filter_kernel, n_rows=n_rows),
        grid=(nb,),
        in_specs=[const((n_rows, HY_WIDTH)), const((HY_WIDTH, HY_WIDTH)), const((HY_WIDTH, HY_WIDTH)),
                  const((HY_WIDTH, HY_WIDTH)), const((SUBLANES, HY_WIDTH)),
                  pl.BlockSpec((HY_WIDTH, LANES), lambda c: (0, c)),
                  pl.BlockSpec((HY_WIDTH, LANES), lambda c: (0, nb + c)),
                  pl.BlockSpec((1, LANES), lambda c: (0, c))],
        out_specs=pl.BlockSpec((n_rows, LANES), lambda c: (0, c)),
        out_shape=jax.ShapeDtypeStruct((n_rows, D_B), F32),
        compiler_params=_cparams("arbitrary"),
        name="hyena_filter",
    )(zrows, w1p, w2, w3, vec, w4, w4, deltas)


def _cplx_block(gr, gi):
    return np.block([[gr, -gi], [gi, gr]])


def _dft_consts(L):
    N = 2 * L
    n2 = FFT_MINOR
    n1 = N // n2
    half = n1 // 2
    f1 = np.arange(n1)
    t2 = np.arange(n2)
    t1 = np.arange(n1)
    tt = n2 * t1[None, None, :] + t2[:, None, None]
    ang = -2.0 * np.pi * ((f1[None, :, None] * tt) % N) / N
    gr, gi = np.cos(ang), np.sin(ang)
    m1_data = np.stack([_cplx_block(gr[j][:, :half], gi[j][:, :half]) for j in range(n2)])
    m1_filt = np.concatenate([gr, gi], axis=1)
    m1_inv = np.stack([_cplx_block(gr[j][:, :half].T, -gi[j][:, :half].T) for j in range(n2)]) / N
    ang2 = -2.0 * np.pi * ((t2[:, None] * t2[None, :]) % n2) / n2
    f2 = _cplx_block(np.cos(ang2), np.sin(ang2))
    f2_inv = _cplx_block(np.cos(ang2), -np.sin(ang2))
    as32 = lambda a: jnp.asarray(a, F32)
    return dict(n1=n1, n2=n2, m1_data=as32(m1_data), m1_filt=as32(m1_filt), m1_inv=as32(m1_inv),
                f2=as32(f2), f2_inv=as32(f2_inv))


def _dft_single_consts(L):
    N = 2 * L
    f = np.arange(N)
    ang = -2.0 * np.pi * ((f[:, None] * f[None, :]) % N) / N
    gr, gi = np.cos(ang), np.sin(ang)
    fwd = _cplx_block(gr[:, :L], gi[:, :L])
    filt = np.concatenate([gr, gi], axis=0)
    inv = _cplx_block(gr[:L, :], -gi[:L, :]) / N
    as32 = lambda a: jnp.asarray(a, F32)
    return dict(fwd=as32(fwd), filt=as32(filt), inv=as32(inv))


def _bmm_kernel(m_ref, x_ref, o_ref, *, gblk, shared):
    for j in range(gblk):
        o_ref[j] = jnp.dot(m_ref[0 if shared else j], x_ref[j], precision=HIGHEST,
                           preferred_element_type=F32)


def _bmm_left(m, x, gblk):
    P, G, K, C = x.shape
    R = m.shape[1]
    shared = m.shape[0] == 1
    gblk = min(gblk, G)
    m_spec = (pl.BlockSpec((1, R, K), lambda g, p: (0, 0, 0)) if shared
              else pl.BlockSpec((gblk, R, K), lambda g, p: (g, 0, 0)))
    return pl.pallas_call(
        functools.partial(_bmm_kernel, gblk=gblk, shared=shared),
        grid=(G // gblk, P),
        in_specs=[m_spec, pl.BlockSpec((None, gblk, K, C), lambda g, p: (p, g, 0, 0))],
        out_specs=pl.BlockSpec((None, gblk, R, C), lambda g, p: (p, g, 0, 0)),
        out_shape=jax.ShapeDtypeStruct((P, G, R, C), F32),
        compiler_params=_cparams("arbitrary", "arbitrary"),
        name="dft_stage",
    )(m, x)


def _mid_kernel(fa_ref, fb_ref, kf_ref, d_ref, o_ref, *, gblk, nf):
    fa = fa_ref[...]
    fb = fb_ref[...]
    for j in range(gblk):
        xx = jnp.dot(fa, d_ref[j], precision=HIGHEST, preferred_element_type=F32)
        xr, xi = xx[:nf], xx[nf:]
        kr, ki = kf_ref[j, :nf], kf_ref[j, nf:]
        yy = jnp.concatenate([xr * kr - xi * ki, xr * ki + xi * kr], axis=0)
        o_ref[j] = jnp.dot(fb, yy, precision=HIGHEST, preferred_element_type=F32)


def _spectral_mid(fa, fb, kf, d, gblk):
    P, G, K, C = d.shape
    nf2 = fa.shape[0]
    Ko = fb.shape[0]
    gblk = min(gblk, G)
    return pl.pallas_call(
        functools.partial(_mid_kernel, gblk=gblk, nf=nf2 // 2),
        grid=(G // gblk, P),
        in_specs=[pl.BlockSpec(fa.shape, lambda g, p: (0, 0)),
                  pl.BlockSpec(fb.shape, lambda g, p: (0, 0)),
                  pl.BlockSpec((gblk, nf2, C), lambda g, p: (g, 0, 0)),
                  pl.BlockSpec((None, gblk, K, C), lambda g, p: (p, g, 0, 0))],
        out_specs=pl.BlockSpec((None, gblk, Ko, C), lambda g, p: (p, g, 0, 0)),
        out_shape=jax.ShapeDtypeStruct((P, G, Ko, C), F32),
        compiler_params=_cparams("arbitrary", "arbitrary"),
        name="dft_mid",
    )(fa, fb, kf, d)


def _long_conv_two_stage(vin, filt_rows, dc):
    B, L, C = vin.shape
    n1, n2 = dc["n1"], dc["n2"]
    half = n1 // 2
    P = B // 2
    ka = _bmm_left(dc["m1_filt"], filt_rows.reshape(1, n2, n1, C), 8)
    ka = ka.reshape(n2, 2, n1, C).transpose(2, 1, 0, 3).reshape(1, n1, 2 * n2, C)
    kf = _bmm_left(dc["f2"][None], ka, 8)[0]
    x = vin.reshape(2, P, half, n2, C).transpose(1, 3, 0, 2, 4).reshape(P, n2, n1, C)
    a = _bmm_left(dc["m1_data"], x, 8)
    a = a.reshape(P, n2, 2, n1, C).transpose(0, 3, 2, 1, 4).reshape(P, n1, 2 * n2, C)
    z = _spectral_mid(dc["f2"], dc["f2_inv"], kf, a, 8)
    z = z.reshape(P, n1, 2, n2, C).transpose(0, 3, 2, 1, 4).reshape(P, n2, 2 * n1, C)
    y = _bmm_left(dc["m1_inv"], z, 8)
    return y.reshape(P, n2, 2, half, C).transpose(2, 0, 3, 1, 4).reshape(B, L, C)


def _long_conv_single(vin, filt, dc):
    B, L, C = vin.shape
    P = B // 2
    kf = _bmm_left(dc["filt"][None], filt.reshape(1, 1, 2 * L, C), 1)[0]
    d = vin.reshape(2, P, L, C).transpose(1, 0, 2, 3).reshape(P, 1, 2 * L, C)
    y = _spectral_mid(dc["fwd"], dc["inv"], kf, d, 1)
    return y.reshape(P, 2, L, C).transpose(1, 0, 2, 3).reshape(B, L, C)


def _outproj_kernel(*refs, final):
    if final:
        oa_ref, y_ref, vin_ref, m_ref, hb_ref, x_ref, gt_ref, w_ref, fw_ref, o_ref = refs
    else:
        oa_ref, y_ref, vin_ref, m_ref, hb_ref, x_ref, gt_ref, w_ref, o_ref = refs
    vin = vin_ref[...]
    ob = (y_ref[...] + vin * hb_ref[...]) * m_ref[...]
    r = (jnp.dot(oa_ref[...].astype(BF16), w_ref[0:D_A, :], preferred_element_type=F32)
         + jnp.dot(ob.astype(BF16), w_ref[D_A:D_A + D_B, :], preferred_element_type=F32))
    xn = x_ref[...] + gt_ref[...] * r
    if final:
        ms = jnp.mean(xn * xn, axis=-1, keepdims=True)
        xn = xn * lax.rsqrt(ms + EPS) * fw_ref[...]
    o_ref[...] = xn


def _out_proj(oa, y, vin, m, hy_bias, x, gt, w_bf16, final_w):
    B, L, D = x.shape
    tm = min(L, 512)
    final = final_w is not None
    half = lambda: pl.BlockSpec((None, tm, D_B), lambda b, i: (b, i, 0))
    in_specs = [half(), half(), half(), half(),
                pl.BlockSpec((1, D_B), lambda b, i: (0, 0)),
                pl.BlockSpec((None, tm, D), lambda b, i: (b, i, 0)),
                pl.BlockSpec((None, 1, D), lambda b, i: (b, 0, 0)),
                pl.BlockSpec((D, D), lambda b, i: (0, 0))]
    args = [oa, y, vin, m, hy_bias.reshape(1, D_B), x, gt, w_bf16]
    if final:
        in_specs.append(pl.BlockSpec((1, D), lambda b, i: (0, 0)))
        args.append(final_w.reshape(1, D))
    return pl.pallas_call(
        functools.partial(_outproj_kernel, final=final),
        grid=(B, L // tm),
        in_specs=in_specs,
        out_specs=pl.BlockSpec((None, tm, D), lambda b, i: (b, i, 0)),
        out_shape=jax.ShapeDtypeStruct((B, L, D), F32),
        compiler_params=_cparams("arbitrary", "arbitrary"),
        name="out_proj",
    )(*args)


def _scan_params(lb, g_norm_w):
    rows = []
    for d in range(2):
        rows += [jnp.log(lb[d]), jnp.log1p(-lb[d]), 1.0 - lb[d]]
    rows += [g_norm_w, jnp.zeros_like(g_norm_w)]
    par = jnp.stack(rows, axis=0)
    return par.reshape(SUBLANES, A_HEADS, HEAD_DIM).transpose(1, 0, 2)


def kernel(x, c, ctx, c_ctx, norm_w, w_ada, b_ada, w_in, w_out, lb_logits, g_norm_w,
           conv_w, conv_b, hy_w1, hy_b1, hy_freq, hy_w2, hy_b2, hy_w3, hy_b3, hy_w4,
           hy_bias, final_norm_w):
    B, L_lat, D = x.shape
    L_ctx = ctx.shape[1]
    p_lb = jax.nn.softmax(lb_logits.astype(F32), axis=0)
    lbs = jnp.cumsum(p_lb, axis=0)
    lbs = lbs - lbs[0:1]

    n_rows = 2 * SUBLANES
    cc = jnp.zeros((n_rows, D), F32).at[:B].set(c).at[B].set(c_ctx)
    mod = _modulation(cc, w_ada, b_ada)

    dc_lat = _dft_consts(L_lat)
    dc_ctx = _dft_single_consts(L_ctx)
    n1, n2 = dc_lat["n1"], dc_lat["n2"]
    order_lat = (np.arange(n2)[:, None] + n2 * np.arange(n1)[None, :]).reshape(-1)
    z_lat = _pos_rows(L_lat, order_lat)
    z_ctx = _pos_rows(L_ctx, np.arange(2 * L_ctx))
    zero_state = jnp.zeros((B, A_HEADS, HEAD_DIM, HEAD_DIM), F32)

    for l in range(DEPTH):
        last = l == DEPTH - 1
        sh_x, sc_x, gt_x = [mod[l, :B, j * D:(j + 1) * D].reshape(B, 1, D) for j in range(3)]
        sh_c, sc_c, gt_c = [jnp.broadcast_to(mod[l, B, j * D:(j + 1) * D].reshape(1, 1, D), (B, 1, D))
                            for j in range(3)]
        w_in_l = w_in[l].astype(BF16)
        w_out_l = w_out[l].astype(BF16)
        par = _scan_params(lbs[l], g_norm_w[l])
        conv_par = jnp.concatenate([conv_w[l], conv_b[l][None],
                                    jnp.zeros((SUBLANES - 4, 3 * D_B), F32)], axis=0)
        filt_args = (hy_w1[l], hy_b1[l], hy_freq[l], hy_w2[l], hy_b2[l], hy_w3[l], hy_b3[l], hy_w4[l])

        if last:
            pc = _in_proj(ctx, norm_w[l], sc_c, sh_c, w_in_l[:, :4 * D_A], 4 * D_A // 2)
            _, s_f, s_b = _hgrn(pc, par, zero_state, zero_state, with_out=False)
        else:
            pc = _in_proj(ctx, norm_w[l], sc_c, sh_c, w_in_l, N_IN // 3)
            oa_c, s_f, s_b = _hgrn(pc, par, zero_state, zero_state, with_out=True)
            vin_c, m_c = _hyena_pre(pc, conv_par)
            filt_c = _hyena_filter(z_ctx, *filt_args)
            y_c = _long_conv_single(vin_c, filt_c, dc_ctx)
            ctx = _out_proj(oa_c, y_c, vin_c, m_c, hy_bias[l], ctx, gt_c, w_out_l, None)

        px = _in_proj(x, norm_w[l], sc_x, sh_x, w_in_l, N_IN // 3)
        oa, _, _ = _hgrn(px, par, s_f, s_b, with_out=True)
        vin, m = _hyena_pre(px, conv_par)
        filt_x = _hyena_filter(z_lat, *filt_args)
        y = _long_conv_two_stage(vin, filt_x, dc_lat)
        x = _out_proj(oa, y, vin, m, hy_bias[l], x, gt_x, w_out_l, final_norm_w if last else None)

    return x
```

```python
import functools
import math

import numpy as np
import jax
import jax.numpy as jnp
from jax import lax
from jax.experimental import pallas as pl
from jax.experimental.pallas import tpu as pltpu

F32 = jnp.float32
BF16 = jnp.bfloat16
HIGHEST = lax.Precision.HIGHEST

D_MODEL = 1024
DEPTH = 2
D_A = 512
D_B = 512
A_HEADS = 4
HEAD_DIM = 128
N_IN = 5 * D_A + 4 * D_B
HY_EMB = 33
HY_BANDS = 16
HY_WIDTH = 64
HY_MIN_DECAY = math.log(1e-2) / 1.5
HY_MAX_DECAY = math.log(1e-2) / 0.3
EPS = 1e-6

LANES = 128
SUBLANES = 8
SCAN_CHUNK = 64
FFT_MINOR = 64
VMEM_LIMIT = 56 * 1024 * 1024


def _cparams(*sem):
    return pltpu.CompilerParams(dimension_semantics=sem, vmem_limit_bytes=VMEM_LIMIT)


def _mod_kernel(c_ref, w_ref, b_ref, o_ref):
    cc = c_ref[...]
    s = cc * jax.nn.sigmoid(cc)
    o_ref[...] = jnp.dot(s, w_ref[...], precision=HIGHEST,
                         preferred_element_type=F32) + b_ref[...]


def _modulation(cc, w_ada, b_ada):
    R, D = cc.shape
    N = w_ada.shape[-1]
    tn = 768
    return pl.pallas_call(
        _mod_kernel,
        grid=(DEPTH, N // tn),
        in_specs=[
            pl.BlockSpec((R, D), lambda l, j: (0, 0)),
            pl.BlockSpec((None, D, tn), lambda l, j: (l, 0, j)),
            pl.BlockSpec((None, 1, tn), lambda l, j: (l, 0, j)),
        ],
        out_specs=pl.BlockSpec((None, R, tn), lambda l, j: (l, 0, j)),
        out_shape=jax.ShapeDtypeStruct((DEPTH, R, N), F32),
        compiler_params=_cparams("arbitrary", "arbitrary"),
        name="modulation",
    )(cc, w_ada, b_ada.reshape(DEPTH, 1, N))


def _inproj_kernel(x_ref, nw_ref, sc_ref, sh_ref, w_ref, o_ref, hx_ref, *, tn):
    x = x_ref[...]
    ms = jnp.mean(x * x, axis=-1, keepdims=True)
    y = x * lax.rsqrt(ms + EPS) * nw_ref[...]
    hx_ref[...] = (y * (1.0 + sc_ref[...]) + sh_ref[...]).astype(BF16)
    for n0 in range(0, o_ref.shape[-1], tn):
        o_ref[:, n0:n0 + tn] = jnp.dot(hx_ref[...], w_ref[:, n0:n0 + tn], preferred_element_type=F32)


def _in_proj(x, norm_w, sc, sh, w_bf16, tn):
    B, L, D = x.shape
    N = w_bf16.shape[1]
    tm = min(L, 512)
    return pl.pallas_call(
        functools.partial(_inproj_kernel, tn=tn),
        grid=(B, L // tm),
        in_specs=[
            pl.BlockSpec((None, tm, D), lambda b, i: (b, i, 0)),
            pl.BlockSpec((1, D), lambda b, i: (0, 0)),
            pl.BlockSpec((None, 1, D), lambda b, i: (b, 0, 0)),
            pl.BlockSpec((None, 1, D), lambda b, i: (b, 0, 0)),
            pl.BlockSpec((D, N), lambda b, i: (0, 0)),
        ],
        out_specs=pl.BlockSpec((None, tm, N), lambda b, i: (b, i, 0)),
        out_shape=jax.ShapeDtypeStruct((B, L, N), F32),
        scratch_shapes=[pltpu.VMEM((tm, D), BF16)],
        compiler_params=_cparams("arbitrary", "arbitrary"),
        name="in_proj",
    )(x, norm_w.reshape(1, D), sc, sh, w_bf16)


def _scan_consts(C, forward):
    idx = np.arange(C)
    i = idx[:, None]
    t = idx[None, :]
    if forward:
        mats = [t <= i, t > i]
    else:
        mats = [t >= i, t < i]
    masks = [i == t]
    h = 1
    while h < C:
        P = 2 * h
        p = i % P
        m = i - p + h
        upper = p >= h
        same = (i // P) == (t // P)
        if forward:
            mats.append(np.where(upper, (t >= m) & (t <= i), (t >= i + 1) & (t <= m - 1)))
            masks.append(same & upper & ((t % P) < h))
        else:
            mats.append(np.where(upper, (t >= m) & (t <= i - 1), (t >= i) & (t <= m - 1)))
            masks.append(same & (~upper) & ((t % P) >= h))
        h = P
    big = np.concatenate([m_.astype(np.float32) for m_ in mats], axis=0)
    big = np.concatenate([big, big, big], axis=1)
    msk = np.stack([m_.astype(np.float32) for m_ in masks], axis=0)
    return big, msk


def _dot_nt(a, b):
    return lax.dot_general(a, b, (((1,), (1,)), ((), ())), preferred_element_type=F32)


def _dot_tn(a, b):
    return lax.dot_general(a, b, (((0,), (0,)), ((), ())), preferred_element_type=F32)


def _split3(a):
    hi = a.astype(BF16)
    r1 = a - hi.astype(F32)
    mid = r1.astype(BF16)
    lo = (r1 - mid.astype(F32)).astype(BF16)
    return hi, mid, lo


def _scan_chunk(q, f, v, par, big_ref, msk_ref, st, C, forward):
    loglb = par[0:1, :]
    log1mlb = par[1:2, :]
    onemlb = par[2:3, :]
    e = jnp.exp(-jnp.abs(f))
    ls = jnp.minimum(f, 0.0) - jnp.log1p(e)
    b = log1mlb + ls
    g = jnp.maximum(loglb, b) + jnp.log1p(jnp.exp(-jnp.abs(loglb - b)))
    k = onemlb * jax.nn.sigmoid(-f)
    ng = -g
    psum = jnp.dot(big_ref[...], jnp.concatenate(_split3(ng), axis=0), preferred_element_type=F32)
    ex = jnp.exp(-psum)
    e_in = ex[0:C]
    e_out = ex[C:2 * C]
    nlev = msk_ref.shape[0] - 1
    o = _dot_nt((q * e_in).astype(BF16), st.astype(BF16))
    scores = msk_ref[0] * _dot_nt(q.astype(BF16), k.astype(BF16))
    for l in range(nlev):
        el = ex[(2 + l) * C:(3 + l) * C]
        scores = scores + msk_ref[1 + l] * _dot_nt((q * el).astype(BF16), (k * el).astype(BF16))
    o = o + jnp.dot(scores.astype(BF16), v.astype(BF16), preferred_element_type=F32)
    e_all = e_in[C - 1:C, :] if forward else e_in[0:1, :]
    st_new = st * e_all + _dot_tn(v.astype(BF16), (k * e_out).astype(BF16))
    return o, st_new


def _hgrn_kernel(*refs, L, C, with_out):
    if with_out:
        (q_ref, ff_ref, fb_ref, iv_ref, ga_ref, par_ref, bigf_ref, mskf_ref, bigb_ref, mskb_ref,
         s0f_ref, s0b_ref, o_ref, sf_ref, sb_ref, of_scr, ob_scr) = refs
    else:
        (q_ref, ff_ref, fb_ref, iv_ref, par_ref, bigf_ref, mskf_ref, bigb_ref, mskb_ref,
         s0f_ref, s0b_ref, sf_ref, sb_ref) = refs
    n = L // C
    sf_ref[...] = s0f_ref[...]
    sb_ref[...] = s0b_ref[...]
    par = par_ref[...]

    def body(c, carry):
        rf = pl.ds(pl.multiple_of(c * C, C), C)
        rb = pl.ds(pl.multiple_of((n - 1 - c) * C, C), C)
        qf = q_ref[rf, :]
        qf = qf * jax.nn.sigmoid(qf)
        o_f, st_f = _scan_chunk(qf, ff_ref[rf, :], iv_ref[rf, :], par[0:3], bigf_ref, mskf_ref,
                                sf_ref[...], C, True)
        sf_ref[...] = st_f
        qb = q_ref[rb, :]
        qb = qb * jax.nn.sigmoid(qb)
        o_b, st_b = _scan_chunk(qb, fb_ref[rb, :], iv_ref[rb, :], par[3:6], bigb_ref, mskb_ref,
                                sb_ref[...], C, False)
        sb_ref[...] = st_b
        if with_out:
            of_scr[rf, :] = o_f
            ob_scr[rb, :] = o_b
        return carry

    lax.fori_loop(0, n, body, 0, unroll=4)

    if with_out:
        gw = par[6:7, :]
        R = min(L, 256)

        def fin(r, carry):
            rows = pl.ds(pl.multiple_of(r * R, R), R)
            o = of_scr[rows, :] + ob_scr[rows, :]
            o = o * lax.rsqrt(jnp.mean(o * o, axis=-1, keepdims=True) + EPS) * gw
            ga = ga_ref[rows, :]
            o_ref[rows, :] = o * (ga * jax.nn.sigmoid(ga))
            return carry

        lax.fori_loop(0, L // R, fin, 0)


def _hgrn(px, par, s0f, s0b, with_out):
    B, L, _ = px.shape
    C = SCAN_CHUNK
    H = A_HEADS
    bigf, mskf = _scan_consts(C, True)
    bigb, mskb = _scan_consts(C, False)
    bigf, bigb = jnp.asarray(bigf, BF16), jnp.asarray(bigb, BF16)
    mskf, mskb = jnp.asarray(mskf, F32), jnp.asarray(mskb, F32)

    def col(j):
        return pl.BlockSpec((None, L, LANES), lambda b, h, j=j: (b, 0, j * H + h))

    def const(a):
        return pl.BlockSpec(a.shape, lambda b, h, nd=a.ndim: (0,) * nd)

    st_spec = pl.BlockSpec((None, None, HEAD_DIM, HEAD_DIM), lambda b, h: (b, h, 0, 0))
    st_shape = jax.ShapeDtypeStruct((B, H, HEAD_DIM, HEAD_DIM), F32)
    in_specs = [col(0), col(1), col(2), col(3)]
    args = [px, px, px, px]
    if with_out:
        in_specs.append(col(4))
        args.append(px)
    in_specs += [pl.BlockSpec((None, SUBLANES, LANES), lambda b, h: (h, 0, 0)),
                 const(bigf), const(mskf), const(bigb), const(mskb), st_spec, st_spec]
    args += [par, bigf, mskf, bigb, mskb, s0f, s0b]
    out_specs = [st_spec, st_spec]
    out_shape = [st_shape, st_shape]
    scratch = []
    if with_out:
        out_specs = [pl.BlockSpec((None, L, LANES), lambda b, h: (b, 0, h))] + out_specs
        out_shape = [jax.ShapeDtypeStruct((B, L, D_A), F32)] + out_shape
        scratch = [pltpu.VMEM((L, LANES), F32), pltpu.VMEM((L, LANES), F32)]
    res = pl.pallas_call(
        functools.partial(_hgrn_kernel, L=L, C=C, with_out=with_out),
        grid=(B, H),
        in_specs=in_specs,
        out_specs=out_specs,
        out_shape=out_shape,
        scratch_shapes=scratch,
        compiler_params=_cparams("arbitrary", "arbitrary"),
        name="hgrn2",
    )(*args)
    if with_out:
        return res[0], res[1], res[2]
    return None, res[0], res[1]


def _conv3(ref, r0, R, L, cp):
    cur = ref[pl.ds(r0, R), :]
    prev_tile = ref[pl.ds(pl.multiple_of(jnp.maximum(r0 - SUBLANES, 0), SUBLANES), SUBLANES), :]
    next_tile = ref[pl.ds(pl.multiple_of(jnp.minimum(r0 + R, L - SUBLANES), SUBLANES), SUBLANES), :]
    prow = jnp.where(r0 > 0, prev_tile[SUBLANES - 1:SUBLANES, :], 0.0)
    nrow = jnp.where(r0 + R < L, next_tile[0:1, :], 0.0)
    rid = lax.broadcasted_iota(jnp.int32, (R, LANES), 0)
    up = jnp.where(rid == 0, prow, pltpu.roll(cur, 1, 0))
    dn = jnp.where(rid == R - 1, nrow, pltpu.roll(cur, R - 1, 0))
    return cp[3:4, :] + up * cp[0:1, :] + cur * cp[1:2, :] + dn * cp[2:3, :]


def _hyena_pre_kernel(x0_ref, x1_ref, v_ref, gb_ref, c0_ref, c1_ref, c2_ref, vin_ref, m_ref, *, L):
    R = min(L, 256)
    c0 = c0_ref[...]
    c1 = c1_ref[...]
    c2 = c2_ref[...]

    def body(r, carry):
        r0 = pl.multiple_of(r * R, R)
        rows = pl.ds(r0, R)
        x0 = _conv3(x0_ref, r0, R, L, c0)
        x1 = _conv3(x1_ref, r0, R, L, c1)
        vv = _conv3(v_ref, r0, R, L, c2)
        gb = gb_ref[rows, :]
        vin_ref[rows, :] = vv * x1
        m_ref[rows, :] = x0 * (gb * jax.nn.sigmoid(gb))
        return carry

    lax.fori_loop(0, L // R, body, 0)


def _hyena_pre(px, conv_par):
    B, L, _ = px.shape
    nb = D_B // LANES
    base = 5 * D_A // LANES

    def col(j):
        return pl.BlockSpec((None, L, LANES), lambda b, c, j=j: (b, 0, base + j * nb + c))

    def cpar(j):
        return pl.BlockSpec((SUBLANES, LANES), lambda b, c, j=j: (0, j * nb + c))

    out_spec = pl.BlockSpec((None, L, LANES), lambda b, c: (b, 0, c))
    out_shape = jax.ShapeDtypeStruct((B, L, D_B), F32)
    return pl.pallas_call(
        functools.partial(_hyena_pre_kernel, L=L),
        grid=(B, nb),
        in_specs=[col(0), col(1), col(2), col(3), cpar(0), cpar(1), cpar(2)],
        out_specs=[out_spec, out_spec],
        out_shape=[out_shape, out_shape],
        compiler_params=_cparams("arbitrary", "arbitrary"),
        name="hyena_pre",
    )(px, px, px, px, conv_par, conv_par, conv_par)


def _pos_rows(L, order):
    t = jnp.linspace(0.0, 1.0, L, dtype=F32)[:, None]
    w = 2.0 * math.pi * jnp.arange(L, dtype=F32)[:, None] / L
    f = jnp.linspace(1e-4, HY_BANDS - 1, HY_BANDS, dtype=F32)[None, :]
    z = jnp.concatenate([t, jnp.cos(f * w), -jnp.sin(f * w)], axis=-1)
    lag = np.arange(2 * L)
    src = np.where(lag < L, lag, 2 * L - lag) % L
    fwd = (lag < L).astype(np.float32)
    valid = (lag != L).astype(np.float32)
    rows = jnp.concatenate(
        [z[src], jnp.asarray(fwd)[:, None], jnp.asarray(valid)[:, None],
         jnp.zeros((2 * L, HY_WIDTH - HY_EMB - 2), F32)], axis=-1)
    return rows[order]


def _filter_kernel(z_ref, w1_ref, w2_ref, w3_ref, vec_ref, w4f_ref, w4b_ref, dl_ref, k_ref, h_scr,
                   *, n_rows):
    R = min(n_rows, 512)
    dl = dl_ref[...]

    @pl.when(pl.program_id(0) == 0)
    def _():
        vec = vec_ref[...]
        b1, b2, b3, fr = vec[0:1], vec[1:2], vec[2:3], vec[3:4]

        def mlp(r, carry):
            rows = pl.ds(pl.multiple_of(r * R, R), R)
            z = z_ref[rows, :]
            h = jnp.sin(fr * (jnp.dot(z, w1_ref[...], precision=HIGHEST, preferred_element_type=F32) + b1))
            h = jnp.sin(fr * (jnp.dot(h, w2_ref[...], precision=HIGHEST, preferred_element_type=F32) + b2))
            h_scr[rows, :] = jnp.sin(
                fr * (jnp.dot(h, w3_ref[...], precision=HIGHEST, preferred_element_type=F32) + b3))
            return carry

        lax.fori_loop(0, n_rows // R, mlp, 0)

    def body(r, acc):
        rows = pl.ds(pl.multiple_of(r * R, R), R)
        z = z_ref[rows, :]
        h = h_scr[rows, :]
        hf = jnp.dot(h, w4f_ref[...], precision=HIGHEST, preferred_element_type=F32)
        hb = jnp.dot(h, w4b_ref[...], precision=HIGHEST, preferred_element_type=F32)
        fwd = z[:, HY_EMB:HY_EMB + 1]
        valid = z[:, HY_EMB + 1:HY_EMB + 2]
        k = jnp.where(fwd > 0.5, hf, hb) * jnp.exp(-z[:, 0:1] * dl) * valid
        k_ref[rows, :] = k
        return acc + jnp.sum(jnp.abs(k), axis=0, keepdims=True)

    tot = lax.fori_loop(0, n_rows // R, body, jnp.zeros((1, LANES), F32))

    def scale(r, carry):
        rows = pl.ds(pl.multiple_of(r * R, R), R)
        k_ref[rows, :] = k_ref[rows, :] / tot
        return carry

    lax.fori_loop(0, n_rows // R, scale, 0)


def _hyena_filter(zrows, w1, b1, freq, w2, b2, w3, b3, w4):
    n_rows = zrows.shape[0]
    w1p = jnp.concatenate([w1, jnp.zeros((HY_WIDTH - HY_EMB, HY_WIDTH), F32)], axis=0)
    vec = jnp.concatenate([b1[None], b2[None], b3[None], freq[None],
                           jnp.zeros((SUBLANES - 4, HY_WIDTH), F32)], axis=0)
    deltas = jnp.abs(jnp.linspace(HY_MIN_DECAY, HY_MAX_DECAY, D_B, dtype=F32))[None, :]
    nb = D_B // LANES

    def const(shape):
        return pl.BlockSpec(shape, lambda c, nd=len(shape): (0,) * nd)

    return pl.pallas_call(
        functools.partial(_filter_kernel, n_rows=n_rows),
        grid=(nb,),
        in_specs=[const((n_rows, HY_WIDTH)), const((HY_WIDTH, HY_WIDTH)), const((HY_WIDTH, HY_WIDTH)),
                  const((HY_WIDTH, HY_WIDTH)), const((SUBLANES, HY_WIDTH)),
                  pl.BlockSpec((HY_WIDTH, LANES), lambda c: (0, c)),
                  pl.BlockSpec((HY_WIDTH, LANES), lambda c: (0, nb + c)),
                  pl.BlockSpec((1, LANES), lambda c: (0, c))],
        out_specs=pl.BlockSpec((n_rows, LANES), lambda c: (0, c)),
        out_shape=jax.ShapeDtypeStruct((n_rows, D_B), F32),
        scratch_shapes=[pltpu.VMEM((n_rows, HY_WIDTH), F32)],
        compiler_params=_cparams("arbitrary"),
        name="hyena_filter",
    )(zrows, w1p, w2, w3, vec, w4, w4, deltas)


def _cplx_block(gr, gi):
    return np.block([[gr, -gi], [gi, gr]])


def _hi_lo(a):
    a = jnp.asarray(a, F32)
    hi = a.astype(BF16)
    return hi, (a - hi.astype(F32)).astype(BF16)


def _dft_consts(L):
    N = 2 * L
    n2 = FFT_MINOR
    n1 = N // n2
    half = n1 // 2
    f1 = np.arange(n1)
    t2 = np.arange(n2)
    t1 = np.arange(n1)
    tt = n2 * t1[None, None, :] + t2[:, None, None]
    ang = -2.0 * np.pi * ((f1[None, :, None] * tt) % N) / N
    gr, gi = np.cos(ang), np.sin(ang)
    m1_data = np.stack([_cplx_block(gr[j][:, :half], gi[j][:, :half]) for j in range(n2)])
    m1_filt = np.concatenate([gr, gi], axis=1)
    m1_inv = np.stack([_cplx_block(gr[j][:, :half].T, -gi[j][:, :half].T) for j in range(n2)]) / N
    ang2 = -2.0 * np.pi * ((t2[:, None] * t2[None, :]) % n2) / n2
    f2 = _cplx_block(np.cos(ang2), np.sin(ang2))
    f2_inv = _cplx_block(np.cos(ang2), -np.sin(ang2))
    return dict(n1=n1, n2=n2, m1_data=_hi_lo(m1_data), m1_filt=_hi_lo(m1_filt), m1_inv=_hi_lo(m1_inv),
                f2=_hi_lo(f2[None]), f2_inv=_hi_lo(f2_inv[None]))


def _dft_single_consts(L):
    N = 2 * L
    f = np.arange(N)
    ang = -2.0 * np.pi * ((f[:, None] * f[None, :]) % N) / N
    gr, gi = np.cos(ang), np.sin(ang)
    fwd = _cplx_block(gr[:, :L], gi[:, :L])
    filt = np.concatenate([gr, gi], axis=0)
    inv = _cplx_block(gr[:L, :], -gi[:L, :]) / N
    return dict(fwd=_hi_lo(fwd[None]), filt=_hi_lo(filt[None]), inv=_hi_lo(inv[None]))


def _bmm_kernel(mh_ref, ml_ref, x_ref, o_ref, *, gblk, shared, precise):
    for j in range(gblk):
        jm = 0 if shared else j
        x = x_ref[j]
        xh = x.astype(BF16)
        acc = jnp.dot(mh_ref[jm], xh, preferred_element_type=F32)
        if precise:
            xl = (x - xh.astype(F32)).astype(BF16)
            acc = (acc + jnp.dot(mh_ref[jm], xl, preferred_element_type=F32)
                   + jnp.dot(ml_ref[jm], xh, preferred_element_type=F32))
        o_ref[j] = acc.astype(o_ref.dtype)


def _bmm_left(m, x, gblk, precise, out_dtype):
    mh, ml = m
    P, G, K, C = x.shape
    R = mh.shape[1]
    shared = mh.shape[0] == 1
    gblk = min(gblk, G)
    m_spec = (pl.BlockSpec((1, R, K), lambda g, p: (0, 0, 0)) if shared
              else pl.BlockSpec((gblk, R, K), lambda g, p: (g, 0, 0)))
    return pl.pallas_call(
        functools.partial(_bmm_kernel, gblk=gblk, shared=shared, precise=precise),
        grid=(G // gblk, P),
        in_specs=[m_spec, m_spec, pl.BlockSpec((None, gblk, K, C), lambda g, p: (p, g, 0, 0))],
        out_specs=pl.BlockSpec((None, gblk, R, C), lambda g, p: (p, g, 0, 0)),
        out_shape=jax.ShapeDtypeStruct((P, G, R, C), out_dtype),
        compiler_params=_cparams("arbitrary", "arbitrary"),
        name="dft_stage",
    )(mh, ml, x)


def _mid_kernel(fa_ref, fb_ref, kf_ref, d_ref, o_ref, *, gblk, nf):
    fa = fa_ref[...]
    fb = fb_ref[...]
    for j in range(gblk):
        xx = jnp.dot(fa, d_ref[j].astype(BF16), preferred_element_type=F32)
        xr, xi = xx[:nf], xx[nf:]
        kr, ki = kf_ref[j, :nf], kf_ref[j, nf:]
        yy = jnp.concatenate([xr * kr - xi * ki, xr * ki + xi * kr], axis=0)
        o_ref[j] = jnp.dot(fb, yy.astype(BF16), preferred_element_type=F32).astype(o_ref.dtype)


def _spectral_mid(fa, fb, kf, d, gblk, out_dtype):
    P, G, K, C = d.shape
    nf2 = fa.shape[0]
    Ko = fb.shape[0]
    gblk = min(gblk, G)
    return pl.pallas_call(
        functools.partial(_mid_kernel, gblk=gblk, nf=nf2 // 2),
        grid=(G // gblk, P),
        in_specs=[pl.BlockSpec(fa.shape, lambda g, p: (0, 0)),
                  pl.BlockSpec(fb.shape, lambda g, p: (0, 0)),
                  pl.BlockSpec((gblk, nf2, C), lambda g, p: (g, 0, 0)),
                  pl.BlockSpec((None, gblk, K, C), lambda g, p: (p, g, 0, 0))],
        out_specs=pl.BlockSpec((None, gblk, Ko, C), lambda g, p: (p, g, 0, 0)),
        out_shape=jax.ShapeDtypeStruct((P, G, Ko, C), out_dtype),
        compiler_params=_cparams("arbitrary", "arbitrary"),
        name="dft_mid",
    )(fa, fb, kf, d)


def _long_conv_two_stage(vin, filt_rows, dc):
    B, L, C = vin.shape
    n1, n2 = dc["n1"], dc["n2"]
    half = n1 // 2
    P = B // 2
    ka = _bmm_left(dc["m1_filt"], filt_rows.reshape(1, n2, n1, C), 8, True, F32)
    ka = ka.reshape(n2, 2, n1, C).transpose(2, 1, 0, 3).reshape(1, n1, 2 * n2, C)
    kf = _bmm_left(dc["f2"], ka, 8, True, F32)[0]
    x = vin.reshape(2, P, half, n2, C).transpose(1, 3, 0, 2, 4).reshape(P, n2, n1, C)
    a = _bmm_left(dc["m1_data"], x, 8, False, BF16)
    a = a.reshape(P, n2, 2, n1, C).transpose(0, 3, 2, 1, 4).reshape(P, n1, 2 * n2, C)
    z = _spectral_mid(dc["f2"][0][0], dc["f2_inv"][0][0], kf, a, 8, BF16)
    z = z.reshape(P, n1, 2, n2, C).transpose(0, 3, 2, 1, 4).reshape(P, n2, 2 * n1, C)
    y = _bmm_left(dc["m1_inv"], z, 8, False, F32)
    return y.reshape(P, n2, 2, half, C).transpose(2, 0, 3, 1, 4).reshape(B, L, C)


def _long_conv_single(vin, filt, dc):
    B, L, C = vin.shape
    P = B // 2
    kf = _bmm_left(dc["filt"], filt.reshape(1, 1, 2 * L, C), 1, True, F32)[0]
    d = vin.reshape(2, P, L, C).transpose(1, 0, 2, 3).reshape(P, 1, 2 * L, C)
    y = _spectral_mid(dc["fwd"][0][0], dc["inv"][0][0], kf, d, 1, F32)
    return y.reshape(P, 2, L, C).transpose(1, 0, 2, 3).reshape(B, L, C)


def _outproj_kernel(*refs, final):
    if final:
        oa_ref, y_ref, vin_ref, m_ref, hb_ref, x_ref, gt_ref, w_ref, fw_ref, o_ref = refs
    else:
        oa_ref, y_ref, vin_ref, m_ref, hb_ref, x_ref, gt_ref, w_ref, o_ref = refs
    vin = vin_ref[...]
    ob = (y_ref[...] + vin * hb_ref[...]) * m_ref[...]
    r = (jnp.dot(oa_ref[...].astype(BF16), w_ref[0:D_A, :], preferred_element_type=F32)
         + jnp.dot(ob.astype(BF16), w_ref[D_A:D_A + D_B, :], preferred_element_type=F32))
    xn = x_ref[...] + gt_ref[...] * r
    if final:
        ms = jnp.mean(xn * xn, axis=-1, keepdims=True)
        xn = xn * lax.rsqrt(ms + EPS) * fw_ref[...]
    o_ref[...] = xn


def _out_proj(oa, y, vin, m, hy_bias, x, gt, w_bf16, final_w):
    B, L, D = x.shape
    tm = min(L, 512)
    final = final_w is not None
    half = lambda: pl.BlockSpec((None, tm, D_B), lambda b, i: (b, i, 0))
    in_specs = [half(), half(), half(), half(),
                pl.BlockSpec((1, D_B), lambda b, i: (0, 0)),
                pl.BlockSpec((None, tm, D), lambda b, i: (b, i, 0)),
                pl.BlockSpec((None, 1, D), lambda b, i: (b, 0, 0)),
                pl.BlockSpec((D, D), lambda b, i: (0, 0))]
    args = [oa, y, vin, m, hy_bias.reshape(1, D_B), x, gt, w_bf16]
    if final:
        in_specs.append(pl.BlockSpec((1, D), lambda b, i: (0, 0)))
        args.append(final_w.reshape(1, D))
    return pl.pallas_call(
        functools.partial(_outproj_kernel, final=final),
        grid=(B, L // tm),
        in_specs=in_specs,
        out_specs=pl.BlockSpec((None, tm, D), lambda b, i: (b, i, 0)),
        out_shape=jax.ShapeDtypeStruct((B, L, D), F32),
        compiler_params=_cparams("arbitrary", "arbitrary"),
        name="out_proj",
    )(*args)


def _scan_params(lb, g_norm_w):
    rows = []
    for d in range(2):
        rows += [jnp.log(lb[d]), jnp.log1p(-lb[d]), 1.0 - lb[d]]
    rows += [g_norm_w, jnp.zeros_like(g_norm_w)]
    par = jnp.stack(rows, axis=0)
    return par.reshape(SUBLANES, A_HEADS, HEAD_DIM).transpose(1, 0, 2)


def kernel(x, c, ctx, c_ctx, norm_w, w_ada, b_ada, w_in, w_out, lb_logits, g_norm_w,
           conv_w, conv_b, hy_w1, hy_b1, hy_freq, hy_w2, hy_b2, hy_w3, hy_b3, hy_w4,
           hy_bias, final_norm_w):
    B, L_lat, D = x.shape
    L_ctx = ctx.shape[1]
    p_lb = jax.nn.softmax(lb_logits.astype(F32), axis=0)
    lbs = jnp.cumsum(p_lb, axis=0)
    lbs = lbs - lbs[0:1]

    n_rows = 2 * SUBLANES
    cc = jnp.zeros((n_rows, D), F32).at[:B].set(c).at[B].set(c_ctx)
    mod = _modulation(cc, w_ada, b_ada)

    dc_lat = _dft_consts(L_lat)
    dc_ctx = _dft_single_consts(L_ctx)
    n1, n2 = dc_lat["n1"], dc_lat["n2"]
    order_lat = (np.arange(n2)[:, None] + n2 * np.arange(n1)[None, :]).reshape(-1)
    z_lat = _pos_rows(L_lat, order_lat)
    z_ctx = _pos_rows(L_ctx, np.arange(2 * L_ctx))
    zero_state = jnp.zeros((B, A_HEADS, HEAD_DIM, HEAD_DIM), F32)

    for l in range(DEPTH):
        last = l == DEPTH - 1
        sh_x, sc_x, gt_x = [mod[l, :B, j * D:(j + 1) * D].reshape(B, 1, D) for j in range(3)]
        sh_c, sc_c, gt_c = [jnp.broadcast_to(mod[l, B, j * D:(j + 1) * D].reshape(1, 1, D), (B, 1, D))
                            for j in range(3)]
        w_in_l = w_in[l].astype(BF16)
        w_out_l = w_out[l].astype(BF16)
        par = _scan_params(lbs[l], g_norm_w[l])
        conv_par = jnp.concatenate([conv_w[l], conv_b[l][None],
                                    jnp.zeros((SUBLANES - 4, 3 * D_B), F32)], axis=0)
        filt_args = (hy_w1[l], hy_b1[l], hy_freq[l], hy_w2[l], hy_b2[l], hy_w3[l], hy_b3[l], hy_w4[l])

        if last:
            pc = _in_proj(ctx, norm_w[l], sc_c, sh_c, w_in_l[:, :4 * D_A], 4 * D_A // 2)
            _, s_f, s_b = _hgrn(pc, par, zero_state, zero_state, with_out=False)
        else:
            pc = _in_proj(ctx, norm_w[l], sc_c, sh_c, w_in_l, N_IN // 3)
            oa_c, s_f, s_b = _hgrn(pc, par, zero_state, zero_state, with_out=True)
            vin_c, m_c = _hyena_pre(pc, conv_par)
            filt_c = _hyena_filter(z_ctx, *filt_args)
            y_c = _long_conv_single(vin_c, filt_c, dc_ctx)
            ctx = _out_proj(oa_c, y_c, vin_c, m_c, hy_bias[l], ctx, gt_c, w_out_l, None)

        px = _in_proj(x, norm_w[l], sc_x, sh_x, w_in_l, N_IN // 3)
        oa, _, _ = _hgrn(px, par, s_f, s_b, with_out=True)
        vin, m = _hyena_pre(px, conv_par)
        filt_x = _hyena_filter(z_lat, *filt_args)
        y = _long_conv_two_stage(vin, filt_x, dc_lat)
        x = _out_proj(oa, y, vin, m, hy_bias[l], x, gt_x, w_out_l, final_norm_w if last else None)

    return x
```

```python
import functools
import math

import numpy as np
import jax
import jax.numpy as jnp
from jax import lax
from jax.experimental import pallas as pl
from jax.experimental.pallas import tpu as pltpu

F32 = jnp.float32
BF16 = jnp.bfloat16
HIGHEST = lax.Precision.HIGHEST

D_MODEL = 1024
DEPTH = 2
D_A = 512
D_B = 512
A_HEADS = 4
HEAD_DIM = 128
N_IN = 5 * D_A + 4 * D_B
HY_EMB = 33
HY_BANDS = 16
HY_WIDTH = 64
HY_MIN_DECAY = math.log(1e-2) / 1.5
HY_MAX_DECAY = math.log(1e-2) / 0.3
EPS = 1e-6

LANES = 128
SUBLANES = 8
SCAN_CHUNK = 64
SCAN_UNROLL = 8
SCAN_GUARD = 80.0
FFT_MINOR = 64
VMEM_LIMIT = 56 * 1024 * 1024


def _cparams(*sem):
    return pltpu.CompilerParams(dimension_semantics=sem, vmem_limit_bytes=VMEM_LIMIT)


def _mod_kernel(c_ref, w_ref, b_ref, o_ref):
    cc = c_ref[...]
    s = cc * jax.nn.sigmoid(cc)
    o_ref[...] = jnp.dot(s, w_ref[...], precision=HIGHEST,
                         preferred_element_type=F32) + b_ref[...]


def _modulation(cc, w_ada, b_ada):
    R, D = cc.shape
    N = w_ada.shape[-1]
    tn = 768
    return pl.pallas_call(
        _mod_kernel,
        grid=(DEPTH, N // tn),
        in_specs=[
            pl.BlockSpec((R, D), lambda l, j: (0, 0)),
            pl.BlockSpec((None, D, tn), lambda l, j: (l, 0, j)),
            pl.BlockSpec((None, 1, tn), lambda l, j: (l, 0, j)),
        ],
        out_specs=pl.BlockSpec((None, R, tn), lambda l, j: (l, 0, j)),
        out_shape=jax.ShapeDtypeStruct((DEPTH, R, N), F32),
        compiler_params=_cparams("arbitrary", "arbitrary"),
        name="modulation",
    )(cc, w_ada, b_ada.reshape(DEPTH, 1, N))


def _inproj_kernel(x_ref, nw_ref, sc_ref, sh_ref, w_ref, o_ref, hx_ref, *, tn):
    x = x_ref[...]
    ms = jnp.mean(x * x, axis=-1, keepdims=True)
    y = x * lax.rsqrt(ms + EPS) * nw_ref[...]
    hx_ref[...] = (y * (1.0 + sc_ref[...]) + sh_ref[...]).astype(BF16)
    for n0 in range(0, o_ref.shape[-1], tn):
        o_ref[:, n0:n0 + tn] = jnp.dot(hx_ref[...], w_ref[:, n0:n0 + tn], preferred_element_type=F32)


def _in_proj(x, norm_w, sc, sh, w_bf16, tn):
    B, L, D = x.shape
    N = w_bf16.shape[1]
    tm = min(L, 512)
    return pl.pallas_call(
        functools.partial(_inproj_kernel, tn=tn),
        grid=(B, L // tm),
        in_specs=[
            pl.BlockSpec((None, tm, D), lambda b, i: (b, i, 0)),
            pl.BlockSpec((1, D), lambda b, i: (0, 0)),
            pl.BlockSpec((None, 1, D), lambda b, i: (b, 0, 0)),
            pl.BlockSpec((None, 1, D), lambda b, i: (b, 0, 0)),
            pl.BlockSpec((D, N), lambda b, i: (0, 0)),
        ],
        out_specs=pl.BlockSpec((None, tm, N), lambda b, i: (b, i, 0)),
        out_shape=jax.ShapeDtypeStruct((B, L, N), F32),
        scratch_shapes=[pltpu.VMEM((tm, D), BF16)],
        compiler_params=_cparams("arbitrary", "arbitrary"),
        name="in_proj",
    )(x, norm_w.reshape(1, D), sc, sh, w_bf16)


def _scan_consts(C, forward):
    idx = np.arange(C)
    i = idx[:, None]
    t = idx[None, :]
    if forward:
        mats = [t <= i, t > i]
    else:
        mats = [t >= i, t < i]
    masks = [i == t]
    h = 1
    while h < C:
        P = 2 * h
        p = i % P
        m = i - p + h
        upper = p >= h
        same = (i // P) == (t // P)
        if forward:
            mats.append(np.where(upper, (t >= m) & (t <= i), (t >= i + 1) & (t <= m - 1)))
            masks.append(same & upper & ((t % P) < h))
        else:
            mats.append(np.where(upper, (t >= m) & (t <= i - 1), (t >= i) & (t <= m - 1)))
            masks.append(same & (~upper) & ((t % P) >= h))
        h = P
    big = np.concatenate([m_.astype(np.float32) for m_ in mats], axis=0)
    big = np.concatenate([big, big, big], axis=1)
    msk = np.stack([m_.astype(np.float32) for m_ in masks], axis=0)
    return big, msk


def _dot_nt(a, b):
    return lax.dot_general(a, b, (((1,), (1,)), ((), ())), preferred_element_type=F32)


def _dot_tn(a, b):
    return lax.dot_general(a, b, (((0,), (0,)), ((), ())), preferred_element_type=F32)


def _split3(a):
    hi = a.astype(BF16)
    r1 = a - hi.astype(F32)
    mid = r1.astype(BF16)
    lo = (r1 - mid.astype(F32)).astype(BF16)
    return hi, mid, lo


def _gates(f, par):
    loglb = par[0:1, :]
    log1mlb = par[1:2, :]
    onemlb = par[2:3, :]
    s1 = jnp.log(1.0 + jnp.exp(-jnp.abs(f)))
    b = log1mlb + (jnp.minimum(f, 0.0) - s1)
    ng = -jnp.maximum(loglb, b) - jnp.log(1.0 + jnp.exp(-jnp.abs(loglb - b)))
    k = onemlb * jnp.exp(-(jnp.maximum(f, 0.0) + s1))
    return ng, k


def _cumulative(tri_ref, ng):
    return jnp.dot(tri_ref[...], jnp.concatenate(_split3(ng), axis=0), preferred_element_type=F32)


def _fast_stage_sums(q, f, v, par, tri_ref):
    ng, k = _gates(f, par)
    return dict(q=q, k=k, vb=v.astype(BF16), cum=_cumulative(tri_ref, ng))


def _fast_stage_scores(s, C, forward):
    h = C // 2
    q, k, cum = s["q"], s["k"], s["cum"]
    if forward:
        early, late = slice(0, h), slice(h, C)
        ref, tot = cum[h - 1:h, :], cum[C - 1:C, :]
    else:
        early, late = slice(h, C), slice(0, h)
        ref, tot = cum[h:h + 1, :], cum[0:1, :]
    qd = (q * jnp.exp(-cum)).astype(BF16)
    dl = cum - ref
    k_early = (k[early] * jnp.exp(cum[early])).astype(BF16)
    q_late = (q[late] * jnp.exp(-dl[late])).astype(BF16)
    k_all = (k * jnp.exp(dl)).astype(BF16)
    kd = (k * jnp.exp(cum - tot)).astype(BF16)
    return dict(vb=s["vb"], qd=qd, kd=kd, e_all=jnp.exp(-tot),
                guard=jnp.maximum(jnp.max(cum[early]), jnp.max(dl[late])),
                s_early=_dot_nt(qd[early], k_early), s_late=_dot_nt(q_late, k_all))


def _fast_stage_intra(s, C, forward):
    h = C // 2
    early = slice(0, h) if forward else slice(h, C)
    ri = lax.broadcasted_iota(jnp.int32, (h, h), 0)
    ci = lax.broadcasted_iota(jnp.int32, (h, h), 1)
    rl = lax.broadcasted_iota(jnp.int32, (h, C), 0) + (h if forward else 0)
    cl = lax.broadcasted_iota(jnp.int32, (h, C), 1)
    keep_e = (ci <= ri) if forward else (ci >= ri)
    keep_l = (cl <= rl) if forward else (cl >= rl)
    s_early = jnp.where(keep_e, s["s_early"], 0.0).astype(BF16)
    s_late = jnp.where(keep_l, s["s_late"], 0.0).astype(BF16)
    o_early = jnp.dot(s_early, s["vb"][early], preferred_element_type=F32)
    o_late = jnp.dot(s_late, s["vb"], preferred_element_type=F32)
    o_intra = jnp.concatenate([o_early, o_late] if forward else [o_late, o_early], axis=0)
    return dict(qd=s["qd"], e_all=s["e_all"], o_intra=o_intra, upd=_dot_tn(s["vb"], s["kd"]))


def _fast_stage_state(s, st):
    o = s["o_intra"] + _dot_nt(s["qd"], st.astype(BF16))
    return o, st * s["e_all"] + s["upd"]


def _scan_chunk(q, f, v, par, big_ref, msk_ref, st, C, forward):
    ng, k = _gates(f, par)
    psum = jnp.dot(big_ref[...], jnp.concatenate(_split3(ng), axis=0), preferred_element_type=F32)
    ex = jnp.exp(-psum)
    e_in = ex[0:C]
    e_out = ex[C:2 * C]
    nlev = msk_ref.shape[0] - 1
    o = _dot_nt((q * e_in).astype(BF16), st.astype(BF16))
    scores = msk_ref[0] * _dot_nt(q.astype(BF16), k.astype(BF16))
    for l in range(nlev):
        el = ex[(2 + l) * C:(3 + l) * C]
        scores = scores + msk_ref[1 + l] * _dot_nt((q * el).astype(BF16), (k * el).astype(BF16))
    o = o + jnp.dot(scores.astype(BF16), v.astype(BF16), preferred_element_type=F32)
    e_all = e_in[C - 1:C, :] if forward else e_in[0:1, :]
    st_new = st * e_all + _dot_tn(v.astype(BF16), (k * e_out).astype(BF16))
    return o, st_new


def _hgrn_kernel(*refs, L, C, U, with_out):
    if with_out:
        (q_ref, ff_ref, fb_ref, iv_ref, ga_ref, par_ref, bigf_ref, mskf_ref, bigb_ref, mskb_ref,
         s0f_ref, s0b_ref, o_ref, sf_ref, sb_ref, sf_in, sb_in, of_scr, ob_scr) = refs
    else:
        (q_ref, ff_ref, fb_ref, iv_ref, par_ref, bigf_ref, mskf_ref, bigb_ref, mskb_ref,
         s0f_ref, s0b_ref, sf_ref, sb_ref, sf_in, sb_in) = refs
    n = L // C
    sf_ref[...] = s0f_ref[...]
    sb_ref[...] = s0b_ref[...]
    par = par_ref[...]
    trif_ref = bigf_ref.at[0:C, :]
    trib_ref = bigb_ref.at[0:C, :]

    def rows_of(it, u):
        c = it * U + u
        return (pl.ds(pl.multiple_of(c * C, C), C), pl.ds(pl.multiple_of((n - 1 - c) * C, C), C))

    def silu_q(rows):
        q = q_ref[rows, :]
        return q * jax.nn.sigmoid(q)

    def put(rf, rb, o_f, o_b):
        if with_out:
            of_scr[rf, :] = o_f
            ob_scr[rb, :] = o_b

    def body(it, carry):
        sf_in[...] = sf_ref[...]
        sb_in[...] = sb_ref[...]
        rows = [rows_of(it, u) for u in range(U)]
        fwd = [_fast_stage_sums(silu_q(rf), ff_ref[rf, :], iv_ref[rf, :], par[0:3], trif_ref)
               for rf, _ in rows]
        bwd = [_fast_stage_sums(silu_q(rb), fb_ref[rb, :], iv_ref[rb, :], par[3:6], trib_ref)
               for _, rb in rows]
        fwd = [_fast_stage_scores(s, C, True) for s in fwd]
        bwd = [_fast_stage_scores(s, C, False) for s in bwd]
        guard = functools.reduce(jnp.maximum, [s["guard"] for s in fwd + bwd])
        fwd = [_fast_stage_intra(s, C, True) for s in fwd]
        bwd = [_fast_stage_intra(s, C, False) for s in bwd]
        st_f, st_b = sf_ref[...], sb_ref[...]
        for u in range(U):
            o_f, st_f = _fast_stage_state(fwd[u], st_f)
            o_b, st_b = _fast_stage_state(bwd[u], st_b)
            put(rows[u][0], rows[u][1], o_f, o_b)
        sf_ref[...] = st_f
        sb_ref[...] = st_b

        @pl.when(jnp.logical_not(guard <= SCAN_GUARD))
        def _():
            sf_ref[...] = sf_in[...]
            sb_ref[...] = sb_in[...]

            def redo(u, carry2):
                rf, rb = rows_of(it, u)
                o_f, s_f = _scan_chunk(silu_q(rf), ff_ref[rf, :], iv_ref[rf, :], par[0:3],
                                       bigf_ref, mskf_ref, sf_ref[...], C, True)
                sf_ref[...] = s_f
                o_b, s_b = _scan_chunk(silu_q(rb), fb_ref[rb, :], iv_ref[rb, :], par[3:6],
                                       bigb_ref, mskb_ref, sb_ref[...], C, False)
                sb_ref[...] = s_b
                put(rf, rb, o_f, o_b)
                return carry2

            lax.fori_loop(0, U, redo, 0)

        return carry

    lax.fori_loop(0, n // U, body, 0)

    if with_out:
        gw = par[6:7, :]
        R = min(L, 256)

        def fin(r, carry):
            rows = pl.ds(pl.multiple_of(r * R, R), R)
            o = of_scr[rows, :] + ob_scr[rows, :]
            o = o * lax.rsqrt(jnp.mean(o * o, axis=-1, keepdims=True) + EPS) * gw
            ga = ga_ref[rows, :]
            o_ref[rows, :] = o * (ga * jax.nn.sigmoid(ga))
            return carry

        lax.fori_loop(0, L // R, fin, 0)


def _hgrn(px, par, s0f, s0b, with_out):
    B, L, _ = px.shape
    C = SCAN_CHUNK
    H = A_HEADS
    bigf, mskf = _scan_consts(C, True)
    bigb, mskb = _scan_consts(C, False)
    bigf, bigb = jnp.asarray(bigf, BF16), jnp.asarray(bigb, BF16)
    mskf, mskb = jnp.asarray(mskf, F32), jnp.asarray(mskb, F32)

    def col(j):
        return pl.BlockSpec((None, L, LANES), lambda b, h, j=j: (b, 0, j * H + h))

    def const(a):
        return pl.BlockSpec(a.shape, lambda b, h, nd=a.ndim: (0,) * nd)

    st_spec = pl.BlockSpec((None, None, HEAD_DIM, HEAD_DIM), lambda b, h: (b, h, 0, 0))
    st_shape = jax.ShapeDtypeStruct((B, H, HEAD_DIM, HEAD_DIM), F32)
    in_specs = [col(0), col(1), col(2), col(3)]
    args = [px, px, px, px]
    if with_out:
        in_specs.append(col(4))
        args.append(px)
    in_specs += [pl.BlockSpec((None, SUBLANES, LANES), lambda b, h: (h, 0, 0)),
                 const(bigf), const(mskf), const(bigb), const(mskb), st_spec, st_spec]
    args += [par, bigf, mskf, bigb, mskb, s0f, s0b]
    out_specs = [st_spec, st_spec]
    out_shape = [st_shape, st_shape]
    scratch = [pltpu.VMEM((HEAD_DIM, HEAD_DIM), F32), pltpu.VMEM((HEAD_DIM, HEAD_DIM), F32)]
    if with_out:
        out_specs = [pl.BlockSpec((None, L, LANES), lambda b, h: (b, 0, h))] + out_specs
        out_shape = [jax.ShapeDtypeStruct((B, L, D_A), F32)] + out_shape
        scratch += [pltpu.VMEM((L, LANES), F32), pltpu.VMEM((L, LANES), F32)]
    res = pl.pallas_call(
        functools.partial(_hgrn_kernel, L=L, C=C, U=min(SCAN_UNROLL, L // C), with_out=with_out),
        grid=(B, H),
        in_specs=in_specs,
        out_specs=out_specs,
        out_shape=out_shape,
        scratch_shapes=scratch,
        compiler_params=_cparams("arbitrary", "arbitrary"),
        name="hgrn2",
    )(*args)
    if with_out:
        return res[0], res[1], res[2]
    return None, res[0], res[1]


def _conv3(ref, r0, R, L, cp):
    cur = ref[pl.ds(r0, R), :]
    prev_tile = ref[pl.ds(pl.multiple_of(jnp.maximum(r0 - SUBLANES, 0), SUBLANES), SUBLANES), :]
    next_tile = ref[pl.ds(pl.multiple_of(jnp.minimum(r0 + R, L - SUBLANES), SUBLANES), SUBLANES), :]
    prow = jnp.where(r0 > 0, prev_tile[SUBLANES - 1:SUBLANES, :], 0.0)
    nrow = jnp.where(r0 + R < L, next_tile[0:1, :], 0.0)
    rid = lax.broadcasted_iota(jnp.int32, (R, LANES), 0)
    up = jnp.where(rid == 0, prow, pltpu.roll(cur, 1, 0))
    dn = jnp.where(rid == R - 1, nrow, pltpu.roll(cur, R - 1, 0))
    return cp[3:4, :] + up * cp[0:1, :] + cur * cp[1:2, :] + dn * cp[2:3, :]


def _hyena_pre_kernel(x0_ref, x1_ref, v_ref, gb_ref, c0_ref, c1_ref, c2_ref, vin_ref, m_ref, *, L):
    R = min(L, 256)
    c0 = c0_ref[...]
    c1 = c1_ref[...]
    c2 = c2_ref[...]

    def body(r, carry):
        r0 = pl.multiple_of(r * R, R)
        rows = pl.ds(r0, R)
        x0 = _conv3(x0_ref, r0, R, L, c0)
        x1 = _conv3(x1_ref, r0, R, L, c1)
        vv = _conv3(v_ref, r0, R, L, c2)
        gb = gb_ref[rows, :]
        vin_ref[rows, :] = vv * x1
        m_ref[rows, :] = x0 * (gb * jax.nn.sigmoid(gb))
        return carry

    lax.fori_loop(0, L // R, body, 0)


def _hyena_pre(px, conv_par):
    B, L, _ = px.shape
    nb = D_B // LANES
    base = 5 * D_A // LANES

    def col(j):
        return pl.BlockSpec((None, L, LANES), lambda b, c, j=j: (b, 0, base + j * nb + c))

    def cpar(j):
        return pl.BlockSpec((SUBLANES, LANES), lambda b, c, j=j: (0, j * nb + c))

    out_spec = pl.BlockSpec((None, L, LANES), lambda b, c: (b, 0, c))
    out_shape = jax.ShapeDtypeStruct((B, L, D_B), F32)
    return pl.pallas_call(
        functools.partial(_hyena_pre_kernel, L=L),
        grid=(B, nb),
        in_specs=[col(0), col(1), col(2), col(3), cpar(0), cpar(1), cpar(2)],
        out_specs=[out_spec, out_spec],
        out_shape=[out_shape, out_shape],
        compiler_params=_cparams("arbitrary", "arbitrary"),
        name="hyena_pre",
    )(px, px, px, px, conv_par, conv_par, conv_par)


def _pos_rows(L, order):
    t = jnp.linspace(0.0, 1.0, L, dtype=F32)[:, None]
    w = 2.0 * math.pi * jnp.arange(L, dtype=F32)[:, None] / L
    f = jnp.linspace(1e-4, HY_BANDS - 1, HY_BANDS, dtype=F32)[None, :]
    z = jnp.concatenate([t, jnp.cos(f * w), -jnp.sin(f * w)], axis=-1)
    lag = np.arange(2 * L)
    src = np.where(lag < L, lag, 2 * L - lag) % L
    fwd = (lag < L).astype(np.float32)
    valid = (lag != L).astype(np.float32)
    rows = jnp.concatenate(
        [z[src], jnp.asarray(fwd)[:, None], jnp.asarray(valid)[:, None],
         jnp.zeros((2 * L, HY_WIDTH - HY_EMB - 2), F32)], axis=-1)
    return rows[order]


def _filter_kernel(z_ref, w1_ref, w2_ref, w3_ref, vec_ref, w4f_ref, w4b_ref, dl_ref, k_ref, h_scr,
                   *, n_rows):
    R = min(n_rows, 512)
    dl = dl_ref[...]

    @pl.when(pl.program_id(0) == 0)
    def _():
        vec = vec_ref[...]
        b1, b2, b3, fr = vec[0:1], vec[1:2], vec[2:3], vec[3:4]

        def mlp(r, carry):
            rows = pl.ds(pl.multiple_of(r * R, R), R)
            z = z_ref[rows, :]
            h = jnp.sin(fr * (jnp.dot(z, w1_ref[...], precision=HIGHEST, preferred_element_type=F32) + b1))
            h = jnp.sin(fr * (jnp.dot(h, w2_ref[...], precision=HIGHEST, preferred_element_type=F32) + b2))
            h_scr[rows, :] = jnp.sin(
                fr * (jnp.dot(h, w3_ref[...], precision=HIGHEST, preferred_element_type=F32) + b3))
            return carry

        lax.fori_loop(0, n_rows // R, mlp, 0)

    def body(r, acc):
        rows = pl.ds(pl.multiple_of(r * R, R), R)
        z = z_ref[rows, :]
        h = h_scr[rows, :]
        hf = jnp.dot(h, w4f_ref[...], precision=HIGHEST, preferred_element_type=F32)
        hb = jnp.dot(h, w4b_ref[...], precision=HIGHEST, preferred_element_type=F32)
        fwd = z[:, HY_EMB:HY_EMB + 1]
        valid = z[:, HY_EMB + 1:HY_EMB + 2]
        k = jnp.where(fwd > 0.5, hf, hb) * jnp.exp(-z[:, 0:1] * dl) * valid
        k_ref[rows, :] = k
        return acc + jnp.sum(jnp.abs(k), axis=0, keepdims=True)

    tot = lax.fori_loop(0, n_rows // R, body, jnp.zeros((1, LANES), F32))

    def scale(r, carry):
        rows = pl.ds(pl.multiple_of(r * R, R), R)
        k_ref[rows, :] = k_ref[rows, :] / tot
        return carry

    lax.fori_loop(0, n_rows // R, scale, 0)


def _hyena_filter(zrows, w1, b1, freq, w2, b2, w3, b3, w4):
    n_rows = zrows.shape[0]
    w1p = jnp.concatenate([w1, jnp.zeros((HY_WIDTH - HY_EMB, HY_WIDTH), F32)], axis=0)
    vec = jnp.concatenate([b1[None], b2[None], b3[None], freq[None],
                           jnp.zeros((SUBLANES - 4, HY_WIDTH), F32)], axis=0)
    deltas = jnp.abs(jnp.linspace(HY_MIN_DECAY, HY_MAX_DECAY, D_B, dtype=F32))[None, :]
    nb = D_B // LANES

    def const(shape):
        return pl.BlockSpec(shape, lambda c, nd=len(shape): (0,) * nd)

    return pl.pallas_call(
        functools.partial(_filter_kernel, n_rows=n_rows),
        grid=(nb,),
        in_specs=[const((n_rows, HY_WIDTH)), const((HY_WIDTH, HY_WIDTH)), const((HY_WIDTH, HY_WIDTH)),
                  const((HY_WIDTH, HY_WIDTH)), const((SUBLANES, HY_WIDTH)),
                  pl.BlockSpec((HY_WIDTH, LANES), lambda c: (0, c)),
                  pl.BlockSpec((HY_WIDTH, LANES), lambda c: (0, nb + c)),
                  pl.BlockSpec((1, LANES), lambda c: (0, c))],
        out_specs=pl.BlockSpec((n_rows, LANES), lambda c: (0, c)),
        out_shape=jax.ShapeDtypeStruct((n_rows, D_B), F32),
        scratch_shapes=[pltpu.VMEM((n_rows, HY_WIDTH), F32)],
        compiler_params=_cparams("arbitrary"),
        name="hyena_filter",
    )(zrows, w1p, w2, w3, vec, w4, w4, deltas)


def _cplx_block(gr, gi):
    return np.block([[gr, -gi], [gi, gr]])


def _hi_lo(a):
    a = jnp.asarray(a, F32)
    hi = a.astype(BF16)
    return hi, (a - hi.astype(F32)).astype(BF16)


def _dft_consts(L):
    N = 2 * L
    n2 = FFT_MINOR
    n1 = N // n2
    half = n1 // 2
    f1 = np.arange(n1)
    t2 = np.arange(n2)
    t1 = np.arange(n1)
    tt = n2 * t1[None, None, :] + t2[:, None, None]
    ang = -2.0 * np.pi * ((f1[None, :, None] * tt) % N) / N
    gr, gi = np.cos(ang), np.sin(ang)
    m1_data = np.stack([_cplx_block(gr[j][:, :half], gi[j][:, :half]) for j in range(n2)])
    m1_filt = np.concatenate([gr, gi], axis=1)
    m1_inv = np.stack([_cplx_block(gr[j][:, :half].T, -gi[j][:, :half].T) for j in range(n2)]) / N
    ang2 = -2.0 * np.pi * ((t2[:, None] * t2[None, :]) % n2) / n2
    f2 = _cplx_block(np.cos(ang2), np.sin(ang2))
    f2_inv = _cplx_block(np.cos(ang2), -np.sin(ang2))
    return dict(n1=n1, n2=n2, m1_data=_hi_lo(m1_data), m1_filt=_hi_lo(m1_filt), m1_inv=_hi_lo(m1_inv),
                f2=_hi_lo(f2[None]), f2_inv=_hi_lo(f2_inv[None]))


def _dft_single_consts(L):
    N = 2 * L
    f = np.arange(N)
    ang = -2.0 * np.pi * ((f[:, None] * f[None, :]) % N) / N
    gr, gi = np.cos(ang), np.sin(ang)
    fwd = _cplx_block(gr[:, :L], gi[:, :L])
    filt = np.concatenate([gr, gi], axis=0)
    inv = _cplx_block(gr[:L, :], -gi[:L, :]) / N
    return dict(fwd=_hi_lo(fwd[None]), filt=_hi_lo(filt[None]), inv=_hi_lo(inv[None]))


def _bmm_kernel(mh_ref, ml_ref, x_ref, o_ref, *, gblk, shared, precise):
    for j in range(gblk):
        jm = 0 if shared else j
        x = x_ref[j]
        xh = x.astype(BF16)
        acc = jnp.dot(mh_ref[jm], xh, preferred_element_type=F32)
        if precise:
            xl = (x - xh.astype(F32)).astype(BF16)
            acc = (acc + jnp.dot(mh_ref[jm], xl, preferred_element_type=F32)
                   + jnp.dot(ml_ref[jm], xh, preferred_element_type=F32))
        o_ref[j] = acc.astype(o_ref.dtype)


def _bmm_left(m, x, gblk, precise, out_dtype):
    mh, ml = m
    P, G, K, C = x.shape
    R = mh.shape[1]
    shared = mh.shape[0] == 1
    gblk = min(gblk, G)
    m_spec = (pl.BlockSpec((1, R, K), lambda g, p: (0, 0, 0)) if shared
              else pl.BlockSpec((gblk, R, K), lambda g, p: (g, 0, 0)))
    return pl.pallas_call(
        functools.partial(_bmm_kernel, gblk=gblk, shared=shared, precise=precise),
        grid=(G // gblk, P),
        in_specs=[m_spec, m_spec, pl.BlockSpec((None, gblk, K, C), lambda g, p: (p, g, 0, 0))],
        out_specs=pl.BlockSpec((None, gblk, R, C), lambda g, p: (p, g, 0, 0)),
        out_shape=jax.ShapeDtypeStruct((P, G, R, C), out_dtype),
        compiler_params=_cparams("arbitrary", "arbitrary"),
        name="dft_stage",
    )(mh, ml, x)


def _mid_kernel(fa_ref, fb_ref, kf_ref, d_ref, o_ref, *, gblk, nf):
    fa = fa_ref[...]
    fb = fb_ref[...]
    for j in range(gblk):
        xx = jnp.dot(fa, d_ref[j].astype(BF16), preferred_element_type=F32)
        xr, xi = xx[:nf], xx[nf:]
        kr, ki = kf_ref[j, :nf], kf_ref[j, nf:]
        yy = jnp.concatenate([xr * kr - xi * ki, xr * ki + xi * kr], axis=0)
        o_ref[j] = jnp.dot(fb, yy.astype(BF16), preferred_element_type=F32).astype(o_ref.dtype)


def _spectral_mid(fa, fb, kf, d, gblk, out_dtype):
    P, G, K, C = d.shape
    nf2 = fa.shape[0]
    Ko = fb.shape[0]
    gblk = min(gblk, G)
    return pl.pallas_call(
        functools.partial(_mid_kernel, gblk=gblk, nf=nf2 // 2),
        grid=(G // gblk, P),
        in_specs=[pl.BlockSpec(fa.shape, lambda g, p: (0, 0)),
                  pl.BlockSpec(fb.shape, lambda g, p: (0, 0)),
                  pl.BlockSpec((gblk, nf2, C), lambda g, p: (g, 0, 0)),
                  pl.BlockSpec((None, gblk, K, C), lambda g, p: (p, g, 0, 0))],
        out_specs=pl.BlockSpec((None, gblk, Ko, C), lambda g, p: (p, g, 0, 0)),
        out_shape=jax.ShapeDtypeStruct((P, G, Ko, C), out_dtype),
        compiler_params=_cparams("arbitrary", "arbitrary"),
        name="dft_mid",
    )(fa, fb, kf, d)


def _long_conv_two_stage(vin, filt_rows, dc):
    B, L, C = vin.shape
    n1, n2 = dc["n1"], dc["n2"]
    half = n1 // 2
    P = B // 2
    ka = _bmm_left(dc["m1_filt"], filt_rows.reshape(1, n2, n1, C), 8, True, F32)
    ka = ka.reshape(n2, 2, n1, C).transpose(2, 1, 0, 3).reshape(1, n1, 2 * n2, C)
    kf = _bmm_left(dc["f2"], ka, 8, True, F32)[0]
    x = vin.reshape(2, P, half, n2, C).transpose(1, 3, 0, 2, 4).reshape(P, n2, n1, C)
    a = _bmm_left(dc["m1_data"], x, 8, False, BF16)
    a = a.reshape(P, n2, 2, n1, C).transpose(0, 3, 2, 1, 4).reshape(P, n1, 2 * n2, C)
    z = _spectral_mid(dc["f2"][0][0], dc["f2_inv"][0][0], kf, a, 8, BF16)
    z = z.reshape(P, n1, 2, n2, C).transpose(0, 3, 2, 1, 4).reshape(P, n2, 2 * n1, C)
    y = _bmm_left(dc["m1_inv"], z, 8, False, F32)
    return y.reshape(P, n2, 2, half, C).transpose(2, 0, 3, 1, 4).reshape(B, L, C)


def _long_conv_single(vin, filt, dc):
    B, L, C = vin.shape
    P = B // 2
    kf = _bmm_left(dc["filt"], filt.reshape(1, 1, 2 * L, C), 1, True, F32)[0]
    d = vin.reshape(2, P, L, C).transpose(1, 0, 2, 3).reshape(P, 1, 2 * L, C)
    y = _spectral_mid(dc["fwd"][0][0], dc["inv"][0][0], kf, d, 1, F32)
    return y.reshape(P, 2, L, C).transpose(1, 0, 2, 3).reshape(B, L, C)


def _outproj_kernel(*refs, final):
    if final:
        oa_ref, y_ref, vin_ref, m_ref, hb_ref, x_ref, gt_ref, w_ref, fw_ref, o_ref = refs
    else:
        oa_ref, y_ref, vin_ref, m_ref, hb_ref, x_ref, gt_ref, w_ref, o_ref = refs
    vin = vin_ref[...]
    ob = (y_ref[...] + vin * hb_ref[...]) * m_ref[...]
    r = (jnp.dot(oa_ref[...].astype(BF16), w_ref[0:D_A, :], preferred_element_type=F32)
         + jnp.dot(ob.astype(BF16), w_ref[D_A:D_A + D_B, :], preferred_element_type=F32))
    xn = x_ref[...] + gt_ref[...] * r
    if final:
        ms = jnp.mean(xn * xn, axis=-1, keepdims=True)
        xn = xn * lax.rsqrt(ms + EPS) * fw_ref[...]
    o_ref[...] = xn


def _out_proj(oa, y, vin, m, hy_bias, x, gt, w_bf16, final_w):
    B, L, D = x.shape
    tm = min(L, 512)
    final = final_w is not None
    half = lambda: pl.BlockSpec((None, tm, D_B), lambda b, i: (b, i, 0))
    in_specs = [half(), half(), half(), half(),
                pl.BlockSpec((1, D_B), lambda b, i: (0, 0)),
                pl.BlockSpec((None, tm, D), lambda b, i: (b, i, 0)),
                pl.BlockSpec((None, 1, D), lambda b, i: (b, 0, 0)),
                pl.BlockSpec((D, D), lambda b, i: (0, 0))]
    args = [oa, y, vin, m, hy_bias.reshape(1, D_B), x, gt, w_bf16]
    if final:
        in_specs.append(pl.BlockSpec((1, D), lambda b, i: (0, 0)))
        args.append(final_w.reshape(1, D))
    return pl.pallas_call(
        functools.partial(_outproj_kernel, final=final),
        grid=(B, L // tm),
        in_specs=in_specs,
        out_specs=pl.BlockSpec((None, tm, D), lambda b, i: (b, i, 0)),
        out_shape=jax.ShapeDtypeStruct((B, L, D), F32),
        compiler_params=_cparams("arbitrary", "arbitrary"),
        name="out_proj",
    )(*args)


def _scan_params(lb, g_norm_w):
    rows = []
    for d in range(2):
        rows += [jnp.log(lb[d]), jnp.log1p(-lb[d]), 1.0 - lb[d]]
    rows += [g_norm_w, jnp.zeros_like(g_norm_w)]
    par = jnp.stack(rows, axis=0)
    return par.reshape(SUBLANES, A_HEADS, HEAD_DIM).transpose(1, 0, 2)


def kernel(x, c, ctx, c_ctx, norm_w, w_ada, b_ada, w_in, w_out, lb_logits, g_norm_w,
           conv_w, conv_b, hy_w1, hy_b1, hy_freq, hy_w2, hy_b2, hy_w3, hy_b3, hy_w4,
           hy_bias, final_norm_w):
    B, L_lat, D = x.shape
    L_ctx = ctx.shape[1]
    p_lb = jax.nn.softmax(lb_logits.astype(F32), axis=0)
    lbs = jnp.cumsum(p_lb, axis=0)
    lbs = lbs - lbs[0:1]

    n_rows = 2 * SUBLANES
    cc = jnp.zeros((n_rows, D), F32).at[:B].set(c).at[B].set(c_ctx)
    mod = _modulation(cc, w_ada, b_ada)

    dc_lat = _dft_consts(L_lat)
    dc_ctx = _dft_single_consts(L_ctx)
    n1, n2 = dc_lat["n1"], dc_lat["n2"]
    order_lat = (np.arange(n2)[:, None] + n2 * np.arange(n1)[None, :]).reshape(-1)
    z_lat = _pos_rows(L_lat, order_lat)
    z_ctx = _pos_rows(L_ctx, np.arange(2 * L_ctx))
    zero_state = jnp.zeros((B, A_HEADS, HEAD_DIM, HEAD_DIM), F32)

    for l in range(DEPTH):
        last = l == DEPTH - 1
        sh_x, sc_x, gt_x = [mod[l, :B, j * D:(j + 1) * D].reshape(B, 1, D) for j in range(3)]
        sh_c, sc_c, gt_c = [jnp.broadcast_to(mod[l, B, j * D:(j + 1) * D].reshape(1, 1, D), (B, 1, D))
                            for j in range(3)]
        w_in_l = w_in[l].astype(BF16)
        w_out_l = w_out[l].astype(BF16)
        par = _scan_params(lbs[l], g_norm_w[l])
        conv_par = jnp.concatenate([conv_w[l], conv_b[l][None],
                                    jnp.zeros((SUBLANES - 4, 3 * D_B), F32)], axis=0)
        filt_args = (hy_w1[l], hy_b1[l], hy_freq[l], hy_w2[l], hy_b2[l], hy_w3[l], hy_b3[l], hy_w4[l])

        if last:
            pc = _in_proj(ctx, norm_w[l], sc_c, sh_c, w_in_l[:, :4 * D_A], 4 * D_A // 2)
            _, s_f, s_b = _hgrn(pc, par, zero_state, zero_state, with_out=False)
        else:
            pc = _in_proj(ctx, norm_w[l], sc_c, sh_c, w_in_l, N_IN // 3)
            oa_c, s_f, s_b = _hgrn(pc, par, zero_state, zero_state, with_out=True)
            vin_c, m_c = _hyena_pre(pc, conv_par)
            filt_c = _hyena_filter(z_ctx, *filt_args)
            y_c = _long_conv_single(vin_c, filt_c, dc_ctx)
            ctx = _out_proj(oa_c, y_c, vin_c, m_c, hy_bias[l], ctx, gt_c, w_out_l, None)

        px = _in_proj(x, norm_w[l], sc_x, sh_x, w_in_l, N_IN // 3)
        oa, _, _ = _hgrn(px, par, s_f, s_b, with_out=True)
        vin, m = _hyena_pre(px, conv_par)
        filt_x = _hyena_filter(z_lat, *filt_args)
        y = _long_conv_two_stage(vin, filt_x, dc_lat)
        x = _out_proj(oa, y, vin, m, hy_bias[l], x, gt_x, w_out_l, final_norm_w if last else None)

    return x
```

```python
import functools
import math

import numpy as np
import jax
import jax.numpy as jnp
from jax import lax
from jax.experimental import pallas as pl
from jax.experimental.pallas import tpu as pltpu

F32 = jnp.float32
BF16 = jnp.bfloat16
HIGHEST = lax.Precision.HIGHEST

D_MODEL = 1024
DEPTH = 2
D_A = 512
D_B = 512
A_HEADS = 4
HEAD_DIM = 128
N_IN = 5 * D_A + 4 * D_B
HY_EMB = 33
HY_BANDS = 16
HY_WIDTH = 64
HY_MIN_DECAY = math.log(1e-2) / 1.5
HY_MAX_DECAY = math.log(1e-2) / 0.3
EPS = 1e-6

LANES = 128
SUBLANES = 8
SCAN_CHUNK = 64
SCAN_UNROLL = 8
SCAN_GUARD = 80.0
FFT_MINOR = 128
VMEM_LIMIT = 56 * 1024 * 1024


def _cparams(*sem):
    return pltpu.CompilerParams(dimension_semantics=sem, vmem_limit_bytes=VMEM_LIMIT)


def _mod_kernel(c_ref, w_ref, b_ref, o_ref):
    cc = c_ref[...]
    s = cc * jax.nn.sigmoid(cc)
    o_ref[...] = jnp.dot(s, w_ref[...], precision=HIGHEST,
                         preferred_element_type=F32) + b_ref[...]


def _modulation(cc, w_ada, b_ada):
    R, D = cc.shape
    N = w_ada.shape[-1]
    tn = 768
    return pl.pallas_call(
        _mod_kernel,
        grid=(DEPTH, N // tn),
        in_specs=[
            pl.BlockSpec((R, D), lambda l, j: (0, 0)),
            pl.BlockSpec((None, D, tn), lambda l, j: (l, 0, j)),
            pl.BlockSpec((None, 1, tn), lambda l, j: (l, 0, j)),
        ],
        out_specs=pl.BlockSpec((None, R, tn), lambda l, j: (l, 0, j)),
        out_shape=jax.ShapeDtypeStruct((DEPTH, R, N), F32),
        compiler_params=_cparams("arbitrary", "arbitrary"),
        name="modulation",
    )(cc, w_ada, b_ada.reshape(DEPTH, 1, N))


def _inproj_kernel(x_ref, nw_ref, sc_ref, sh_ref, w_ref, o_ref, hx_ref, *, tn):
    x = x_ref[...]
    ms = jnp.mean(x * x, axis=-1, keepdims=True)
    y = x * lax.rsqrt(ms + EPS) * nw_ref[...]
    hx_ref[...] = (y * (1.0 + sc_ref[...]) + sh_ref[...]).astype(BF16)
    for n0 in range(0, o_ref.shape[-1], tn):
        o_ref[:, n0:n0 + tn] = jnp.dot(hx_ref[...], w_ref[:, n0:n0 + tn], preferred_element_type=F32)


def _in_proj(x, norm_w, sc, sh, w_bf16, tn):
    B, L, D = x.shape
    N = w_bf16.shape[1]
    tm = min(L, 512)
    return pl.pallas_call(
        functools.partial(_inproj_kernel, tn=tn),
        grid=(B, L // tm),
        in_specs=[
            pl.BlockSpec((None, tm, D), lambda b, i: (b, i, 0)),
            pl.BlockSpec((1, D), lambda b, i: (0, 0)),
            pl.BlockSpec((None, 1, D), lambda b, i: (b, 0, 0)),
            pl.BlockSpec((None, 1, D), lambda b, i: (b, 0, 0)),
            pl.BlockSpec((D, N), lambda b, i: (0, 0)),
        ],
        out_specs=pl.BlockSpec((None, tm, N), lambda b, i: (b, i, 0)),
        out_shape=jax.ShapeDtypeStruct((B, L, N), F32),
        scratch_shapes=[pltpu.VMEM((tm, D), BF16)],
        compiler_params=_cparams("arbitrary", "arbitrary"),
        name="in_proj",
    )(x, norm_w.reshape(1, D), sc, sh, w_bf16)


def _scan_consts(C, forward):
    idx = np.arange(C)
    i = idx[:, None]
    t = idx[None, :]
    if forward:
        mats = [t <= i, t > i]
    else:
        mats = [t >= i, t < i]
    masks = [i == t]
    h = 1
    while h < C:
        P = 2 * h
        p = i % P
        m = i - p + h
        upper = p >= h
        same = (i // P) == (t // P)
        if forward:
            mats.append(np.where(upper, (t >= m) & (t <= i), (t >= i + 1) & (t <= m - 1)))
            masks.append(same & upper & ((t % P) < h))
        else:
            mats.append(np.where(upper, (t >= m) & (t <= i - 1), (t >= i) & (t <= m - 1)))
            masks.append(same & (~upper) & ((t % P) >= h))
        h = P
    big = np.concatenate([m_.astype(np.float32) for m_ in mats], axis=0)
    big = np.concatenate([big, big, big], axis=1)
    msk = np.stack([m_.astype(np.float32) for m_ in masks], axis=0)
    return big, msk


def _dot_nt(a, b):
    return lax.dot_general(a, b, (((1,), (1,)), ((), ())), preferred_element_type=F32)


def _dot_tn(a, b):
    return lax.dot_general(a, b, (((0,), (0,)), ((), ())), preferred_element_type=F32)


def _split3(a):
    hi = a.astype(BF16)
    r1 = a - hi.astype(F32)
    mid = r1.astype(BF16)
    lo = (r1 - mid.astype(F32)).astype(BF16)
    return hi, mid, lo


def _gates(f, par):
    loglb = par[0:1, :]
    log1mlb = par[1:2, :]
    onemlb = par[2:3, :]
    s1 = jnp.log(1.0 + jnp.exp(-jnp.abs(f)))
    b = log1mlb + (jnp.minimum(f, 0.0) - s1)
    ng = -jnp.maximum(loglb, b) - jnp.log(1.0 + jnp.exp(-jnp.abs(loglb - b)))
    k = onemlb * jnp.exp(-(jnp.maximum(f, 0.0) + s1))
    return ng, k


def _cumulative(tri_ref, ng):
    return jnp.dot(tri_ref[...], jnp.concatenate(_split3(ng), axis=0), preferred_element_type=F32)


def _fast_stage_sums(q, f, v, par, tri_ref):
    ng, k = _gates(f, par)
    return dict(q=q, k=k, vb=v.astype(BF16), cum=_cumulative(tri_ref, ng))


def _fast_stage_scores(s, C, forward):
    h = C // 2
    q, k, cum = s["q"], s["k"], s["cum"]
    if forward:
        early, late = slice(0, h), slice(h, C)
        ref, tot = cum[h - 1:h, :], cum[C - 1:C, :]
    else:
        early, late = slice(h, C), slice(0, h)
        ref, tot = cum[h:h + 1, :], cum[0:1, :]
    qd = (q * jnp.exp(-cum)).astype(BF16)
    dl = cum - ref
    k_early = (k[early] * jnp.exp(cum[early])).astype(BF16)
    q_late = (q[late] * jnp.exp(-dl[late])).astype(BF16)
    k_all = (k * jnp.exp(dl)).astype(BF16)
    kd = (k * jnp.exp(cum - tot)).astype(BF16)
    return dict(vb=s["vb"], qd=qd, kd=kd, e_all=jnp.exp(-tot),
                guard=jnp.maximum(jnp.max(cum[early]), jnp.max(dl[late])),
                s_early=_dot_nt(qd[early], k_early), s_late=_dot_nt(q_late, k_all))


def _fast_stage_intra(s, C, forward):
    h = C // 2
    early = slice(0, h) if forward else slice(h, C)
    ri = lax.broadcasted_iota(jnp.int32, (h, h), 0)
    ci = lax.broadcasted_iota(jnp.int32, (h, h), 1)
    rl = lax.broadcasted_iota(jnp.int32, (h, C), 0) + (h if forward else 0)
    cl = lax.broadcasted_iota(jnp.int32, (h, C), 1)
    keep_e = (ci <= ri) if forward else (ci >= ri)
    keep_l = (cl <= rl) if forward else (cl >= rl)
    s_early = jnp.where(keep_e, s["s_early"], 0.0).astype(BF16)
    s_late = jnp.where(keep_l, s["s_late"], 0.0).astype(BF16)
    o_early = jnp.dot(s_early, s["vb"][early], preferred_element_type=F32)
    o_late = jnp.dot(s_late, s["vb"], preferred_element_type=F32)
    o_intra = jnp.concatenate([o_early, o_late] if forward else [o_late, o_early], axis=0)
    return dict(qd=s["qd"], e_all=s["e_all"], o_intra=o_intra, upd=_dot_tn(s["vb"], s["kd"]))


def _fast_stage_state(s, st):
    o = s["o_intra"] + _dot_nt(s["qd"], st.astype(BF16))
    return o, st * s["e_all"] + s["upd"]


def _scan_chunk(q, f, v, par, big_ref, msk_ref, st, C, forward):
    ng, k = _gates(f, par)
    psum = jnp.dot(big_ref[...], jnp.concatenate(_split3(ng), axis=0), preferred_element_type=F32)
    ex = jnp.exp(-psum)
    e_in = ex[0:C]
    e_out = ex[C:2 * C]
    nlev = msk_ref.shape[0] - 1
    o = _dot_nt((q * e_in).astype(BF16), st.astype(BF16))
    scores = msk_ref[0] * _dot_nt(q.astype(BF16), k.astype(BF16))
    for l in range(nlev):
        el = ex[(2 + l) * C:(3 + l) * C]
        scores = scores + msk_ref[1 + l] * _dot_nt((q * el).astype(BF16), (k * el).astype(BF16))
    o = o + jnp.dot(scores.astype(BF16), v.astype(BF16), preferred_element_type=F32)
    e_all = e_in[C - 1:C, :] if forward else e_in[0:1, :]
    st_new = st * e_all + _dot_tn(v.astype(BF16), (k * e_out).astype(BF16))
    return o, st_new


def _hgrn_kernel(*refs, L, C, U, with_out):
    if with_out:
        (q_ref, ff_ref, fb_ref, iv_ref, ga_ref, par_ref, bigf_ref, mskf_ref, bigb_ref, mskb_ref,
         s0f_ref, s0b_ref, o_ref, sf_ref, sb_ref, sf_in, sb_in, of_scr, ob_scr) = refs
    else:
        (q_ref, ff_ref, fb_ref, iv_ref, par_ref, bigf_ref, mskf_ref, bigb_ref, mskb_ref,
         s0f_ref, s0b_ref, sf_ref, sb_ref, sf_in, sb_in) = refs
    n = L // C
    sf_ref[...] = s0f_ref[...]
    sb_ref[...] = s0b_ref[...]
    par = par_ref[...]
    trif_ref = bigf_ref.at[0:C, :]
    trib_ref = bigb_ref.at[0:C, :]

    def rows_of(it, u):
        c = it * U + u
        return (pl.ds(pl.multiple_of(c * C, C), C), pl.ds(pl.multiple_of((n - 1 - c) * C, C), C))

    def silu_q(rows):
        q = q_ref[rows, :]
        return q * jax.nn.sigmoid(q)

    def put(rf, rb, o_f, o_b):
        if with_out:
            of_scr[rf, :] = o_f
            ob_scr[rb, :] = o_b

    def body(it, carry):
        sf_in[...] = sf_ref[...]
        sb_in[...] = sb_ref[...]
        rows = [rows_of(it, u) for u in range(U)]
        fwd = [_fast_stage_sums(silu_q(rf), ff_ref[rf, :], iv_ref[rf, :], par[0:3], trif_ref)
               for rf, _ in rows]
        bwd = [_fast_stage_sums(silu_q(rb), fb_ref[rb, :], iv_ref[rb, :], par[3:6], trib_ref)
               for _, rb in rows]
        fwd = [_fast_stage_scores(s, C, True) for s in fwd]
        bwd = [_fast_stage_scores(s, C, False) for s in bwd]
        guard = functools.reduce(jnp.maximum, [s["guard"] for s in fwd + bwd])
        fwd = [_fast_stage_intra(s, C, True) for s in fwd]
        bwd = [_fast_stage_intra(s, C, False) for s in bwd]
        st_f, st_b = sf_ref[...], sb_ref[...]
        for u in range(U):
            o_f, st_f = _fast_stage_state(fwd[u], st_f)
            o_b, st_b = _fast_stage_state(bwd[u], st_b)
            put(rows[u][0], rows[u][1], o_f, o_b)
        sf_ref[...] = st_f
        sb_ref[...] = st_b

        @pl.when(jnp.logical_not(guard <= SCAN_GUARD))
        def _():
            sf_ref[...] = sf_in[...]
            sb_ref[...] = sb_in[...]

            def redo(u, carry2):
                rf, rb = rows_of(it, u)
                o_f, s_f = _scan_chunk(silu_q(rf), ff_ref[rf, :], iv_ref[rf, :], par[0:3],
                                       bigf_ref, mskf_ref, sf_ref[...], C, True)
                sf_ref[...] = s_f
                o_b, s_b = _scan_chunk(silu_q(rb), fb_ref[rb, :], iv_ref[rb, :], par[3:6],
                                       bigb_ref, mskb_ref, sb_ref[...], C, False)
                sb_ref[...] = s_b
                put(rf, rb, o_f, o_b)
                return carry2

            lax.fori_loop(0, U, redo, 0)

        return carry

    lax.fori_loop(0, n // U, body, 0)

    if with_out:
        gw = par[6:7, :]
        R = min(L, 256)

        def fin(r, carry):
            rows = pl.ds(pl.multiple_of(r * R, R), R)
            o = of_scr[rows, :] + ob_scr[rows, :]
            o = o * lax.rsqrt(jnp.mean(o * o, axis=-1, keepdims=True) + EPS) * gw
            ga = ga_ref[rows, :]
            o_ref[rows, :] = (o * (ga * jax.nn.sigmoid(ga))).astype(o_ref.dtype)
            return carry

        lax.fori_loop(0, L // R, fin, 0)


def _hgrn(px, par, s0f, s0b, with_out):
    B, L, _ = px.shape
    C = SCAN_CHUNK
    H = A_HEADS
    bigf, mskf = _scan_consts(C, True)
    bigb, mskb = _scan_consts(C, False)
    bigf, bigb = jnp.asarray(bigf, BF16), jnp.asarray(bigb, BF16)
    mskf, mskb = jnp.asarray(mskf, F32), jnp.asarray(mskb, F32)

    def col(j):
        return pl.BlockSpec((None, L, LANES), lambda b, h, j=j: (b, 0, j * H + h))

    def const(a):
        return pl.BlockSpec(a.shape, lambda b, h, nd=a.ndim: (0,) * nd)

    st_spec = pl.BlockSpec((None, None, HEAD_DIM, HEAD_DIM), lambda b, h: (b, h, 0, 0))
    st_shape = jax.ShapeDtypeStruct((B, H, HEAD_DIM, HEAD_DIM), F32)
    in_specs = [col(0), col(1), col(2), col(3)]
    args = [px, px, px, px]
    if with_out:
        in_specs.append(col(4))
        args.append(px)
    in_specs += [pl.BlockSpec((None, SUBLANES, LANES), lambda b, h: (h, 0, 0)),
                 const(bigf), const(mskf), const(bigb), const(mskb), st_spec, st_spec]
    args += [par, bigf, mskf, bigb, mskb, s0f, s0b]
    out_specs = [st_spec, st_spec]
    out_shape = [st_shape, st_shape]
    scratch = [pltpu.VMEM((HEAD_DIM, HEAD_DIM), F32), pltpu.VMEM((HEAD_DIM, HEAD_DIM), F32)]
    if with_out:
        out_specs = [pl.BlockSpec((None, L, LANES), lambda b, h: (b, 0, h))] + out_specs
        out_shape = [jax.ShapeDtypeStruct((B, L, D_A), BF16)] + out_shape
        scratch += [pltpu.VMEM((L, LANES), F32), pltpu.VMEM((L, LANES), F32)]
    res = pl.pallas_call(
        functools.partial(_hgrn_kernel, L=L, C=C, U=min(SCAN_UNROLL, L // C), with_out=with_out),
        grid=(B, H),
        in_specs=in_specs,
        out_specs=out_specs,
        out_shape=out_shape,
        scratch_shapes=scratch,
        compiler_params=_cparams("arbitrary", "arbitrary"),
        name="hgrn2",
    )(*args)
    if with_out:
        return res[0], res[1], res[2]
    return None, res[0], res[1]


def _conv3(ref, r0, R, L, cp):
    cur = ref[pl.ds(r0, R), :]
    prev_tile = ref[pl.ds(pl.multiple_of(jnp.maximum(r0 - SUBLANES, 0), SUBLANES), SUBLANES), :]
    next_tile = ref[pl.ds(pl.multiple_of(jnp.minimum(r0 + R, L - SUBLANES), SUBLANES), SUBLANES), :]
    prow = jnp.where(r0 > 0, prev_tile[SUBLANES - 1:SUBLANES, :], 0.0)
    nrow = jnp.where(r0 + R < L, next_tile[0:1, :], 0.0)
    rid = lax.broadcasted_iota(jnp.int32, (R, LANES), 0)
    up = jnp.where(rid == 0, prow, pltpu.roll(cur, 1, 0))
    dn = jnp.where(rid == R - 1, nrow, pltpu.roll(cur, R - 1, 0))
    return cp[3:4, :] + up * cp[0:1, :] + cur * cp[1:2, :] + dn * cp[2:3, :]


def _hyena_pre_kernel(x0_ref, x1_ref, v_ref, gb_ref, c0_ref, c1_ref, c2_ref, vin_ref, m_ref, *, L):
    R = min(L, 256)
    c0 = c0_ref[...]
    c1 = c1_ref[...]
    c2 = c2_ref[...]

    def body(r, carry):
        r0 = pl.multiple_of(r * R, R)
        rows = pl.ds(r0, R)
        x0 = _conv3(x0_ref, r0, R, L, c0)
        x1 = _conv3(x1_ref, r0, R, L, c1)
        vv = _conv3(v_ref, r0, R, L, c2)
        gb = gb_ref[rows, :]
        vin_ref[rows, :] = vv * x1
        m_ref[rows, :] = (x0 * (gb * jax.nn.sigmoid(gb))).astype(m_ref.dtype)
        return carry

    lax.fori_loop(0, L // R, body, 0)


def _hyena_pre(px, conv_par):
    B, L, _ = px.shape
    nb = D_B // LANES
    base = 5 * D_A // LANES

    def col(j):
        return pl.BlockSpec((None, L, LANES), lambda b, c, j=j: (b, 0, base + j * nb + c))

    def cpar(j):
        return pl.BlockSpec((SUBLANES, LANES), lambda b, c, j=j: (0, j * nb + c))

    out_spec = pl.BlockSpec((None, L, LANES), lambda b, c: (b, 0, c))
    out_shape = jax.ShapeDtypeStruct((B, L, D_B), F32)
    return pl.pallas_call(
        functools.partial(_hyena_pre_kernel, L=L),
        grid=(B, nb),
        in_specs=[col(0), col(1), col(2), col(3), cpar(0), cpar(1), cpar(2)],
        out_specs=[out_spec, out_spec],
        out_shape=[out_shape, jax.ShapeDtypeStruct((B, L, D_B), BF16)],
        compiler_params=_cparams("arbitrary", "arbitrary"),
        name="hyena_pre",
    )(px, px, px, px, conv_par, conv_par, conv_par)


def _pos_rows(L, order):
    f32 = np.float32
    t = np.linspace(0.0, 1.0, L, dtype=f32)[:, None]
    w = (f32(2.0 * math.pi) * np.arange(L, dtype=f32)[:, None]) / f32(L)
    f = np.linspace(1e-4, HY_BANDS - 1, HY_BANDS, dtype=f32)[None, :]
    z = np.concatenate([t, np.cos(f * w), -np.sin(f * w)], axis=-1).astype(f32)
    lag = np.arange(2 * L)
    src = np.where(lag < L, lag, 2 * L - lag) % L
    fwd = (lag < L).astype(f32)
    valid = (lag != L).astype(f32)
    rows = np.concatenate([z[src], fwd[:, None], valid[:, None],
                           np.zeros((2 * L, HY_WIDTH - HY_EMB - 2), f32)], axis=-1)
    return jnp.asarray(rows[order])


def _filter_kernel(z_ref, w1_ref, w2_ref, w3_ref, vec_ref, w4f_ref, w4b_ref, dl_ref, k_ref, h_scr,
                   *, n_rows):
    R = min(n_rows, 512)
    dl = dl_ref[...]

    @pl.when(pl.program_id(0) == 0)
    def _():
        vec = vec_ref[...]
        b1, b2, b3, fr = vec[0:1], vec[1:2], vec[2:3], vec[3:4]

        def mlp(r, carry):
            rows = pl.ds(pl.multiple_of(r * R, R), R)
            z = z_ref[rows, :]
            h = jnp.sin(fr * (jnp.dot(z, w1_ref[...], precision=HIGHEST, preferred_element_type=F32) + b1))
            h = jnp.sin(fr * (jnp.dot(h, w2_ref[...], precision=HIGHEST, preferred_element_type=F32) + b2))
            h_scr[rows, :] = jnp.sin(
                fr * (jnp.dot(h, w3_ref[...], precision=HIGHEST, preferred_element_type=F32) + b3))
            return carry

        lax.fori_loop(0, n_rows // R, mlp, 0)

    def body(r, acc):
        rows = pl.ds(pl.multiple_of(r * R, R), R)
        z = z_ref[rows, :]
        h = h_scr[rows, :]
        hf = jnp.dot(h, w4f_ref[...], precision=HIGHEST, preferred_element_type=F32)
        hb = jnp.dot(h, w4b_ref[...], precision=HIGHEST, preferred_element_type=F32)
        fwd = z[:, HY_EMB:HY_EMB + 1]
        valid = z[:, HY_EMB + 1:HY_EMB + 2]
        k = jnp.where(fwd > 0.5, hf, hb) * jnp.exp(-z[:, 0:1] * dl) * valid
        k_ref[rows, :] = k
        return acc + jnp.sum(jnp.abs(k), axis=0, keepdims=True)

    tot = lax.fori_loop(0, n_rows // R, body, jnp.zeros((1, LANES), F32))

    def scale(r, carry):
        rows = pl.ds(pl.multiple_of(r * R, R), R)
        k_ref[rows, :] = k_ref[rows, :] / tot
        return carry

    lax.fori_loop(0, n_rows // R, scale, 0)


def _hyena_filter(zrows, w1, b1, freq, w2, b2, w3, b3, w4):
    n_rows = zrows.shape[0]
    w1p = jnp.concatenate([w1, jnp.zeros((HY_WIDTH - HY_EMB, HY_WIDTH), F32)], axis=0)
    vec = jnp.concatenate([b1[None], b2[None], b3[None], freq[None],
                           jnp.zeros((SUBLANES - 4, HY_WIDTH), F32)], axis=0)
    deltas = jnp.abs(jnp.linspace(HY_MIN_DECAY, HY_MAX_DECAY, D_B, dtype=F32))[None, :]
    nb = D_B // LANES

    def const(shape):
        return pl.BlockSpec(shape, lambda c, nd=len(shape): (0,) * nd)

    return pl.pallas_call(
        functools.partial(_filter_kernel, n_rows=n_rows),
        grid=(nb,),
        in_specs=[const((n_rows, HY_WIDTH)), const((HY_WIDTH, HY_WIDTH)), const((HY_WIDTH, HY_WIDTH)),
                  const((HY_WIDTH, HY_WIDTH)), const((SUBLANES, HY_WIDTH)),
                  pl.BlockSpec((HY_WIDTH, LANES), lambda c: (0, c)),
                  pl.BlockSpec((HY_WIDTH, LANES), lambda c: (0, nb + c)),
                  pl.BlockSpec((1, LANES), lambda c: (0, c))],
        out_specs=pl.BlockSpec((n_rows, LANES), lambda c: (0, c)),
        out_shape=jax.ShapeDtypeStruct((n_rows, D_B), F32),
        scratch_shapes=[pltpu.VMEM((n_rows, HY_WIDTH), F32)],
        compiler_params=_cparams("arbitrary"),
        name="hyena_filter",
    )(zrows, w1p, w2, w3, vec, w4, w4, deltas)


def _cplx_block(gr, gi):
    return np.block([[gr, -gi], [gi, gr]])


def _hi_lo(a):
    a = jnp.asarray(a, F32)
    hi = a.astype(BF16)
    return hi, (a - hi.astype(F32)).astype(BF16)


def _dft_consts(L):
    N = 2 * L
    n2 = FFT_MINOR
    n1 = N // n2
    half = n1 // 2
    f1 = np.arange(n1)
    t2 = np.arange(n2)
    t1 = np.arange(n1)
    tt = n2 * t1[None, None, :] + t2[:, None, None]
    ang = -2.0 * np.pi * ((f1[None, :, None] * tt) % N) / N
    gr, gi = np.cos(ang), np.sin(ang)
    m1_data = np.stack([_cplx_block(gr[j][:, :half], gi[j][:, :half]) for j in range(n2)])
    m1_filt = np.concatenate([gr, gi], axis=1)
    m1_inv = np.stack([_cplx_block(gr[j][:, :half].T, -gi[j][:, :half].T) for j in range(n2)]) / N
    ang2 = -2.0 * np.pi * ((t2[:, None] * t2[None, :]) % n2) / n2
    f2 = _cplx_block(np.cos(ang2), np.sin(ang2))
    f2_inv = _cplx_block(np.cos(ang2), -np.sin(ang2))
    return dict(n1=n1, n2=n2, m1_data=_hi_lo(m1_data), m1_filt=_hi_lo(m1_filt), m1_inv=_hi_lo(m1_inv),
                f2=_hi_lo(f2[None]), f2_inv=_hi_lo(f2_inv[None]))


def _dft_single_consts(L):
    N = 2 * L
    f = np.arange(N)
    ang = -2.0 * np.pi * ((f[:, None] * f[None, :]) % N) / N
    gr, gi = np.cos(ang), np.sin(ang)
    fwd = _cplx_block(gr[:, :L], gi[:, :L])
    filt = np.concatenate([gr, gi], axis=0)
    inv = _cplx_block(gr[:L, :], -gi[:L, :]) / N
    return dict(fwd=_hi_lo(fwd[None]), filt=_hi_lo(filt[None]), inv=_hi_lo(inv[None]))


def _bmm_kernel(mh_ref, ml_ref, x_ref, o_ref, *, gblk, shared, precise):
    for j in range(gblk):
        jm = 0 if shared else j
        x = x_ref[j]
        xh = x.astype(BF16)
        acc = jnp.dot(mh_ref[jm], xh, preferred_element_type=F32)
        if precise:
            xl = (x - xh.astype(F32)).astype(BF16)
            acc = (acc + jnp.dot(mh_ref[jm], xl, preferred_element_type=F32)
                   + jnp.dot(ml_ref[jm], xh, preferred_element_type=F32))
        o_ref[j] = acc.astype(o_ref.dtype)


def _bmm_left(m, x, gblk, precise, out_dtype):
    mh, ml = m
    P, G, K, C = x.shape
    R = mh.shape[1]
    shared = mh.shape[0] == 1
    gblk = min(gblk, G)
    m_spec = (pl.BlockSpec((1, R, K), lambda g, p: (0, 0, 0)) if shared
              else pl.BlockSpec((gblk, R, K), lambda g, p: (g, 0, 0)))
    return pl.pallas_call(
        functools.partial(_bmm_kernel, gblk=gblk, shared=shared, precise=precise),
        grid=(G // gblk, P),
        in_specs=[m_spec, m_spec, pl.BlockSpec((None, gblk, K, C), lambda g, p: (p, g, 0, 0))],
        out_specs=pl.BlockSpec((None, gblk, R, C), lambda g, p: (p, g, 0, 0)),
        out_shape=jax.ShapeDtypeStruct((P, G, R, C), out_dtype),
        compiler_params=_cparams("arbitrary", "arbitrary"),
        name="dft_stage",
    )(mh, ml, x)


def _mid_kernel(fa_ref, fb_ref, kf_ref, d_ref, o_ref, *, gblk, nf):
    fa = fa_ref[...]
    fb = fb_ref[...]
    for j in range(gblk):
        xx = jnp.dot(fa, d_ref[j].astype(BF16), preferred_element_type=F32)
        xr, xi = xx[:nf], xx[nf:]
        kr, ki = kf_ref[j, :nf], kf_ref[j, nf:]
        yy = jnp.concatenate([xr * kr - xi * ki, xr * ki + xi * kr], axis=0)
        o_ref[j] = jnp.dot(fb, yy.astype(BF16), preferred_element_type=F32).astype(o_ref.dtype)


def _spectral_mid(fa, fb, kf, d, gblk, out_dtype):
    P, G, K, C = d.shape
    nf2 = fa.shape[0]
    Ko = fb.shape[0]
    gblk = min(gblk, G)
    return pl.pallas_call(
        functools.partial(_mid_kernel, gblk=gblk, nf=nf2 // 2),
        grid=(G // gblk, P),
        in_specs=[pl.BlockSpec(fa.shape, lambda g, p: (0, 0)),
                  pl.BlockSpec(fb.shape, lambda g, p: (0, 0)),
                  pl.BlockSpec((gblk, nf2, C), lambda g, p: (g, 0, 0)),
                  pl.BlockSpec((None, gblk, K, C), lambda g, p: (p, g, 0, 0))],
        out_specs=pl.BlockSpec((None, gblk, Ko, C), lambda g, p: (p, g, 0, 0)),
        out_shape=jax.ShapeDtypeStruct((P, G, Ko, C), out_dtype),
        compiler_params=_cparams("arbitrary", "arbitrary"),
        name="dft_mid",
    )(fa, fb, kf, d)


def _first_stage_kernel(mh_ref, ml_ref, x_ref, o_ref, *, gblk, n1, paired, precise):
    for j in range(gblk):
        if paired:
            x = jnp.concatenate([x_ref[0, :, j, :], x_ref[1, :, j, :]], axis=0)
        else:
            x = x_ref[j]
        xh = x.astype(BF16)
        acc = jnp.dot(mh_ref[j], xh, preferred_element_type=F32)
        if precise:
            xl = (x - xh.astype(F32)).astype(BF16)
            acc = (acc + jnp.dot(mh_ref[j], xl, preferred_element_type=F32)
                   + jnp.dot(ml_ref[j], xh, preferred_element_type=F32))
        o_ref[:, 0, j, :] = acc[:n1]
        o_ref[:, 1, j, :] = acc[n1:]


def _first_stage(m, x, n1, n2, paired, precise):
    mh, ml = m
    gblk = SUBLANES
    if paired:
        _, P, half, _, C = x.shape
        x_spec = pl.BlockSpec((2, None, half, gblk, C), lambda g, p: (0, p, 0, g, 0))
    else:
        P, _, _, C = x.shape
        x_spec = pl.BlockSpec((None, gblk, n1, C), lambda g, p: (p, g, 0, 0))
    m_spec = pl.BlockSpec((gblk,) + mh.shape[1:], lambda g, p: (g, 0, 0))
    return pl.pallas_call(
        functools.partial(_first_stage_kernel, gblk=gblk, n1=n1, paired=paired, precise=precise),
        grid=(n2 // gblk, P),
        in_specs=[m_spec, m_spec, x_spec],
        out_specs=pl.BlockSpec((None, n1, 2, gblk, C), lambda g, p: (p, 0, 0, g, 0)),
        out_shape=jax.ShapeDtypeStruct((P, n1, 2, n2, C), F32),
        compiler_params=_cparams("arbitrary", "arbitrary"),
        name="dft_first",
    )(mh, ml, x)


def _last_stage_kernel(m_ref, z_ref, y_ref, *, gblk, half):
    for j in range(gblk):
        z = jnp.concatenate([z_ref[:, 0, j, :], z_ref[:, 1, j, :]], axis=0).astype(BF16)
        y = jnp.dot(m_ref[j], z, preferred_element_type=F32)
        y_ref[0, :, j, :] = y[:half]
        y_ref[1, :, j, :] = y[half:]


def _last_stage(m, z, half):
    P, n1, _, n2, C = z.shape
    gblk = SUBLANES
    return pl.pallas_call(
        functools.partial(_last_stage_kernel, gblk=gblk, half=half),
        grid=(n2 // gblk, P),
        in_specs=[pl.BlockSpec((gblk,) + m.shape[1:], lambda g, p: (g, 0, 0)),
                  pl.BlockSpec((None, n1, 2, gblk, C), lambda g, p: (p, 0, 0, g, 0))],
        out_specs=pl.BlockSpec((2, None, half, gblk, C), lambda g, p: (0, p, 0, g, 0)),
        out_shape=jax.ShapeDtypeStruct((2, P, half, n2, C), F32),
        compiler_params=_cparams("arbitrary", "arbitrary"),
        name="dft_last",
    )(m, z)


def _long_conv_two_stage(vin, filt_rows, dc):
    B, L, C = vin.shape
    n1, n2 = dc["n1"], dc["n2"]
    half = n1 // 2
    P = B // 2
    ka = _first_stage(dc["m1_filt"], filt_rows.reshape(1, n2, n1, C), n1, n2, False, True)
    kf = _bmm_left(dc["f2"], ka.reshape(1, n1, 2 * n2, C), 8, True, F32)[0]
    a = _first_stage(dc["m1_data"], vin.reshape(2, P, half, n2, C), n1, n2, True, False)
    z = _spectral_mid(dc["f2"][0][0], dc["f2_inv"][0][0], kf, a.reshape(P, n1, 2 * n2, C), 4, F32)
    y = _last_stage(dc["m1_inv"][0], z.reshape(P, n1, 2, n2, C), half)
    return y.reshape(B, L, C)


def _long_conv_single(vin, filt, dc):
    B, L, C = vin.shape
    P = B // 2
    kf = _bmm_left(dc["filt"], filt.reshape(1, 1, 2 * L, C), 1, True, F32)[0]
    d = vin.reshape(2, P, L, C).transpose(1, 0, 2, 3).reshape(P, 1, 2 * L, C)
    y = _spectral_mid(dc["fwd"][0][0], dc["inv"][0][0], kf, d, 1, F32)
    return y.reshape(P, 2, L, C).transpose(1, 0, 2, 3).reshape(B, L, C)


def _outproj_kernel(*refs, final):
    if final:
        oa_ref, y_ref, vin_ref, m_ref, hb_ref, x_ref, gt_ref, w_ref, fw_ref, o_ref = refs
    else:
        oa_ref, y_ref, vin_ref, m_ref, hb_ref, x_ref, gt_ref, w_ref, o_ref = refs
    vin = vin_ref[...]
    ob = (y_ref[...] + vin * hb_ref[...]) * m_ref[...]
    r = (jnp.dot(oa_ref[...], w_ref[0:D_A, :], preferred_element_type=F32)
         + jnp.dot(ob.astype(BF16), w_ref[D_A:D_A + D_B, :], preferred_element_type=F32))
    xn = x_ref[...] + gt_ref[...] * r
    if final:
        ms = jnp.mean(xn * xn, axis=-1, keepdims=True)
        xn = xn * lax.rsqrt(ms + EPS) * fw_ref[...]
    o_ref[...] = xn


def _out_proj(oa, y, vin, m, hy_bias, x, gt, w_bf16, final_w):
    B, L, D = x.shape
    tm = min(L, 512)
    final = final_w is not None
    half = lambda: pl.BlockSpec((None, tm, D_B), lambda b, i: (b, i, 0))
    in_specs = [half(), half(), half(), half(),
                pl.BlockSpec((1, D_B), lambda b, i: (0, 0)),
                pl.BlockSpec((None, tm, D), lambda b, i: (b, i, 0)),
                pl.BlockSpec((None, 1, D), lambda b, i: (b, 0, 0)),
                pl.BlockSpec((D, D), lambda b, i: (0, 0))]
    args = [oa, y, vin, m, hy_bias.reshape(1, D_B), x, gt, w_bf16]
    if final:
        in_specs.append(pl.BlockSpec((1, D), lambda b, i: (0, 0)))
        args.append(final_w.reshape(1, D))
    return pl.pallas_call(
        functools.partial(_outproj_kernel, final=final),
        grid=(B, L // tm),
        in_specs=in_specs,
        out_specs=pl.BlockSpec((None, tm, D), lambda b, i: (b, i, 0)),
        out_shape=jax.ShapeDtypeStruct((B, L, D), F32),
        compiler_params=_cparams("arbitrary", "arbitrary"),
        name="out_proj",
    )(*args)


def _scan_params(lb, g_norm_w):
    rows = []
    for d in range(2):
        rows += [jnp.log(lb[d]), jnp.log1p(-lb[d]), 1.0 - lb[d]]
    rows += [g_norm_w, jnp.zeros_like(g_norm_w)]
    par = jnp.stack(rows, axis=0)
    return par.reshape(SUBLANES, A_HEADS, HEAD_DIM).transpose(1, 0, 2)


def kernel(x, c, ctx, c_ctx, norm_w, w_ada, b_ada, w_in, w_out, lb_logits, g_norm_w,
           conv_w, conv_b, hy_w1, hy_b1, hy_freq, hy_w2, hy_b2, hy_w3, hy_b3, hy_w4,
           hy_bias, final_norm_w):
    B, L_lat, D = x.shape
    L_ctx = ctx.shape[1]
    p_lb = jax.nn.softmax(lb_logits.astype(F32), axis=0)
    lbs = jnp.cumsum(p_lb, axis=0)
    lbs = lbs - lbs[0:1]

    n_rows = 2 * SUBLANES
    cc = jnp.zeros((n_rows, D), F32).at[:B].set(c).at[B].set(c_ctx)
    mod = _modulation(cc, w_ada, b_ada)

    dc_lat = _dft_consts(L_lat)
    dc_ctx = _dft_single_consts(L_ctx)
    n1, n2 = dc_lat["n1"], dc_lat["n2"]
    order_lat = (np.arange(n2)[:, None] + n2 * np.arange(n1)[None, :]).reshape(-1)
    z_lat = _pos_rows(L_lat, order_lat)
    z_ctx = _pos_rows(L_ctx, np.arange(2 * L_ctx))
    zero_state = jnp.zeros((B, A_HEADS, HEAD_DIM, HEAD_DIM), F32)

    for l in range(DEPTH):
        last = l == DEPTH - 1
        sh_x, sc_x, gt_x = [mod[l, :B, j * D:(j + 1) * D].reshape(B, 1, D) for j in range(3)]
        sh_c, sc_c, gt_c = [jnp.broadcast_to(mod[l, B, j * D:(j + 1) * D].reshape(1, 1, D), (B, 1, D))
                            for j in range(3)]
        w_in_l = w_in[l].astype(BF16)
        w_out_l = w_out[l].astype(BF16)
        par = _scan_params(lbs[l], g_norm_w[l])
        conv_par = jnp.concatenate([conv_w[l], conv_b[l][None],
                                    jnp.zeros((SUBLANES - 4, 3 * D_B), F32)], axis=0)
        filt_args = (hy_w1[l], hy_b1[l], hy_freq[l], hy_w2[l], hy_b2[l], hy_w3[l], hy_b3[l], hy_w4[l])

        if last:
            pc = _in_proj(ctx, norm_w[l], sc_c, sh_c, w_in_l[:, :4 * D_A], 4 * D_A // 2)
            _, s_f, s_b = _hgrn(pc, par, zero_state, zero_state, with_out=False)
        else:
            pc = _in_proj(ctx, norm_w[l], sc_c, sh_c, w_in_l, N_IN // 3)
            oa_c, s_f, s_b = _hgrn(pc, par, zero_state, zero_state, with_out=True)
            vin_c, m_c = _hyena_pre(pc, conv_par)
            filt_c = _hyena_filter(z_ctx, *filt_args)
            y_c = _long_conv_single(vin_c, filt_c, dc_ctx)
            ctx = _out_proj(oa_c, y_c, vin_c, m_c, hy_bias[l], ctx, gt_c, w_out_l, None)

        px = _in_proj(x, norm_w[l], sc_x, sh_x, w_in_l, N_IN // 3)
        oa, _, _ = _hgrn(px, par, s_f, s_b, with_out=True)
        vin, m = _hyena_pre(px, conv_par)
        filt_x = _hyena_filter(z_lat, *filt_args)
        y = _long_conv_two_stage(vin, filt_x, dc_lat)
        x = _out_proj(oa, y, vin, m, hy_bias[l], x, gt_x, w_out_l, final_norm_w if last else None)

    return x
```

```python
import functools
import math

import numpy as np
import jax
import jax.numpy as jnp
from jax import lax
from jax.experimental import pallas as pl
from jax.experimental.pallas import tpu as pltpu

F32 = jnp.float32
BF16 = jnp.bfloat16
HIGHEST = lax.Precision.HIGHEST

D_MODEL = 1024
DEPTH = 2
D_A = 512
D_B = 512
A_HEADS = 4
HEAD_DIM = 128
N_IN = 5 * D_A + 4 * D_B
HY_EMB = 33
HY_BANDS = 16
HY_WIDTH = 64
HY_MIN_DECAY = math.log(1e-2) / 1.5
HY_MAX_DECAY = math.log(1e-2) / 0.3
EPS = 1e-6

LANES = 128
SUBLANES = 8
SCAN_CHUNK = 64
SCAN_UNROLL = 8
SCAN_GUARD = 80.0
FFT_MINOR = 128
VMEM_LIMIT = 56 * 1024 * 1024


def _cparams(*sem):
    return pltpu.CompilerParams(dimension_semantics=sem, vmem_limit_bytes=VMEM_LIMIT)


def _mod_kernel(c_ref, w_ref, b_ref, o_ref):
    cc = c_ref[...]
    s = cc * jax.nn.sigmoid(cc)
    o_ref[...] = jnp.dot(s, w_ref[...], precision=HIGHEST,
                         preferred_element_type=F32) + b_ref[...]


def _modulation(cc, w_ada, b_ada):
    R, D = cc.shape
    N = w_ada.shape[-1]
    tn = 768
    return pl.pallas_call(
        _mod_kernel,
        grid=(DEPTH, N // tn),
        in_specs=[
            pl.BlockSpec((R, D), lambda l, j: (0, 0)),
            pl.BlockSpec((None, D, tn), lambda l, j: (l, 0, j)),
            pl.BlockSpec((None, 1, tn), lambda l, j: (l, 0, j)),
        ],
        out_specs=pl.BlockSpec((None, R, tn), lambda l, j: (l, 0, j)),
        out_shape=jax.ShapeDtypeStruct((DEPTH, R, N), F32),
        compiler_params=_cparams("arbitrary", "arbitrary"),
        name="modulation",
    )(cc, w_ada, b_ada.reshape(DEPTH, 1, N))


def _gates(f, par):
    loglb = par[0:1, :]
    log1mlb = par[1:2, :]
    onemlb = par[2:3, :]
    s1 = jnp.log(1.0 + jnp.exp(-jnp.abs(f)))
    b = log1mlb + (jnp.minimum(f, 0.0) - s1)
    ng = -jnp.maximum(loglb, b) - jnp.log(1.0 + jnp.exp(-jnp.abs(loglb - b)))
    k = onemlb * jnp.exp(-(jnp.maximum(f, 0.0) + s1))
    return ng, k


def _inproj_kernel(x_ref, nw_ref, sc_ref, sh_ref, gp_ref, w_ref, o_ref, hx_ref):
    x = x_ref[...]
    ms = jnp.mean(x * x, axis=-1, keepdims=True)
    y = x * lax.rsqrt(ms + EPS) * nw_ref[...]
    hx_ref[...] = (y * (1.0 + sc_ref[...]) + sh_ref[...]).astype(BF16)
    gp = gp_ref[...]
    w = D_A
    for g in range(w_ref.shape[-1] // w):
        r = jnp.dot(hx_ref[...], w_ref[:, g * w:(g + 1) * w], preferred_element_type=F32)
        if g == 0:
            o_ref[:, 0:w] = r * jax.nn.sigmoid(r)
        elif g in (1, 2):
            ng, k = _gates(r, gp[3 * (g - 1):3 * g])
            o_ref[:, (2 * g - 1) * w:2 * g * w] = ng
            o_ref[:, 2 * g * w:(2 * g + 1) * w] = k
        else:
            o_ref[:, (g + 2) * w:(g + 3) * w] = r


def _in_proj(x, norm_w, sc, sh, gate_par, w_bf16):
    B, L, D = x.shape
    N = w_bf16.shape[1]
    No = N + 2 * D_A
    tm = min(L, 512)
    return pl.pallas_call(
        _inproj_kernel,
        grid=(B, L // tm),
        in_specs=[
            pl.BlockSpec((None, tm, D), lambda b, i: (b, i, 0)),
            pl.BlockSpec((1, D), lambda b, i: (0, 0)),
            pl.BlockSpec((None, 1, D), lambda b, i: (b, 0, 0)),
            pl.BlockSpec((None, 1, D), lambda b, i: (b, 0, 0)),
            pl.BlockSpec((SUBLANES, D_A), lambda b, i: (0, 0)),
            pl.BlockSpec((D, N), lambda b, i: (0, 0)),
        ],
        out_specs=pl.BlockSpec((None, tm, No), lambda b, i: (b, i, 0)),
        out_shape=jax.ShapeDtypeStruct((B, L, No), F32),
        scratch_shapes=[pltpu.VMEM((tm, D), BF16)],
        compiler_params=_cparams("arbitrary", "arbitrary"),
        name="in_proj",
    )(x, norm_w.reshape(1, D), sc, sh, gate_par, w_bf16)


def _scan_consts(C, forward):
    idx = np.arange(C)
    i = idx[:, None]
    t = idx[None, :]
    if forward:
        mats = [t <= i, t > i]
    else:
        mats = [t >= i, t < i]
    masks = [i == t]
    h = 1
    while h < C:
        P = 2 * h
        p = i % P
        m = i - p + h
        upper = p >= h
        same = (i // P) == (t // P)
        if forward:
            mats.append(np.where(upper, (t >= m) & (t <= i), (t >= i + 1) & (t <= m - 1)))
            masks.append(same & upper & ((t % P) < h))
        else:
            mats.append(np.where(upper, (t >= m) & (t <= i - 1), (t >= i) & (t <= m - 1)))
            masks.append(same & (~upper) & ((t % P) >= h))
        h = P
    big = np.concatenate([m_.astype(np.float32) for m_ in mats], axis=0)
    big = np.concatenate([big, big, big], axis=1)
    msk = np.stack([m_.astype(np.float32) for m_ in masks], axis=0)
    return big, msk


def _dot_nt(a, b):
    return lax.dot_general(a, b, (((1,), (1,)), ((), ())), preferred_element_type=F32)


def _dot_tn(a, b):
    return lax.dot_general(a, b, (((0,), (0,)), ((), ())), preferred_element_type=F32)


def _split3(a):
    hi = a.astype(BF16)
    r1 = a - hi.astype(F32)
    mid = r1.astype(BF16)
    lo = (r1 - mid.astype(F32)).astype(BF16)
    return hi, mid, lo


def _cumulative(tri_ref, ng):
    return jnp.dot(tri_ref[...], jnp.concatenate(_split3(ng), axis=0), preferred_element_type=F32)


def _fast_stage_sums(q, ng, k, v, tri_ref):
    return dict(q=q, k=k, vb=v.astype(BF16), cum=_cumulative(tri_ref, ng))


def _fast_stage_scores(s, C, forward):
    h = C // 2
    q, k, cum = s["q"], s["k"], s["cum"]
    if forward:
        early, late = slice(0, h), slice(h, C)
        ref, tot = cum[h - 1:h, :], cum[C - 1:C, :]
    else:
        early, late = slice(h, C), slice(0, h)
        ref, tot = cum[h:h + 1, :], cum[0:1, :]
    qd = (q * jnp.exp(-cum)).astype(BF16)
    dl = cum - ref
    k_early = (k[early] * jnp.exp(cum[early])).astype(BF16)
    q_late = (q[late] * jnp.exp(-dl[late])).astype(BF16)
    k_all = (k * jnp.exp(dl)).astype(BF16)
    kd = (k * jnp.exp(cum - tot)).astype(BF16)
    return dict(vb=s["vb"], qd=qd, kd=kd, e_all=jnp.exp(-tot),
                guard=jnp.maximum(jnp.max(cum[early]), jnp.max(dl[late])),
                s_early=_dot_nt(qd[early], k_early), s_late=_dot_nt(q_late, k_all))


def _fast_stage_intra(s, C, forward):
    h = C // 2
    early = slice(0, h) if forward else slice(h, C)
    ri = lax.broadcasted_iota(jnp.int32, (h, h), 0)
    ci = lax.broadcasted_iota(jnp.int32, (h, h), 1)
    rl = lax.broadcasted_iota(jnp.int32, (h, C), 0) + (h if forward else 0)
    cl = lax.broadcasted_iota(jnp.int32, (h, C), 1)
    keep_e = (ci <= ri) if forward else (ci >= ri)
    keep_l = (cl <= rl) if forward else (cl >= rl)
    s_early = jnp.where(keep_e, s["s_early"], 0.0).astype(BF16)
    s_late = jnp.where(keep_l, s["s_late"], 0.0).astype(BF16)
    o_early = jnp.dot(s_early, s["vb"][early], preferred_element_type=F32)
    o_late = jnp.dot(s_late, s["vb"], preferred_element_type=F32)
    o_intra = jnp.concatenate([o_early, o_late] if forward else [o_late, o_early], axis=0)
    return dict(qd=s["qd"], e_all=s["e_all"], o_intra=o_intra, upd=_dot_tn(s["vb"], s["kd"]))


def _fast_stage_state(s, st):
    o = s["o_intra"] + _dot_nt(s["qd"], st.astype(BF16))
    return o, st * s["e_all"] + s["upd"]


def _scan_chunk(q, ng, k, v, big_ref, msk_ref, st, C, forward):
    psum =jnp.dot(big_ref[...], jnp.concatenate(_split3(ng), axis=0), preferred_element_type=F32)
    ex = jnp.exp(-psum)
    e_in = ex[0:C]
    e_out = ex[C:2 * C]
    nlev = msk_ref.shape[0] - 1
    o = _dot_nt((q * e_in).astype(BF16), st.astype(BF16))
    scores = msk_ref[0] * _dot_nt(q.astype(BF16), k.astype(BF16))
    for l in range(nlev):
        el = ex[(2 + l) * C:(3 + l) * C]
        scores = scores + msk_ref[1 + l] * _dot_nt((q * el).astype(BF16), (k * el).astype(BF16))
    o = o + jnp.dot(scores.astype(BF16), v.astype(BF16), preferred_element_type=F32)
    e_all = e_in[C - 1:C, :] if forward else e_in[0:1, :]
    st_new = st * e_all + _dot_tn(v.astype(BF16), (k * e_out).astype(BF16))
    return o, st_new


def _hgrn_kernel(*refs, L, C, U, with_out):
    if with_out:
        (q_ref, ngf_ref, kf_ref, ngb_ref, kb_ref, iv_ref, ga_ref, gw_ref, bigf_ref, mskf_ref,
         bigb_ref, mskb_ref, s0f_ref, s0b_ref, o_ref, sf_ref, sb_ref, sf_in, sb_in, of_scr, ob_scr) = refs
    else:
        (q_ref, ngf_ref, kf_ref, ngb_ref, kb_ref, iv_ref, bigf_ref, mskf_ref, bigb_ref, mskb_ref,
         s0f_ref, s0b_ref, sf_ref, sb_ref, sf_in, sb_in) = refs
    n = L // C
    sf_ref[...] = s0f_ref[...]
    sb_ref[...] = s0b_ref[...]
    trif_ref = bigf_ref.at[0:C, :]
    trib_ref = bigb_ref.at[0:C, :]

    def rows_of(it, u):
        c = it * U + u
        return (pl.ds(pl.multiple_of(c * C, C), C), pl.ds(pl.multiple_of((n - 1 - c) * C, C), C))

    def put(rf, rb, o_f, o_b):
        if with_out:
            of_scr[rf, :] = o_f
            ob_scr[rb, :] = o_b

    def body(it, carry):
        sf_in[...] = sf_ref[...]
        sb_in[...] = sb_ref[...]
        rows = [rows_of(it, u) for u in range(U)]
        fwd = [_fast_stage_sums(q_ref[rf, :], ngf_ref[rf, :], kf_ref[rf, :], iv_ref[rf, :], trif_ref)
               for rf, _ in rows]
        bwd = [_fast_stage_sums(q_ref[rb, :], ngb_ref[rb, :], kb_ref[rb, :], iv_ref[rb, :], trib_ref)
               for _, rb in rows]
        fwd = [_fast_stage_scores(s, C, True) for s in fwd]
        bwd = [_fast_stage_scores(s, C, False) for s in bwd]
        guard = functools.reduce(jnp.maximum, [s["guard"] for s in fwd + bwd])
        fwd = [_fast_stage_intra(s, C, True) for s in fwd]
        bwd = [_fast_stage_intra(s, C, False) for s in bwd]
        st_f, st_b = sf_ref[...], sb_ref[...]
        for u in range(U):
            o_f, st_f = _fast_stage_state(fwd[u], st_f)
            o_b, st_b = _fast_stage_state(bwd[u], st_b)
            put(rows[u][0], rows[u][1], o_f, o_b)
        sf_ref[...] = st_f
        sb_ref[...] = st_b

        @pl.when(jnp.logical_not(guard <= SCAN_GUARD))
        def _():
            sf_ref[...] = sf_in[...]
            sb_ref[...] = sb_in[...]

            def redo(u, carry2):
                rf, rb = rows_of(it, u)
                o_f, s_f = _scan_chunk(q_ref[rf, :], ngf_ref[rf, :], kf_ref[rf, :], iv_ref[rf, :],
                                       bigf_ref, mskf_ref, sf_ref[...], C, True)
                sf_ref[...] = s_f
                o_b, s_b = _scan_chunk(q_ref[rb, :], ngb_ref[rb, :], kb_ref[rb, :], iv_ref[rb, :],
                                       bigb_ref, mskb_ref, sb_ref[...], C, False)
                sb_ref[...] = s_b
                put(rf, rb, o_f, o_b)
                return carry2

            lax.fori_loop(0, U, redo, 0)

        return carry

    lax.fori_loop(0, n // U, body, 0)

    if with_out:
        gw = gw_ref[...]
        R = min(L, 512)

        def fin(r, carry):
            rows = pl.ds(pl.multiple_of(r * R, R), R)
            o = of_scr[rows, :] + ob_scr[rows, :]
            o = o * lax.rsqrt(jnp.mean(o * o, axis=-1, keepdims=True) + EPS) * gw
            ga = ga_ref[rows, :]
            o_ref[rows, :] = (o * (ga * jax.nn.sigmoid(ga))).astype(o_ref.dtype)
            return carry

        lax.fori_loop(0, L // R, fin, 0)


def _hgrn(px, g_norm_w, s0f, s0b, with_out):
    B, L, _ = px.shape
    C = SCAN_CHUNK
    H = A_HEADS
    bigf, mskf = _scan_consts(C, True)
    bigb, mskb = _scan_consts(C, False)
    bigf, bigb = jnp.asarray(bigf, BF16), jnp.asarray(bigb, BF16)
    mskf, mskb = jnp.asarray(mskf, F32), jnp.asarray(mskb, F32)

    def col(j):
        return pl.BlockSpec((None, L, LANES), lambda b, h, j=j: (b, 0, j * H + h))

    def const(a):
        return pl.BlockSpec(a.shape, lambda b, h, nd=a.ndim: (0,) * nd)

    st_spec = pl.BlockSpec((None, None, HEAD_DIM, HEAD_DIM), lambda b, h: (b, h, 0, 0))
    st_shape = jax.ShapeDtypeStruct((B, H, HEAD_DIM, HEAD_DIM), F32)
    in_specs = [col(j) for j in range(6)]
    args = [px] * 6
    if with_out:
        in_specs += [col(6), pl.BlockSpec((1, LANES), lambda b, h: (0, h))]
        args += [px, g_norm_w.reshape(1, D_A)]
    in_specs += [const(bigf), const(mskf), const(bigb), const(mskb), st_spec, st_spec]
    args += [bigf, mskf, bigb, mskb, s0f, s0b]
    out_specs = [st_spec, st_spec]
    out_shape = [st_shape, st_shape]
    scratch = [pltpu.VMEM((HEAD_DIM, HEAD_DIM), F32), pltpu.VMEM((HEAD_DIM, HEAD_DIM), F32)]
    if with_out:
        out_specs = [pl.BlockSpec((None, L, LANES), lambda b, h: (b, 0, h))] + out_specs
        out_shape = [jax.ShapeDtypeStruct((B, L, D_A), BF16)] + out_shape
        scratch += [pltpu.VMEM((L, LANES), F32), pltpu.VMEM((L, LANES), F32)]
    res = pl.pallas_call(
        functools.partial(_hgrn_kernel, L=L, C=C, U=min(SCAN_UNROLL, L // C), with_out=with_out),
        grid=(B, H),
        in_specs=in_specs,
        out_specs=out_specs,
        out_shape=out_shape,
        scratch_shapes=scratch,
        compiler_params=_cparams("arbitrary", "arbitrary"),
        name="hgrn2",
    )(*args)
    if with_out:
        return res[0], res[1], res[2]
    return None, res[0], res[1]


def _conv3(ref, r0, R, L, cp):
    cur = ref[pl.ds(r0, R), :]
    prev_tile = ref[pl.ds(pl.multiple_of(jnp.maximum(r0 - SUBLANES, 0), SUBLANES), SUBLANES), :]
    next_tile = ref[pl.ds(pl.multiple_of(jnp.minimum(r0 + R, L - SUBLANES), SUBLANES), SUBLANES), :]
    prow = jnp.where(r0 > 0, prev_tile[SUBLANES - 1:SUBLANES, :], 0.0)
    nrow = jnp.where(r0 + R < L, next_tile[0:1, :], 0.0)
    rid = lax.broadcasted_iota(jnp.int32, (R, LANES), 0)
    up = jnp.where(rid == 0, prow, pltpu.roll(cur, 1, 0))
    dn = jnp.where(rid == R - 1, nrow, pltpu.roll(cur, R - 1, 0))
    return cp[3:4, :] + up * cp[0:1, :] + cur * cp[1:2, :] + dn * cp[2:3, :]


def _hyena_pre_kernel(x0_ref, x1_ref, v_ref, gb_ref, c0_ref, c1_ref, c2_ref, vin_ref, m_ref, *, L):
    R = min(L, 256)
    c0 = c0_ref[...]
    c1 = c1_ref[...]
    c2 = c2_ref[...]

    def body(r, carry):
        r0 = pl.multiple_of(r * R, R)
        rows = pl.ds(r0, R)
        x0 = _conv3(x0_ref, r0, R, L, c0)
        x1 = _conv3(x1_ref, r0, R, L, c1)
        vv = _conv3(v_ref, r0, R, L, c2)
        gb = gb_ref[rows, :]
        vin_ref[rows, :] = vv * x1
        m_ref[rows, :] = (x0 * (gb * jax.nn.sigmoid(gb))).astype(m_ref.dtype)
        return carry

    lax.fori_loop(0, L // R, body, 0)


def _hyena_pre(px, conv_par):
    B, L, _ = px.shape
    nb = D_B // LANES
    base = 7 * D_A // LANES

    def col(j):
        return pl.BlockSpec((None, L, LANES), lambda b, c, j=j: (b, 0, base + j * nb + c))

    def cpar(j):
        return pl.BlockSpec((SUBLANES, LANES), lambda b, c, j=j: (0, j * nb + c))

    out_spec = pl.BlockSpec((None, L, LANES), lambda b, c: (b, 0, c))
    out_shape = jax.ShapeDtypeStruct((B, L, D_B), F32)
    return pl.pallas_call(
        functools.partial(_hyena_pre_kernel, L=L),
        grid=(B, nb),
        in_specs=[col(0), col(1), col(2), col(3), cpar(0), cpar(1), cpar(2)],
        out_specs=[out_spec, out_spec],
        out_shape=[out_shape, jax.ShapeDtypeStruct((B, L, D_B), BF16)],
        compiler_params=_cparams("arbitrary", "arbitrary"),
        name="hyena_pre",
    )(px, px, px, px, conv_par, conv_par, conv_par)


def _pos_rows(L, order):
    f32 = np.float32
    t = np.linspace(0.0, 1.0, L, dtype=f32)[:, None]
    w = (f32(2.0 * math.pi) * np.arange(L, dtype=f32)[:, None]) / f32(L)
    f = np.linspace(1e-4, HY_BANDS - 1, HY_BANDS, dtype=f32)[None, :]
    z = np.concatenate([t, np.cos(f * w), -np.sin(f * w)], axis=-1).astype(f32)
    lag = np.arange(2 * L)
    src = np.where(lag < L, lag, 2 * L - lag) % L
    fwd = (lag < L).astype(f32)
    valid = (lag != L).astype(f32)
    rows = np.concatenate([z[src], fwd[:, None], valid[:, None],
                           np.zeros((2 * L, HY_WIDTH - HY_EMB - 2), f32)], axis=-1)
    return jnp.asarray(rows[order])


def _filter_kernel(z_ref, w1_ref, w2_ref, w3_ref, vec_ref, w4f_ref, w4b_ref, dl_ref, k_ref, h_scr,
                   *, n_rows):
    R = min(n_rows, 512)
    dl = dl_ref[...]

    @pl.when(pl.program_id(0) == 0)
    def _():
        vec = vec_ref[...]
        b1, b2, b3, fr = vec[0:1], vec[1:2], vec[2:3], vec[3:4]

        def mlp(r, carry):
            rows = pl.ds(pl.multiple_of(r * R, R), R)
            z = z_ref[rows, :]
            h = jnp.sin(fr * (jnp.dot(z, w1_ref[...], precision=HIGHEST, preferred_element_type=F32) + b1))
            h = jnp.sin(fr * (jnp.dot(h, w2_ref[...], precision=HIGHEST, preferred_element_type=F32) + b2))
            h_scr[rows, :] = jnp.sin(
                fr * (jnp.dot(h, w3_ref[...], precision=HIGHEST, preferred_element_type=F32) + b3))
            return carry

        lax.fori_loop(0, n_rows // R, mlp, 0)

    def body(r, acc):
        rows = pl.ds(pl.multiple_of(r * R, R), R)
        z = z_ref[rows, :]
        h = h_scr[rows, :]
        hf = jnp.dot(h, w4f_ref[...], precision=HIGHEST, preferred_element_type=F32)
        hb = jnp.dot(h, w4b_ref[...], precision=HIGHEST, preferred_element_type=F32)
        fwd = z[:, HY_EMB:HY_EMB + 1]
        valid = z[:, HY_EMB + 1:HY_EMB + 2]
        k = jnp.where(fwd > 0.5, hf, hb) * jnp.exp(-z[:, 0:1] * dl) * valid
        k_ref[rows, :] = k
        return acc + jnp.sum(jnp.abs(k), axis=0, keepdims=True)

    tot = lax.fori_loop(0, n_rows // R, body, jnp.zeros((1, LANES), F32))

    def scale(r, carry):
        rows = pl.ds(pl.multiple_of(r * R, R), R)
        k_ref[rows, :] = k_ref[rows, :] / tot
        return carry

    lax.fori_loop(0, n_rows // R, scale, 0)


def _hyena_filter(zrows, w1, b1, freq, w2, b2, w3, b3, w4):
    n_rows = zrows.shape[0]
    w1p = jnp.concatenate([w1, jnp.zeros((HY_WIDTH - HY_EMB, HY_WIDTH), F32)], axis=0)
    vec = jnp.concatenate([b1[None], b2[None], b3[None], freq[None],
                           jnp.zeros((SUBLANES - 4, HY_WIDTH), F32)], axis=0)
    deltas = jnp.abs(jnp.linspace(HY_MIN_DECAY, HY_MAX_DECAY, D_B, dtype=F32))[None, :]
    nb = D_B // LANES

    def const(shape):
        return pl.BlockSpec(shape, lambda c, nd=len(shape): (0,) * nd)

    return pl.pallas_call(
        functools.partial(_filter_kernel, n_rows=n_rows),
        grid=(nb,),
        in_specs=[const((n_rows, HY_WIDTH)), const((HY_WIDTH, HY_WIDTH)), const((HY_WIDTH, HY_WIDTH)),
                  const((HY_WIDTH, HY_WIDTH)), const((SUBLANES, HY_WIDTH)),
                  pl.BlockSpec((HY_WIDTH, LANES), lambda c: (0, c)),
                  pl.BlockSpec((HY_WIDTH, LANES), lambda c: (0, nb + c)),
                  pl.BlockSpec((1, LANES), lambda c: (0, c))],
        out_specs=pl.BlockSpec((n_rows, LANES), lambda c: (0, c)),
        out_shape=jax.ShapeDtypeStruct((n_rows, D_B), F32),
        scratch_shapes=[pltpu.VMEM((n_rows, HY_WIDTH), F32)],
        compiler_params=_cparams("arbitrary"),
        name="hyena_filter",
    )(zrows, w1p, w2, w3, vec, w4, w4, deltas)


def _cplx_block(gr, gi):
    return np.block([[gr, -gi], [gi, gr]])


def _hi_lo(a):
    a = jnp.asarray(a, F32)
    hi = a.astype(BF16)
    return hi, (a - hi.astype(F32)).astype(BF16)


def _dft_consts(L):
    N = 2 * L
    n2 = FFT_MINOR
    n1 = N // n2
    half = n1 // 2
    f1 = np.arange(n1)
    t2 = np.arange(n2)
    t1 = np.arange(n1)
    tt = n2 * t1[None, None, :] + t2[:, None, None]
    ang = -2.0 * np.pi * ((f1[None, :, None] * tt) % N) / N
    gr, gi = np.cos(ang), np.sin(ang)
    m1_data = np.stack([_cplx_block(gr[j][:, :half], gi[j][:, :half]) for j in range(n2)])
    m1_filt = np.concatenate([gr, gi], axis=1)
    m1_inv = np.stack([_cplx_block(gr[j][:, :half].T, -gi[j][:, :half].T) for j in range(n2)]) / N
    ang2 = -2.0 * np.pi * ((t2[:, None] * t2[None, :]) % n2) / n2
    f2 = _cplx_block(np.cos(ang2), np.sin(ang2))
    f2_inv = _cplx_block(np.cos(ang2), -np.sin(ang2))
    return dict(n1=n1, n2=n2, m1_data=_hi_lo(m1_data), m1_filt=_hi_lo(m1_filt), m1_inv=_hi_lo(m1_inv),
                f2=_hi_lo(f2[None]), f2_inv=_hi_lo(f2_inv[None]))


def _dft_single_consts(L):
    N = 2 * L
    f = np.arange(N)
    ang = -2.0 * np.pi * ((f[:, None] * f[None, :]) % N) / N
    gr, gi = np.cos(ang), np.sin(ang)
    fwd = _cplx_block(gr[:, :L], gi[:, :L])
    filt = np.concatenate([gr, gi], axis=0)
    inv = _cplx_block(gr[:L, :], -gi[:L, :]) / N
    return dict(fwd=_hi_lo(fwd[None]), filt=_hi_lo(filt[None]), inv=_hi_lo(inv[None]))


def _bmm_kernel(mh_ref, ml_ref, x_ref, o_ref, *, gblk, shared, precise):
    for j in range(gblk):
        jm = 0 if shared else j
        x = x_ref[j]
        xh = x.astype(BF16)
        acc = jnp.dot(mh_ref[jm], xh, preferred_element_type=F32)
        if precise:
            xl = (x - xh.astype(F32)).astype(BF16)
            acc = (acc + jnp.dot(mh_ref[jm], xl, preferred_element_type=F32)
                   + jnp.dot(ml_ref[jm], xh, preferred_element_type=F32))
        o_ref[j] = acc.astype(o_ref.dtype)


def _bmm_left(m, x, gblk, precise, out_dtype):
    mh, ml = m
    P, G, K, C = x.shape
    R = mh.shape[1]
    shared = mh.shape[0] == 1
    gblk = min(gblk, G)
    m_spec = (pl.BlockSpec((1, R, K), lambda g, p: (0, 0, 0)) if shared
              else pl.BlockSpec((gblk, R, K), lambda g, p: (g, 0, 0)))
    return pl.pallas_call(
        functools.partial(_bmm_kernel, gblk=gblk, shared=shared, precise=precise),
        grid=(G // gblk, P),
        in_specs=[m_spec, m_spec, pl.BlockSpec((None, gblk, K, C), lambda g, p: (p, g, 0, 0))],
        out_specs=pl.BlockSpec((None, gblk, R, C), lambda g, p: (p, g, 0, 0)),
        out_shape=jax.ShapeDtypeStruct((P, G, R, C), out_dtype),
        compiler_params=_cparams("arbitrary", "arbitrary"),
        name="dft_stage",
    )(mh, ml, x)


def _mid_kernel(fa_ref, fb_ref, kf_ref, d_ref, o_ref, *, gblk, nf):
    fa = fa_ref[...]
    fb = fb_ref[...]
    for j in range(gblk):
        xx = jnp.dot(fa, d_ref[j].astype(BF16), preferred_element_type=F32)
        xr, xi = xx[:nf], xx[nf:]
        kr, ki = kf_ref[j, :nf], kf_ref[j, nf:]
        yy = jnp.concatenate([xr * kr - xi * ki, xr * ki + xi * kr], axis=0)
        o_ref[j] = jnp.dot(fb, yy.astype(BF16), preferred_element_type=F32).astype(o_ref.dtype)


def _spectral_mid(fa, fb, kf, d, gblk, out_dtype):
    P, G, K, C = d.shape
    nf2 = fa.shape[0]
    Ko = fb.shape[0]
    gblk = min(gblk, G)
    return pl.pallas_call(
        functools.partial(_mid_kernel, gblk=gblk, nf=nf2 // 2),
        grid=(G // gblk, P),
        in_specs=[pl.BlockSpec(fa.shape, lambda g, p: (0, 0)),
                  pl.BlockSpec(fb.shape, lambda g, p: (0, 0)),
                  pl.BlockSpec((gblk, nf2, C), lambda g, p: (g, 0, 0)),
                  pl.BlockSpec((None, gblk, K, C), lambda g, p: (p, g, 0, 0))],
        out_specs=pl.BlockSpec((None, gblk, Ko, C), lambda g, p: (p, g, 0, 0)),
        out_shape=jax.ShapeDtypeStruct((P, G, Ko, C), out_dtype),
        compiler_params=_cparams("arbitrary", "arbitrary"),
        name="dft_mid",
    )(fa, fb, kf, d)


def _first_stage_kernel(mh_ref, ml_ref, x_ref, o_ref, *, gblk, n1, paired, precise):
    for j in range(gblk):
        if paired:
            x = jnp.concatenate([x_ref[0, :, j, :], x_ref[1, :, j, :]], axis=0)
        else:
            x = x_ref[j]
        xh = x.astype(BF16)
        acc = jnp.dot(mh_ref[j], xh, preferred_element_type=F32)
        if precise:
            xl = (x - xh.astype(F32)).astype(BF16)
            acc = (acc + jnp.dot(mh_ref[j], xl, preferred_element_type=F32)
                   + jnp.dot(ml_ref[j], xh, preferred_element_type=F32))
        o_ref[:, 0, j, :] = acc[:n1]
        o_ref[:, 1, j, :] = acc[n1:]


def _first_stage(m, x, n1, n2, paired, precise):
    mh, ml = m
    gblk = SUBLANES
    if paired:
        _, P, half, _, C = x.shape
        x_spec = pl.BlockSpec((2, None, half, gblk, C), lambda g, p: (0, p, 0, g, 0))
    else:
        P, _, _, C = x.shape
        x_spec = pl.BlockSpec((None, gblk, n1, C), lambda g, p: (p, g, 0, 0))
    m_spec = pl.BlockSpec((gblk,) + mh.shape[1:], lambda g, p: (g, 0, 0))
    return pl.pallas_call(
        functools.partial(_first_stage_kernel, gblk=gblk, n1=n1, paired=paired, precise=precise),
        grid=(n2 // gblk, P),
        in_specs=[m_spec, m_spec, x_spec],
        out_specs=pl.BlockSpec((None, n1, 2, gblk, C), lambda g, p: (p, 0, 0, g, 0)),
        out_shape=jax.ShapeDtypeStruct((P, n1, 2, n2, C), F32),
        compiler_params=_cparams("arbitrary", "arbitrary"),
        name="dft_first",
    )(mh, ml, x)


def _last_stage_kernel(m_ref, z_ref, y_ref, *, gblk, half):
    for j in range(gblk):
        z = jnp.concatenate([z_ref[:, 0, j, :], z_ref[:, 1, j, :]], axis=0).astype(BF16)
        y = jnp.dot(m_ref[j], z, preferred_element_type=F32)
        y_ref[0, :, j, :] = y[:half]
        y_ref[1, :, j, :] = y[half:]


def _last_stage(m, z, half):
    P, n1, _, n2, C = z.shape
    gblk = SUBLANES
    return pl.pallas_call(
        functools.partial(_last_stage_kernel, gblk=gblk, half=half),
        grid=(n2 // gblk, P),
        in_specs=[pl.BlockSpec((gblk,) + m.shape[1:], lambda g, p: (g, 0, 0)),
                  pl.BlockSpec((None, n1, 2, gblk, C), lambda g, p: (p, 0, 0, g, 0))],
        out_specs=pl.BlockSpec((2, None, half, gblk, C), lambda g, p: (0, p, 0, g, 0)),
        out_shape=jax.ShapeDtypeStruct((2, P, half, n2, C), F32),
        compiler_params=_cparams("arbitrary", "arbitrary"),
        name="dft_last",
    )(m, z)


def _long_conv_two_stage(vin, filt_rows, dc):
    B, L, C = vin.shape
    n1, n2 = dc["n1"], dc["n2"]
    half = n1 // 2
    P = B // 2
    ka = _first_stage(dc["m1_filt"], filt_rows.reshape(1, n2, n1, C), n1, n2, False, True)
    kf = _bmm_left(dc["f2"], ka.reshape(1, n1, 2 * n2, C), 8, True, F32)[0]
    a = _first_stage(dc["m1_data"], vin.reshape(2, P, half, n2, C), n1, n2, True, False)
    z = _spectral_mid(dc["f2"][0][0], dc["f2_inv"][0][0], kf, a.reshape(P, n1, 2 * n2, C), 4, F32)
    y = _last_stage(dc["m1_inv"][0], z.reshape(P, n1, 2, n2, C), half)
    return y.reshape(B, L, C)


def _long_conv_single(vin, filt, dc):
    B, L, C = vin.shape
    P = B // 2
    kf = _bmm_left(dc["filt"], filt.reshape(1, 1, 2 * L, C), 1, True, F32)[0]
    d = vin.reshape(2, P, L, C).transpose(1, 0, 2, 3).reshape(P, 1, 2 * L, C)
    y = _spectral_mid(dc["fwd"][0][0], dc["inv"][0][0], kf, d, 1, F32)
    return y.reshape(P, 2, L, C).transpose(1, 0, 2, 3).reshape(B, L, C)


def _outproj_kernel(*refs, final):
    if final:
        oa_ref, y_ref, vin_ref, m_ref, hb_ref, x_ref, gt_ref, w_ref, fw_ref, o_ref = refs
    else:
        oa_ref, y_ref, vin_ref, m_ref, hb_ref, x_ref, gt_ref, w_ref, o_ref = refs
    vin = vin_ref[...]
    ob = (y_ref[...] + vin * hb_ref[...]) * m_ref[...]
    r = (jnp.dot(oa_ref[...], w_ref[0:D_A, :], preferred_element_type=F32)
         + jnp.dot(ob.astype(BF16), w_ref[D_A:D_A + D_B, :], preferred_element_type=F32))
    xn = x_ref[...] + gt_ref[...] * r
    if final:
        ms = jnp.mean(xn * xn, axis=-1, keepdims=True)
        xn = xn * lax.rsqrt(ms + EPS) * fw_ref[...]
    o_ref[...] = xn


def _out_proj(oa, y, vin, m, hy_bias, x, gt, w_bf16, final_w):
    B, L, D = x.shape
    tm = min(L, 512)
    final = final_w is not None
    half = lambda: pl.BlockSpec((None, tm, D_B), lambda b, i: (b, i, 0))
    in_specs = [half(), half(), half(), half(),
                pl.BlockSpec((1, D_B), lambda b, i: (0, 0)),
                pl.BlockSpec((None, tm, D), lambda b, i: (b, i, 0)),
                pl.BlockSpec((None, 1, D), lambda b, i: (b, 0, 0)),
                pl.BlockSpec((D, D), lambda b, i: (0, 0))]
    args = [oa, y, vin, m, hy_bias.reshape(1, D_B), x, gt, w_bf16]
    if final:
        in_specs.append(pl.BlockSpec((1, D), lambda b, i: (0, 0)))
        args.append(final_w.reshape(1, D))
    return pl.pallas_call(
        functools.partial(_outproj_kernel, final=final),
        grid=(B, L // tm),
        in_specs=in_specs,
        out_specs=pl.BlockSpec((None, tm, D), lambda b, i: (b, i, 0)),
        out_shape=jax.ShapeDtypeStruct((B, L, D), F32),
        compiler_params=_cparams("arbitrary", "arbitrary"),
        name="out_proj",
    )(*args)


def _gate_params(lb):
    rows = []
    for d in range(2):
        rows += [jnp.log(lb[d]), jnp.log1p(-lb[d]), 1.0 - lb[d]]
    rows += [jnp.zeros_like(lb[0])] * (SUBLANES - len(rows))
    return jnp.stack(rows, axis=0)


def kernel(x, c, ctx, c_ctx, norm_w, w_ada, b_ada, w_in, w_out, lb_logits, g_norm_w,
           conv_w, conv_b, hy_w1, hy_b1, hy_freq, hy_w2, hy_b2, hy_w3, hy_b3, hy_w4,
           hy_bias, final_norm_w):
    B, L_lat, D = x.shape
    L_ctx = ctx.shape[1]
    p_lb = jax.nn.softmax(lb_logits.astype(F32), axis=0)
    lbs = jnp.cumsum(p_lb, axis=0)
    lbs = lbs - lbs[0:1]

    n_rows = 2 * SUBLANES
    cc = jnp.zeros((n_rows, D), F32).at[:B].set(c).at[B].set(c_ctx)
    mod = _modulation(cc, w_ada, b_ada)

    dc_lat = _dft_consts(L_lat)
    dc_ctx = _dft_single_consts(L_ctx)
    n1, n2 = dc_lat["n1"], dc_lat["n2"]
    order_lat = (np.arange(n2)[:, None] + n2 * np.arange(n1)[None, :]).reshape(-1)
    z_lat = _pos_rows(L_lat, order_lat)
    z_ctx = _pos_rows(L_ctx, np.arange(2 * L_ctx))
    zero_state = jnp.zeros((B, A_HEADS, HEAD_DIM, HEAD_DIM), F32)

    for l in range(DEPTH):
        last = l == DEPTH - 1
        sh_x, sc_x, gt_x = [mod[l, :B, j * D:(j + 1) * D].reshape(B, 1, D) for j in range(3)]
        sh_c, sc_c, gt_c = [jnp.broadcast_to(mod[l, B, j * D:(j + 1) * D].reshape(1, 1, D), (B, 1, D))
                            for j in range(3)]
        w_in_l = w_in[l].astype(BF16)
        w_out_l = w_out[l].astype(BF16)
        gate_par = _gate_params(lbs[l])
        conv_par = jnp.concatenate([conv_w[l], conv_b[l][None],
                                    jnp.zeros((SUBLANES - 4, 3 * D_B), F32)], axis=0)
        filt_args = (hy_w1[l], hy_b1[l], hy_freq[l], hy_w2[l], hy_b2[l], hy_w3[l], hy_b3[l], hy_w4[l])

        if last:
            pc = _in_proj(ctx, norm_w[l], sc_c, sh_c, gate_par, w_in_l[:, :4 * D_A])
            _, s_f, s_b = _hgrn(pc, g_norm_w[l], zero_state, zero_state, with_out=False)
        else:
            pc = _in_proj(ctx, norm_w[l], sc_c, sh_c, gate_par, w_in_l)
            oa_c, s_f, s_b = _hgrn(pc, g_norm_w[l], zero_state, zero_state, with_out=True)
            vin_c, m_c = _hyena_pre(pc, conv_par)
            filt_c = _hyena_filter(z_ctx, *filt_args)
            y_c = _long_conv_single(vin_c, filt_c, dc_ctx)
            ctx = _out_proj(oa_c, y_c, vin_c, m_c, hy_bias[l], ctx, gt_c, w_out_l, None)

        px = _in_proj(x, norm_w[l], sc_x, sh_x, gate_par, w_in_l)
        oa, _, _ = _hgrn(px, g_norm_w[l], s_f, s_b, with_out=True)
        vin, m = _hyena_pre(px, conv_par)
        filt_x = _hyena_filter(z_lat, *filt_args)
        y = _long_conv_two_stage(vin, filt_x, dc_lat)
        x = _out_proj(oa, y, vin, m, hy_bias[l], x, gt_x, w_out_l, final_norm_w if last else None)

    return x
```

```python
import functools
import math

import numpy as np
import jax
import jax.numpy as jnp
from jax import lax
from jax.experimental import pallas as pl
from jax.experimental.pallas import tpu as pltpu

F32 = jnp.float32
BF16 = jnp.bfloat16
HIGHEST = lax.Precision.HIGHEST

D_MODEL = 1024
DEPTH = 2
D_A = 512
D_B = 512
A_HEADS = 4
HEAD_DIM = 128
N_IN = 5 * D_A + 4 * D_B
HY_EMB = 33
HY_BANDS = 16
HY_WIDTH = 64
HY_MIN_DECAY = math.log(1e-2) / 1.5
HY_MAX_DECAY = math.log(1e-2) / 0.3
EPS = 1e-6

LANES = 128
SUBLANES = 8
SCAN_CHUNK = 64
SCAN_UNROLL = 8
SCAN_GUARD = 80.0
DFT_SLAB = 32
FFT_MINOR = 128
VMEM_LIMIT = 56 * 1024 * 1024


def _cparams(*sem):
    return pltpu.CompilerParams(dimension_semantics=sem, vmem_limit_bytes=VMEM_LIMIT)


def _mod_kernel(c_ref, w_ref, b_ref, o_ref):
    cc = c_ref[...]
    s = cc * jax.nn.sigmoid(cc)
    o_ref[...] = jnp.dot(s, w_ref[...], precision=HIGHEST,
                         preferred_element_type=F32) + b_ref[...]


def _modulation(cc, w_ada, b_ada):
    R, D = cc.shape
    N = w_ada.shape[-1]
    tn = 768
    return pl.pallas_call(
        _mod_kernel,
        grid=(DEPTH, N // tn),
        in_specs=[
            pl.BlockSpec((R, D), lambda l, j: (0, 0)),
            pl.BlockSpec((None, D, tn), lambda l, j: (l, 0, j)),
            pl.BlockSpec((None, 1, tn), lambda l, j: (l, 0, j)),
        ],
        out_specs=pl.BlockSpec((None, R, tn), lambda l, j: (l, 0, j)),
        out_shape=jax.ShapeDtypeStruct((DEPTH, R, N), F32),
        compiler_params=_cparams("arbitrary", "arbitrary"),
        name="modulation",
    )(cc, w_ada, b_ada.reshape(DEPTH, 1, N))


def _gates(f, par):
    loglb = par[0:1, :]
    log1mlb = par[1:2, :]
    onemlb = par[2:3, :]
    s1 = jnp.log(1.0 + jnp.exp(-jnp.abs(f)))
    b = log1mlb + (jnp.minimum(f, 0.0) - s1)
    ng = -jnp.maximum(loglb, b) - jnp.log(1.0 + jnp.exp(-jnp.abs(loglb - b)))
    k = onemlb * jnp.exp(-(jnp.maximum(f, 0.0) + s1))
    return ng, k


N_SCAN_GROUPS = 5


def _inproj_kernel(x_ref, nw_ref, sc_ref, sh_ref, gp_ref, w_ref, oa_ref, *rest):
    ob_ref = rest[0] if len(rest) == 2 else None
    hx_ref = rest[-1]
    x = x_ref[...]
    ms = jnp.mean(x * x, axis=-1, keepdims=True)
    y = x * lax.rsqrt(ms + EPS) * nw_ref[...]
    hx_ref[...] = (y * (1.0 + sc_ref[...]) + sh_ref[...]).astype(BF16)
    gp = gp_ref[...]
    w = D_A
    for g in range(w_ref.shape[-1] // w):
        r = jnp.dot(hx_ref[...], w_ref[:, g * w:(g + 1) * w], preferred_element_type=F32)
        if g == 0:
            oa_ref[:, 0:w] = r * jax.nn.sigmoid(r)
        elif g in (1, 2):
            ng, k = _gates(r, gp[3 * (g - 1):3 * g])
            oa_ref[:, (2 * g - 1) * w:2 * g * w] = ng
            oa_ref[:, 2 * g * w:(2 * g + 1) * w] = k
        elif g < N_SCAN_GROUPS:
            oa_ref[:, (g + 2) * w:(g + 3) * w] = r
        else:
            ob_ref[:, (g - N_SCAN_GROUPS) * w:(g - N_SCAN_GROUPS + 1) * w] = r.astype(ob_ref.dtype)


def _in_proj(x, norm_w, sc, sh, gate_par, w_bf16):
    B, L, D = x.shape
    N = w_bf16.shape[1]
    n_a = min(N // D_A, N_SCAN_GROUPS)
    Na = (n_a + 2) * D_A
    Nb = N - n_a * D_A
    tm = min(L, 512)
    out_specs = [pl.BlockSpec((None, tm, Na), lambda b, i: (b, i, 0))]
    out_shape = [jax.ShapeDtypeStruct((B, L, Na), F32)]
    if Nb:
        out_specs.append(pl.BlockSpec((None, tm, Nb), lambda b, i: (b, i, 0)))
        out_shape.append(jax.ShapeDtypeStruct((B, L, Nb), BF16))
    res = pl.pallas_call(
        _inproj_kernel,
        grid=(B, L // tm),
        in_specs=[
            pl.BlockSpec((None, tm, D), lambda b, i: (b, i, 0)),
            pl.BlockSpec((1, D), lambda b, i: (0, 0)),
            pl.BlockSpec((None, 1, D), lambda b, i: (b, 0, 0)),
            pl.BlockSpec((None, 1, D), lambda b, i: (b, 0, 0)),
            pl.BlockSpec((SUBLANES, D_A), lambda b, i: (0, 0)),
            pl.BlockSpec((D, N), lambda b, i: (0, 0)),
        ],
        out_specs=out_specs,
        out_shape=out_shape,
        scratch_shapes=[pltpu.VMEM((tm, D), BF16)],
        compiler_params=_cparams("arbitrary", "arbitrary"),
        name="in_proj",
    )(x, norm_w.reshape(1, D), sc, sh, gate_par, w_bf16)
    return (res[0], res[1]) if Nb else (res[0], None)


def _scan_consts(C, forward):
    idx = np.arange(C)
    i = idx[:, None]
    t = idx[None, :]
    if forward:
        mats = [t <= i, t > i]
    else:
        mats = [t >= i, t < i]
    masks = [i == t]
    h = 1
    while h < C:
        P = 2 * h
        p = i % P
        m = i - p + h
        upper = p >= h
        same = (i // P) == (t // P)
        if forward:
            mats.append(np.where(upper, (t >= m) & (t <= i), (t >= i + 1) & (t <= m - 1)))
            masks.append(same & upper & ((t % P) < h))
        else:
            mats.append(np.where(upper, (t >= m) & (t <= i - 1), (t >= i) & (t <= m - 1)))
            masks.append(same & (~upper) & ((t % P) >= h))
        h = P
    big = np.concatenate([m_.astype(np.float32) for m_ in mats], axis=0)
    big = np.concatenate([big, big, big], axis=1)
    msk = np.stack([m_.astype(np.float32) for m_ in masks], axis=0)
    return big, msk


def _dot_nt(a, b):
    return lax.dot_general(a, b, (((1,), (1,)), ((), ())), preferred_element_type=F32)


def _dot_tn(a, b):
    return lax.dot_general(a, b, (((0,), (0,)), ((), ())), preferred_element_type=F32)


def _split3(a):
    hi = a.astype(BF16)
    r1 = a - hi.astype(F32)
    mid = r1.astype(BF16)
    lo = (r1 - mid.astype(F32)).astype(BF16)
    return hi, mid, lo


def _cumulative(tri_ref, ng):
    return jnp.dot(tri_ref[...], jnp.concatenate(_split3(ng), axis=0), preferred_element_type=F32)


def _fast_stage_sums(q, ng, k, v, tri_ref):
    return dict(q=q, k=k, vb=v.astype(BF16), cum=_cumulative(tri_ref, ng))


def _fast_stage_scores(s, C, forward):
    h = C // 2
    q, k, cum = s["q"], s["k"], s["cum"]
    if forward:
        early, late = slice(0, h), slice(h, C)
        ref, tot = cum[h - 1:h, :], cum[C - 1:C, :]
    else:
        early, late = slice(h, C), slice(0, h)
        ref, tot = cum[h:h + 1, :], cum[0:1, :]
    qd = (q * jnp.exp(-cum)).astype(BF16)
    dl = cum - ref
    k_early = (k[early] * jnp.exp(cum[early])).astype(BF16)
    q_late = (q[late] * jnp.exp(-dl[late])).astype(BF16)
    k_all = (k * jnp.exp(dl)).astype(BF16)
    kd = (k * jnp.exp(cum - tot)).astype(BF16)
    return dict(vb=s["vb"], qd=qd, kd=kd, e_all=jnp.exp(-tot),
                guard=jnp.maximum(jnp.max(cum[early]), jnp.max(dl[late])),
                s_early=_dot_nt(qd[early], k_early), s_late=_dot_nt(q_late, k_all))


def _fast_stage_intra(s, C, forward):
    h = C // 2
    early = slice(0, h) if forward else slice(h, C)
    ri = lax.broadcasted_iota(jnp.int32, (h, h), 0)
    ci = lax.broadcasted_iota(jnp.int32, (h, h), 1)
    rl = lax.broadcasted_iota(jnp.int32, (h, C), 0) + (h if forward else 0)
    cl = lax.broadcasted_iota(jnp.int32, (h, C), 1)
    keep_e = (ci <= ri) if forward else (ci >= ri)
    keep_l = (cl <= rl) if forward else (cl >= rl)
    s_early = jnp.where(keep_e, s["s_early"], 0.0).astype(BF16)
    s_late = jnp.where(keep_l, s["s_late"], 0.0).astype(BF16)
    o_early = jnp.dot(s_early, s["vb"][early], preferred_element_type=F32)
    o_late = jnp.dot(s_late, s["vb"], preferred_element_type=F32)
    o_intra = jnp.concatenate([o_early, o_late] if forward else [o_late, o_early], axis=0)
    return dict(qd=s["qd"], e_all=s["e_all"], o_intra=o_intra, upd=_dot_tn(s["vb"], s["kd"]))


def _fast_stage_state(s, st):
    o = s["o_intra"] + _dot_nt(s["qd"], st.astype(BF16))
    return o, st * s["e_all"] + s["upd"]


def _scan_chunk(q, ng, k, v, big_ref, msk_ref, st, C, forward):
    psum =jnp.dot(big_ref[...], jnp.concatenate(_split3(ng), axis=0), preferred_element_type=F32)
    ex = jnp.exp(-psum)
    e_in = ex[0:C]
    e_out = ex[C:2 * C]
    nlev = msk_ref.shape[0] - 1
    o = _dot_nt((q * e_in).astype(BF16), st.astype(BF16))
    scores = msk_ref[0] * _dot_nt(q.astype(BF16), k.astype(BF16))
    for l in range(nlev):
        el = ex[(2 + l) * C:(3 + l) * C]
        scores = scores + msk_ref[1 + l] * _dot_nt((q * el).astype(BF16), (k * el).astype(BF16))
    o = o + jnp.dot(scores.astype(BF16), v.astype(BF16), preferred_element_type=F32)
    e_all = e_in[C - 1:C, :] if forward else e_in[0:1, :]
    st_new = st * e_all + _dot_tn(v.astype(BF16), (k * e_out).astype(BF16))
    return o, st_new


def _hgrn_kernel(*refs, L, C, U, with_out):
    if with_out:
        (q_ref, ngf_ref, kf_ref, ngb_ref, kb_ref, iv_ref, ga_ref, gw_ref, bigf_ref, mskf_ref,
         bigb_ref, mskb_ref, s0f_ref, s0b_ref, o_ref, sf_ref, sb_ref, sf_in, sb_in, of_scr, ob_scr) = refs
    else:
        (q_ref, ngf_ref, kf_ref, ngb_ref, kb_ref, iv_ref, bigf_ref, mskf_ref, bigb_ref, mskb_ref,
         s0f_ref, s0b_ref, sf_ref, sb_ref, sf_in, sb_in) = refs
    n = L // C
    sf_ref[...] = s0f_ref[...]
    sb_ref[...] = s0b_ref[...]
    trif_ref = bigf_ref.at[0:C, :]
    trib_ref = bigb_ref.at[0:C, :]

    def rows_of(it, u):
        c = it * U + u
        return (pl.ds(pl.multiple_of(c * C, C), C), pl.ds(pl.multiple_of((n - 1 - c) * C, C), C))

    def put(rf, rb, o_f, o_b):
        if with_out:
            of_scr[rf, :] = o_f
            ob_scr[rb, :] = o_b

    def body(it, carry):
        sf_in[...] = sf_ref[...]
        sb_in[...] = sb_ref[...]
        rows = [rows_of(it, u) for u in range(U)]
        fwd = [_fast_stage_sums(q_ref[rf, :], ngf_ref[rf, :], kf_ref[rf, :], iv_ref[rf, :], trif_ref)
               for rf, _ in rows]
        bwd = [_fast_stage_sums(q_ref[rb, :], ngb_ref[rb, :], kb_ref[rb, :], iv_ref[rb, :], trib_ref)
               for _, rb in rows]
        fwd = [_fast_stage_scores(s, C, True) for s in fwd]
        bwd = [_fast_stage_scores(s, C, False) for s in bwd]
        guard = functools.reduce(jnp.maximum, [s["guard"] for s in fwd + bwd])
        fwd = [_fast_stage_intra(s, C, True) for s in fwd]
        bwd = [_fast_stage_intra(s, C, False) for s in bwd]
        st_f, st_b = sf_ref[...], sb_ref[...]
        for u in range(U):
            o_f, st_f = _fast_stage_state(fwd[u], st_f)
            o_b, st_b = _fast_stage_state(bwd[u], st_b)
            put(rows[u][0], rows[u][1], o_f, o_b)
        sf_ref[...] = st_f
        sb_ref[...] = st_b

        @pl.when(jnp.logical_not(guard <= SCAN_GUARD))
        def _():
            sf_ref[...] = sf_in[...]
            sb_ref[...] = sb_in[...]

            def redo(u, carry2):
                rf, rb = rows_of(it, u)
                o_f, s_f = _scan_chunk(q_ref[rf, :], ngf_ref[rf, :], kf_ref[rf, :], iv_ref[rf, :],
                                       bigf_ref, mskf_ref, sf_ref[...], C, True)
                sf_ref[...] = s_f
                o_b, s_b = _scan_chunk(q_ref[rb, :], ngb_ref[rb, :], kb_ref[rb, :], iv_ref[rb, :],
                                       bigb_ref, mskb_ref, sb_ref[...], C, False)
                sb_ref[...] = s_b
                put(rf, rb, o_f, o_b)
                return carry2

            lax.fori_loop(0, U, redo, 0)

        return carry

    lax.fori_loop(0, n // U, body, 0)

    if with_out:
        gw = gw_ref[...]
        R = min(L, 512)

        def fin(r, carry):
            rows = pl.ds(pl.multiple_of(r * R, R), R)
            o = of_scr[rows, :] + ob_scr[rows, :]
            o = o * lax.rsqrt(jnp.mean(o * o, axis=-1, keepdims=True) + EPS) * gw
            ga = ga_ref[rows, :]
            o_ref[rows, :] = (o * (ga * jax.nn.sigmoid(ga))).astype(o_ref.dtype)
            return carry

        lax.fori_loop(0, L // R, fin, 0)


def _hgrn(px, g_norm_w, s0f, s0b, with_out):
    B, L, _ = px.shape
    C = SCAN_CHUNK
    H = A_HEADS
    bigf, mskf = _scan_consts(C, True)
    bigb, mskb = _scan_consts(C, False)
    bigf, bigb = jnp.asarray(bigf, BF16), jnp.asarray(bigb, BF16)
    mskf, mskb = jnp.asarray(mskf, F32), jnp.asarray(mskb, F32)

    def col(j):
        return pl.BlockSpec((None, L, LANES), lambda b, h, j=j: (b, 0, j * H + h))

    def const(a):
        return pl.BlockSpec(a.shape, lambda b, h, nd=a.ndim: (0,) * nd)

    st_spec = pl.BlockSpec((None, None, HEAD_DIM, HEAD_DIM), lambda b, h: (b, h, 0, 0))
    st_shape = jax.ShapeDtypeStruct((B, H, HEAD_DIM, HEAD_DIM), F32)
    in_specs = [col(j) for j in range(6)]
    args = [px] * 6
    if with_out:
        in_specs += [col(6), pl.BlockSpec((1, LANES), lambda b, h: (0, h))]
        args += [px, g_norm_w.reshape(1, D_A)]
    in_specs += [const(bigf), const(mskf), const(bigb), const(mskb), st_spec, st_spec]
    args += [bigf, mskf, bigb, mskb, s0f, s0b]
    out_specs = [st_spec, st_spec]
    out_shape = [st_shape, st_shape]
    scratch = [pltpu.VMEM((HEAD_DIM, HEAD_DIM), F32), pltpu.VMEM((HEAD_DIM, HEAD_DIM), F32)]
    if with_out:
        out_specs = [pl.BlockSpec((None, L, LANES), lambda b, h: (b, 0, h))] + out_specs
        out_shape = [jax.ShapeDtypeStruct((B, L, D_A), BF16)] + out_shape
        scratch += [pltpu.VMEM((L, LANES), F32), pltpu.VMEM((L, LANES), F32)]
    res = pl.pallas_call(
        functools.partial(_hgrn_kernel, L=L, C=C, U=min(SCAN_UNROLL, L // C), with_out=with_out),
        grid=(B, H),
        in_specs=in_specs,
        out_specs=out_specs,
        out_shape=out_shape,
        scratch_shapes=scratch,
        compiler_params=_cparams("arbitrary", "arbitrary"),
        name="hgrn2",
    )(*args)
    if with_out:
        return res[0], res[1], res[2]
    return None, res[0], res[1]


def _conv3(ref, r0, R, L, cp):
    T = 2 * SUBLANES
    cur = ref[pl.ds(r0, R), :].astype(F32)
    prev_tile = ref[pl.ds(pl.multiple_of(jnp.maximum(r0 - T, 0), T), T), :].astype(F32)
    next_tile = ref[pl.ds(pl.multiple_of(jnp.minimum(r0 + R, L - T), T), T), :].astype(F32)
    prow = jnp.where(r0 > 0, prev_tile[T - 1:T, :], 0.0)
    nrow = jnp.where(r0 + R < L, next_tile[0:1, :], 0.0)
    rid = lax.broadcasted_iota(jnp.int32, (R, LANES), 0)
    up = jnp.where(rid == 0, prow, pltpu.roll(cur, 1, 0))
    dn = jnp.where(rid == R - 1, nrow, pltpu.roll(cur, R - 1, 0))
    return cp[3:4, :] + up * cp[0:1, :] + cur * cp[1:2, :] + dn * cp[2:3, :]


def _hyena_pre_kernel(x0_ref, x1_ref, v_ref, gb_ref, c0_ref, c1_ref, c2_ref, vin_ref, m_ref, *, L):
    R = min(L, 256)
    c0 = c0_ref[...]
    c1 = c1_ref[...]
    c2 = c2_ref[...]

    def body(r, carry):
        r0 = pl.multiple_of(r * R, R)
        rows = pl.ds(r0, R)
        x0 = _conv3(x0_ref, r0, R, L, c0)
        x1 = _conv3(x1_ref, r0, R, L, c1)
        vv = _conv3(v_ref, r0, R, L, c2)
        gb = gb_ref[rows, :].astype(F32)
        vin_ref[rows, :] = vv * x1
        m_ref[rows, :] = (x0 * (gb * jax.nn.sigmoid(gb))).astype(m_ref.dtype)
        return carry

    lax.fori_loop(0, L // R, body, 0)


def _hyena_pre(px, conv_par):
    B, L, _ = px.shape
    nb = D_B // LANES

    def col(j):
        return pl.BlockSpec((None, L, LANES), lambda b, c, j=j: (b, 0, j * nb + c))

    def cpar(j):
        return pl.BlockSpec((SUBLANES, LANES), lambda b, c, j=j: (0, j * nb + c))

    out_spec = pl.BlockSpec((None, L, LANES), lambda b, c: (b, 0, c))
    out_shape = jax.ShapeDtypeStruct((B, L, D_B), F32)
    return pl.pallas_call(
        functools.partial(_hyena_pre_kernel, L=L),
        grid=(B, nb),
        in_specs=[col(0), col(1), col(2), col(3), cpar(0), cpar(1), cpar(2)],
        out_specs=[out_spec, out_spec],
        out_shape=[out_shape, jax.ShapeDtypeStruct((B, L, D_B), BF16)],
        compiler_params=_cparams("arbitrary", "arbitrary"),
        name="hyena_pre",
    )(px, px, px, px, conv_par, conv_par, conv_par)


def _pos_rows(L, order):
    f32 = np.float32
    t = np.linspace(0.0, 1.0, L, dtype=f32)[:, None]
    w = (f32(2.0 * math.pi) * np.arange(L, dtype=f32)[:, None]) / f32(L)
    f = np.linspace(1e-4, HY_BANDS - 1, HY_BANDS, dtype=f32)[None, :]
    z = np.concatenate([t, np.cos(f * w), -np.sin(f * w)], axis=-1).astype(f32)
    lag = np.arange(2 * L)
    src = np.where(lag < L, lag, 2 * L - lag) % L
    fwd = (lag < L).astype(f32)
    valid = (lag != L).astype(f32)
    rows = np.concatenate([z[src], fwd[:, None], valid[:, None],
                           np.zeros((2 * L, HY_WIDTH - HY_EMB - 2), f32)], axis=-1)
    return jnp.asarray(rows[order])


def _filter_kernel(z_ref, w1_ref, w2_ref, w3_ref, vec_ref, w4f_ref, w4b_ref, dl_ref, k_ref, h_scr,
                   *, n_rows):
    R = min(n_rows, 512)
    dl = dl_ref[...]

    @pl.when(pl.program_id(0) == 0)
    def _():
        vec = vec_ref[...]
        b1, b2, b3, fr = vec[0:1], vec[1:2], vec[2:3], vec[3:4]

        def mlp(r, carry):
            rows = pl.ds(pl.multiple_of(r * R, R), R)
            z = z_ref[rows, :]
            h = jnp.sin(fr * (jnp.dot(z, w1_ref[...], precision=HIGHEST, preferred_element_type=F32) + b1))
            h = jnp.sin(fr * (jnp.dot(h, w2_ref[...], precision=HIGHEST, preferred_element_type=F32) + b2))
            h_scr[rows, :] = jnp.sin(
                fr * (jnp.dot(h, w3_ref[...], precision=HIGHEST, preferred_element_type=F32) + b3))
            return carry

        lax.fori_loop(0, n_rows // R, mlp, 0)

    def body(r, acc):
        rows = pl.ds(pl.multiple_of(r * R, R), R)
        z = z_ref[rows, :]
        h = h_scr[rows, :]
        hf = jnp.dot(h, w4f_ref[...], precision=HIGHEST, preferred_element_type=F32)
        hb = jnp.dot(h, w4b_ref[...], precision=HIGHEST, preferred_element_type=F32)
        fwd = z[:, HY_EMB:HY_EMB + 1]
        valid = z[:, HY_EMB + 1:HY_EMB + 2]
        k = jnp.where(fwd > 0.5, hf, hb) * jnp.exp(-z[:, 0:1] * dl) * valid
        k_ref[rows, :] = k
        return acc + jnp.sum(jnp.abs(k), axis=0, keepdims=True)

    tot = lax.fori_loop(0, n_rows // R, body, jnp.zeros((1, LANES), F32))

    def scale(r, carry):
        rows = pl.ds(pl.multiple_of(r * R, R), R)
        k_ref[rows, :] = k_ref[rows, :] / tot
        return carry

    lax.fori_loop(0, n_rows // R, scale, 0)


def _hyena_filter(zrows, w1, b1, freq, w2, b2, w3, b3, w4):
    n_rows = zrows.shape[0]
    w1p = jnp.concatenate([w1, jnp.zeros((HY_WIDTH - HY_EMB, HY_WIDTH), F32)], axis=0)
    vec = jnp.concatenate([b1[None], b2[None], b3[None], freq[None],
                           jnp.zeros((SUBLANES - 4, HY_WIDTH), F32)], axis=0)
    deltas = jnp.abs(jnp.linspace(HY_MIN_DECAY, HY_MAX_DECAY, D_B, dtype=F32))[None, :]
    nb = D_B // LANES

    def const(shape):
        return pl.BlockSpec(shape, lambda c, nd=len(shape): (0,) * nd)

    return pl.pallas_call(
        functools.partial(_filter_kernel, n_rows=n_rows),
        grid=(nb,),
        in_specs=[const((n_rows, HY_WIDTH)), const((HY_WIDTH, HY_WIDTH)), const((HY_WIDTH, HY_WIDTH)),
                  const((HY_WIDTH, HY_WIDTH)), const((SUBLANES, HY_WIDTH)),
                  pl.BlockSpec((HY_WIDTH, LANES), lambda c: (0, c)),
                  pl.BlockSpec((HY_WIDTH, LANES), lambda c: (0, nb + c)),
                  pl.BlockSpec((1, LANES), lambda c: (0, c))],
        out_specs=pl.BlockSpec((n_rows, LANES), lambda c: (0, c)),
        out_shape=jax.ShapeDtypeStruct((n_rows, D_B), F32),
        scratch_shapes=[pltpu.VMEM((n_rows, HY_WIDTH), F32)],
        compiler_params=_cparams("arbitrary"),
        name="hyena_filter",
    )(zrows, w1p, w2, w3, vec, w4, w4, deltas)


def _cplx_block(gr, gi):
    return np.block([[gr, -gi], [gi, gr]])


def _hi_lo(a):
    a = jnp.asarray(a, F32)
    hi = a.astype(BF16)
    return hi, (a - hi.astype(F32)).astype(BF16)


def _dft_consts(L):
    N = 2 * L
    n2 = FFT_MINOR
    n1 = N // n2
    half = n1 // 2
    f1 = np.arange(n1)
    t2 = np.arange(n2)
    t1 = np.arange(n1)
    tt = n2 * t1[None, None, :] + t2[:, None, None]
    ang = -2.0 * np.pi * ((f1[None, :, None] * tt) % N) / N
    gr, gi = np.cos(ang), np.sin(ang)
    m1_data = np.stack([_cplx_block(gr[j][:, :half], gi[j][:, :half]) for j in range(n2)])
    m1_filt = np.concatenate([gr, gi], axis=1)
    m1_inv = np.stack([_cplx_block(gr[j][:, :half].T, -gi[j][:, :half].T) for j in range(n2)]) / N
    ang2 = -2.0 * np.pi * ((t2[:, None] * t2[None, :]) % n2) / n2
    f2 = _cplx_block(np.cos(ang2), np.sin(ang2))
    f2_inv = _cplx_block(np.cos(ang2), -np.sin(ang2))
    return dict(n1=n1, n2=n2, m1_data=_hi_lo(m1_data), m1_filt=_hi_lo(m1_filt), m1_inv=_hi_lo(m1_inv),
                f2=_hi_lo(f2[None]), f2_inv=_hi_lo(f2_inv[None]))


def _dft_single_consts(L):
    N = 2 * L
    f = np.arange(N)
    ang = -2.0 * np.pi * ((f[:, None] * f[None, :]) % N) / N
    gr, gi = np.cos(ang), np.sin(ang)
    fwd = _cplx_block(gr[:, :L], gi[:, :L])
    filt = np.concatenate([gr, gi], axis=0)
    inv = _cplx_block(gr[:L, :], -gi[:L, :]) / N
    return dict(fwd=_hi_lo(fwd[None]), filt=_hi_lo(filt[None]), inv=_hi_lo(inv[None]))


def _bmm_kernel(mh_ref, ml_ref, x_ref, o_ref, *, gblk, shared, precise):
    for j in range(gblk):
        jm = 0 if shared else j
        x = x_ref[j]
        xh = x.astype(BF16)
        acc = jnp.dot(mh_ref[jm], xh, preferred_element_type=F32)
        if precise:
            xl = (x - xh.astype(F32)).astype(BF16)
            acc = (acc + jnp.dot(mh_ref[jm], xl, preferred_element_type=F32)
                   + jnp.dot(ml_ref[jm], xh, preferred_element_type=F32))
        o_ref[j] = acc.astype(o_ref.dtype)


def _bmm_left(m, x, gblk, precise, out_dtype):
    mh, ml = m
    P, G, K, C = x.shape
    R = mh.shape[1]
    shared = mh.shape[0] == 1
    gblk = min(gblk, G)
    m_spec = (pl.BlockSpec((1, R, K), lambda g, p: (0, 0, 0)) if shared
              else pl.BlockSpec((gblk, R, K), lambda g, p: (g, 0, 0)))
    return pl.pallas_call(
        functools.partial(_bmm_kernel, gblk=gblk, shared=shared, precise=precise),
        grid=(G // gblk, P),
        in_specs=[m_spec, m_spec, pl.BlockSpec((None, gblk, K, C), lambda g, p: (p, g, 0, 0))],
        out_specs=pl.BlockSpec((None, gblk, R, C), lambda g, p: (p, g, 0, 0)),
        out_shape=jax.ShapeDtypeStruct((P, G, R, C), out_dtype),
        compiler_params=_cparams("arbitrary", "arbitrary"),
        name="dft_stage",
    )(mh, ml, x)


def _mid_kernel(fa_ref, fb_ref, kf_ref, d_ref, o_ref, *, gblk, nf):
    fa = fa_ref[...]
    fb = fb_ref[...]
    for j in range(gblk):
        xx = jnp.dot(fa, d_ref[j].astype(BF16), preferred_element_type=F32)
        xr, xi = xx[:nf], xx[nf:]
        kr, ki = kf_ref[j, :nf], kf_ref[j, nf:]
        yy = jnp.concatenate([xr * kr - xi * ki, xr * ki + xi * kr], axis=0)
        o_ref[j] = jnp.dot(fb, yy.astype(BF16), preferred_element_type=F32).astype(o_ref.dtype)


def _spectral_mid(fa, fb, kf, d, gblk, out_dtype):
    P, G, K, C = d.shape
    nf2 = fa.shape[0]
    Ko = fb.shape[0]
    gblk = min(gblk, G)
    return pl.pallas_call(
        functools.partial(_mid_kernel, gblk=gblk, nf=nf2 // 2),
        grid=(G // gblk, P),
        in_specs=[pl.BlockSpec(fa.shape, lambda g, p: (0, 0)),
                  pl.BlockSpec(fb.shape, lambda g, p: (0, 0)),
                  pl.BlockSpec((gblk, nf2, C), lambda g, p: (g, 0, 0)),
                  pl.BlockSpec((None, gblk, K, C), lambda g, p: (p, g, 0, 0))],
        out_specs=pl.BlockSpec((None, gblk, Ko, C), lambda g, p: (p, g, 0, 0)),
        out_shape=jax.ShapeDtypeStruct((P, G, Ko, C), out_dtype),
        compiler_params=_cparams("arbitrary", "arbitrary"),
        name="dft_mid",
    )(fa, fb, kf, d)


def _first_stage_kernel(mh_ref, ml_ref, x_ref, o_ref, *, gblk, n1, paired, precise):
    for j in range(gblk):
        if paired:
            x = jnp.concatenate([x_ref[0, :, j, :], x_ref[1, :, j, :]], axis=0)
        else:
            x = x_ref[j]
        xh = x.astype(BF16)
        acc = jnp.dot(mh_ref[j], xh, preferred_element_type=F32)
        if precise:
            xl = (x - xh.astype(F32)).astype(BF16)
            acc = (acc + jnp.dot(mh_ref[j], xl, preferred_element_type=F32)
                   + jnp.dot(ml_ref[j], xh, preferred_element_type=F32))
        o_ref[:, 0, j, :] = acc[:n1]
        o_ref[:, 1, j, :] = acc[n1:]


def _first_stage(m, x, n1, n2, paired, precise):
    mh, ml = m
    gblk = min(DFT_SLAB, n2)
    if paired:
        _, P, half, _, C = x.shape
        x_spec = pl.BlockSpec((2, None, half, gblk, C), lambda g, p: (0, p, 0, g, 0))
    else:
        P, _, _, C = x.shape
        x_spec = pl.BlockSpec((None, gblk, n1, C), lambda g, p: (p, g, 0, 0))
    m_spec = pl.BlockSpec((gblk,) + mh.shape[1:], lambda g, p: (g, 0, 0))
    return pl.pallas_call(
        functools.partial(_first_stage_kernel, gblk=gblk, n1=n1, paired=paired, precise=precise),
        grid=(n2 // gblk, P),
        in_specs=[m_spec, m_spec, x_spec],
        out_specs=pl.BlockSpec((None, n1, 2, gblk, C), lambda g, p: (p, 0, 0, g, 0)),
        out_shape=jax.ShapeDtypeStruct((P, n1, 2, n2, C), F32),
        compiler_params=_cparams("arbitrary", "arbitrary"),
        name="dft_first",
    )(mh, ml, x)


def _last_stage_kernel(m_ref, z_ref, y_ref, *, gblk, half):
    for j in range(gblk):
        z = jnp.concatenate([z_ref[:, 0, j, :], z_ref[:, 1, j, :]], axis=0).astype(BF16)
        y = jnp.dot(m_ref[j], z, preferred_element_type=F32)
        y_ref[0, :, j, :] = y[:half]
        y_ref[1, :, j, :] = y[half:]


def _last_stage(m, z, half):
    P, n1, _, n2, C = z.shape
    gblk = min(DFT_SLAB, n2)
    return pl.pallas_call(
        functools.partial(_last_stage_kernel, gblk=gblk, half=half),
        grid=(n2 // gblk, P),
        in_specs=[pl.BlockSpec((gblk,) + m.shape[1:], lambda g, p: (g, 0, 0)),
                  pl.BlockSpec((None, n1, 2, gblk, C), lambda g, p: (p, 0, 0, g, 0))],
        out_specs=pl.BlockSpec((2, None, half, gblk, C), lambda g, p: (0, p, 0, g, 0)),
        out_shape=jax.ShapeDtypeStruct((2, P, half, n2, C), F32),
        compiler_params=_cparams("arbitrary", "arbitrary"),
        name="dft_last",
    )(m, z)


def _long_conv_two_stage(vin, filt_rows, dc):
    B, L, C = vin.shape
    n1, n2 = dc["n1"], dc["n2"]
    half = n1 // 2
    P = B // 2
    ka = _first_stage(dc["m1_filt"], filt_rows.reshape(1, n2, n1, C), n1, n2, False, True)
    kf = _bmm_left(dc["f2"], ka.reshape(1, n1, 2 * n2, C), 8, True, F32)[0]
    a = _first_stage(dc["m1_data"], vin.reshape(2, P, half, n2, C), n1, n2, True, False)
    z = _spectral_mid(dc["f2"][0][0], dc["f2_inv"][0][0], kf, a.reshape(P, n1, 2 * n2, C), 8, F32)
    y = _last_stage(dc["m1_inv"][0], z.reshape(P, n1, 2, n2, C), half)
    return y.reshape(B, L, C)


def _long_conv_single(vin, filt, dc):
    B, L, C = vin.shape
    P = B // 2
    kf = _bmm_left(dc["filt"], filt.reshape(1, 1, 2 * L, C), 1, True, F32)[0]
    d = vin.reshape(2, P, L, C).transpose(1, 0, 2, 3).reshape(P, 1, 2 * L, C)
    y = _spectral_mid(dc["fwd"][0][0], dc["inv"][0][0], kf, d, 1, F32)
    return y.reshape(P, 2, L, C).transpose(1, 0, 2, 3).reshape(B, L, C)


def _outproj_kernel(*refs, final):
    if final:
        oa_ref, y_ref, vin_ref, m_ref, hb_ref, x_ref, gt_ref, w_ref, fw_ref, o_ref = refs
    else:
        oa_ref, y_ref, vin_ref, m_ref, hb_ref, x_ref, gt_ref, w_ref, o_ref = refs
    vin = vin_ref[...]
    ob = (y_ref[...] + vin * hb_ref[...]) * m_ref[...]
    r = (jnp.dot(oa_ref[...], w_ref[0:D_A, :], preferred_element_type=F32)
         + jnp.dot(ob.astype(BF16), w_ref[D_A:D_A + D_B, :], preferred_element_type=F32))
    xn = x_ref[...] + gt_ref[...] * r
    if final:
        ms = jnp.mean(xn * xn, axis=-1, keepdims=True)
        xn = xn * lax.rsqrt(ms + EPS) * fw_ref[...]
    o_ref[...] = xn


def _out_proj(oa, y, vin, m, hy_bias, x, gt, w_bf16, final_w):
    B, L, D = x.shape
    tm = min(L, 512)
    final = final_w is not None
    half = lambda: pl.BlockSpec((None, tm, D_B), lambda b, i: (b, i, 0))
    in_specs = [half(), half(), half(), half(),
                pl.BlockSpec((1, D_B), lambda b, i: (0, 0)),
                pl.BlockSpec((None, tm, D), lambda b, i: (b, i, 0)),
                pl.BlockSpec((None, 1, D), lambda b, i: (b, 0, 0)),
                pl.BlockSpec((D, D), lambda b, i: (0, 0))]
    args = [oa, y, vin, m, hy_bias.reshape(1, D_B), x, gt, w_bf16]
    if final:
        in_specs.append(pl.BlockSpec((1, D), lambda b, i: (0, 0)))
        args.append(final_w.reshape(1, D))
    return pl.pallas_call(
        functools.partial(_outproj_kernel, final=final),
        grid=(B, L // tm),
        in_specs=in_specs,
        out_specs=pl.BlockSpec((None, tm, D), lambda b, i: (b, i, 0)),
        out_shape=jax.ShapeDtypeStruct((B, L, D), F32),
        compiler_params=_cparams("arbitrary", "arbitrary"),
        name="out_proj",
    )(*args)


def _gate_params(lb):
    rows = []
    for d in range(2):
        rows += [jnp.log(lb[d]), jnp.log1p(-lb[d]), 1.0 - lb[d]]
    rows += [jnp.zeros_like(lb[0])] * (SUBLANES - len(rows))
    return jnp.stack(rows, axis=0)


def kernel(x, c, ctx, c_ctx, norm_w, w_ada, b_ada, w_in, w_out, lb_logits, g_norm_w,
           conv_w, conv_b, hy_w1, hy_b1, hy_freq, hy_w2, hy_b2, hy_w3, hy_b3, hy_w4,
           hy_bias, final_norm_w):
    B, L_lat, D = x.shape
    L_ctx = ctx.shape[1]
    p_lb = jax.nn.softmax(lb_logits.astype(F32), axis=0)
    lbs = jnp.cumsum(p_lb, axis=0)
    lbs = lbs - lbs[0:1]

    n_rows = 2 * SUBLANES
    cc = jnp.zeros((n_rows, D), F32).at[:B].set(c).at[B].set(c_ctx)
    mod = _modulation(cc, w_ada, b_ada)

    dc_lat = _dft_consts(L_lat)
    dc_ctx = _dft_single_consts(L_ctx)
    n1, n2 = dc_lat["n1"], dc_lat["n2"]
    order_lat = (np.arange(n2)[:, None] + n2 * np.arange(n1)[None, :]).reshape(-1)
    z_lat = _pos_rows(L_lat, order_lat)
    z_ctx = _pos_rows(L_ctx, np.arange(2 * L_ctx))
    zero_state = jnp.zeros((B, A_HEADS, HEAD_DIM, HEAD_DIM), F32)

    for l in range(DEPTH):
        last = l == DEPTH - 1
        sh_x, sc_x, gt_x = [mod[l, :B, j * D:(j + 1) * D].reshape(B, 1, D) for j in range(3)]
        sh_c, sc_c, gt_c = [jnp.broadcast_to(mod[l, B, j * D:(j + 1) * D].reshape(1, 1, D), (B, 1, D))
                            for j in range(3)]
        w_in_l = w_in[l].astype(BF16)
        w_out_l = w_out[l].astype(BF16)
        gate_par = _gate_params(lbs[l])
        conv_par = jnp.concatenate([conv_w[l], conv_b[l][None],
                                    jnp.zeros((SUBLANES - 4, 3 * D_B), F32)], axis=0)
        filt_args = (hy_w1[l], hy_b1[l], hy_freq[l], hy_w2[l], hy_b2[l], hy_w3[l], hy_b3[l], hy_w4[l])

        if last:
            pc, _ = _in_proj(ctx, norm_w[l], sc_c, sh_c, gate_par, w_in_l[:, :4 * D_A])
            _, s_f, s_b = _hgrn(pc, g_norm_w[l], zero_state, zero_state, with_out=False)
        else:
            pc, pc_hy = _in_proj(ctx, norm_w[l], sc_c, sh_c, gate_par, w_in_l)
            oa_c, s_f, s_b = _hgrn(pc, g_norm_w[l], zero_state, zero_state, with_out=True)
            vin_c, m_c = _hyena_pre(pc_hy, conv_par)
            filt_c = _hyena_filter(z_ctx, *filt_args)
            y_c = _long_conv_single(vin_c, filt_c, dc_ctx)
            ctx = _out_proj(oa_c, y_c, vin_c, m_c, hy_bias[l], ctx, gt_c, w_out_l, None)

        px, px_hy = _in_proj(x, norm_w[l], sc_x, sh_x, gate_par, w_in_l)
        oa, _, _ = _hgrn(px, g_norm_w[l], s_f, s_b, with_out=True)
        vin, m = _hyena_pre(px_hy, conv_par)
        filt_x = _hyena_filter(z_lat, *filt_args)
        y = _long_conv_two_stage(vin, filt_x, dc_lat)
        x = _out_proj(oa, y, vin, m, hy_bias[l], x, gt_x, w_out_l, final_norm_w if last else None)

    return x
```

```python
import functools
import math

import numpy as np
import jax
import jax.numpy as jnp
from jax import lax
from jax.experimental import pallas as pl
from jax.experimental.pallas import tpu as pltpu

F32 = jnp.float32
BF16 = jnp.bfloat16
HIGHEST = lax.Precision.HIGHEST

D_MODEL = 1024
DEPTH = 2
D_A = 512
D_B = 512
A_HEADS = 4
HEAD_DIM = 128
N_IN = 5 * D_A + 4 * D_B
HY_EMB = 33
HY_BANDS = 16
HY_WIDTH = 64
HY_MIN_DECAY = math.log(1e-2) / 1.5
HY_MAX_DECAY = math.log(1e-2) / 0.3
EPS = 1e-6

LANES = 128
SUBLANES = 8
SCAN_CHUNK = 64
SCAN_UNROLL = 8
SCAN_GUARD = 80.0
DFT_SLAB = 32
FFT_MINOR = 128
VMEM_LIMIT = 56 * 1024 * 1024


def _cparams(*sem):
    return pltpu.CompilerParams(dimension_semantics=sem, vmem_limit_bytes=VMEM_LIMIT)


def _mod_kernel(c_ref, w_ref, b_ref, o_ref):
    cc = c_ref[...]
    s = cc * jax.nn.sigmoid(cc)
    o_ref[...] = jnp.dot(s, w_ref[...], precision=HIGHEST,
                         preferred_element_type=F32) + b_ref[...]


def _modulation(cc, w_ada, b_ada):
    R, D = cc.shape
    N = w_ada.shape[-1]
    tn = 768
    return pl.pallas_call(
        _mod_kernel,
        grid=(DEPTH, N // tn),
        in_specs=[
            pl.BlockSpec((R, D), lambda l, j: (0, 0)),
            pl.BlockSpec((None, D, tn), lambda l, j: (l, 0, j)),
            pl.BlockSpec((None, 1, tn), lambda l, j: (l, 0, j)),
        ],
        out_specs=pl.BlockSpec((None, R, tn), lambda l, j: (l, 0, j)),
        out_shape=jax.ShapeDtypeStruct((DEPTH, R, N), F32),
        compiler_params=_cparams("arbitrary", "arbitrary"),
        name="modulation",
    )(cc, w_ada, b_ada.reshape(DEPTH, 1, N))


def _gates(f, par):
    loglb = par[0:1, :]
    log1mlb = par[1:2, :]
    onemlb = par[2:3, :]
    s1 = jnp.log(1.0 + jnp.exp(-jnp.abs(f)))
    b = log1mlb + (jnp.minimum(f, 0.0) - s1)
    ng = -jnp.maximum(loglb, b) - jnp.log(1.0 + jnp.exp(-jnp.abs(loglb - b)))
    k = onemlb * jnp.exp(-(jnp.maximum(f, 0.0) + s1))
    return ng, k


N_SCAN_GROUPS = 5


def _inproj_kernel(x_ref, nw_ref, sc_ref, sh_ref, gp_ref, w_ref, oa_ref, *rest):
    ob_ref = rest[0] if len(rest) == 2 else None
    hx_ref = rest[-1]
    x = x_ref[...]
    ms = jnp.mean(x * x, axis=-1, keepdims=True)
    y = x * lax.rsqrt(ms + EPS) * nw_ref[...]
    hx_ref[...] = (y * (1.0 + sc_ref[...]) + sh_ref[...]).astype(BF16)
    gp = gp_ref[...]
    w = D_A
    for g in range(w_ref.shape[-1] // w):
        r = jnp.dot(hx_ref[...], w_ref[:, g * w:(g + 1) * w], preferred_element_type=F32)
        if g == 0:
            oa_ref[:, 0:w] = r * jax.nn.sigmoid(r)
        elif g in (1, 2):
            ng, k = _gates(r, gp[3 * (g - 1):3 * g])
            oa_ref[:, (2 * g - 1) * w:2 * g * w] = ng
            oa_ref[:, 2 * g * w:(2 * g + 1) * w] = k
        elif g < N_SCAN_GROUPS:
            oa_ref[:, (g + 2) * w:(g + 3) * w] = r
        else:
            ob_ref[:, (g - N_SCAN_GROUPS) * w:(g - N_SCAN_GROUPS + 1) * w] = r.astype(ob_ref.dtype)


def _in_proj(x, norm_w, sc, sh, gate_par, w_bf16):
    B, L, D = x.shape
    N = w_bf16.shape[1]
    n_a = min(N // D_A, N_SCAN_GROUPS)
    Na = (n_a + 2) * D_A
    Nb = N - n_a * D_A
    tm = min(L, 512)
    out_specs = [pl.BlockSpec((None, tm, Na), lambda b, i: (b, i, 0))]
    out_shape = [jax.ShapeDtypeStruct((B, L, Na), F32)]
    if Nb:
        out_specs.append(pl.BlockSpec((None, tm, Nb), lambda b, i: (b, i, 0)))
        out_shape.append(jax.ShapeDtypeStruct((B, L, Nb), BF16))
    res = pl.pallas_call(
        _inproj_kernel,
        grid=(B, L // tm),
        in_specs=[
            pl.BlockSpec((None, tm, D), lambda b, i: (b, i, 0)),
            pl.BlockSpec((1, D), lambda b, i: (0, 0)),
            pl.BlockSpec((None, 1, D), lambda b, i: (b, 0, 0)),
            pl.BlockSpec((None, 1, D), lambda b, i: (b, 0, 0)),
            pl.BlockSpec((SUBLANES, D_A), lambda b, i: (0, 0)),
            pl.BlockSpec((D, N), lambda b, i: (0, 0)),
        ],
        out_specs=out_specs,
        out_shape=out_shape,
        scratch_shapes=[pltpu.VMEM((tm, D), BF16)],
        compiler_params=_cparams("arbitrary", "arbitrary"),
        name="in_proj",
    )(x, norm_w.reshape(1, D), sc, sh, gate_par, w_bf16)
    return (res[0], res[1]) if Nb else (res[0], None)


def _scan_consts(C, forward):
    idx = np.arange(C)
    i = idx[:, None]
    t = idx[None, :]
    if forward:
        mats = [t <= i, t > i]
    else:
        mats = [t >= i, t < i]
    masks = [i == t]
    h = 1
    while h < C:
        P = 2 * h
        p = i % P
        m = i - p + h
        upper = p >= h
        same = (i // P) == (t // P)
        if forward:
            mats.append(np.where(upper, (t >= m) & (t <= i), (t >= i + 1) & (t <= m - 1)))
            masks.append(same & upper & ((t % P) < h))
        else:
            mats.append(np.where(upper, (t >= m) & (t <= i - 1), (t >= i) & (t <= m - 1)))
            masks.append(same & (~upper) & ((t % P) >= h))
        h = P
    big = np.concatenate([m_.astype(np.float32) for m_ in mats], axis=0)
    big = np.concatenate([big, big, big], axis=1)
    msk = np.stack([m_.astype(np.float32) for m_ in masks], axis=0)
    return big, msk


def _dot_nt(a, b):
    return lax.dot_general(a, b, (((1,), (1,)), ((), ())), preferred_element_type=F32)


def _dot_tn(a, b):
    return lax.dot_general(a, b, (((0,), (0,)), ((), ())), preferred_element_type=F32)


def _split3(a):
    hi = a.astype(BF16)
    r1 = a - hi.astype(F32)
    mid = r1.astype(BF16)
    lo = (r1 - mid.astype(F32)).astype(BF16)
    return hi, mid, lo


def _cumulative(tri_ref, ng):
    return jnp.dot(tri_ref[...], jnp.concatenate(_split3(ng), axis=0), preferred_element_type=F32)


def _fast_stage_sums(q, ng, k, v, tri_ref):
    return dict(q=q, k=k, vb=v.astype(BF16), cum=_cumulative(tri_ref, ng))


def _fast_stage_scores(s, C, forward):
    h = C // 2
    q, k, cum = s["q"], s["k"], s["cum"]
    if forward:
        early, late = slice(0, h), slice(h, C)
        ref, tot = cum[h - 1:h, :], cum[C - 1:C, :]
    else:
        early, late = slice(h, C), slice(0, h)
        ref, tot = cum[h:h + 1, :], cum[0:1, :]
    qd = (q * jnp.exp(-cum)).astype(BF16)
    dl = cum - ref
    k_early = (k[early] * jnp.exp(cum[early])).astype(BF16)
    q_late = (q[late] * jnp.exp(-dl[late])).astype(BF16)
    k_all = (k * jnp.exp(dl)).astype(BF16)
    kd = (k * jnp.exp(cum - tot)).astype(BF16)
    return dict(vb=s["vb"], qd=qd, kd=kd, e_all=jnp.exp(-tot),
                guard=jnp.maximum(jnp.max(cum[early]), jnp.max(dl[late])),
                s_early=_dot_nt(qd[early], k_early), s_late=_dot_nt(q_late, k_all))


def _fast_stage_intra(s, C, forward):
    h = C // 2
    early = slice(0, h) if forward else slice(h, C)
    ri = lax.broadcasted_iota(jnp.int32, (h, h), 0)
    ci = lax.broadcasted_iota(jnp.int32, (h, h), 1)
    rl = lax.broadcasted_iota(jnp.int32, (h, C), 0) + (h if forward else 0)
    cl = lax.broadcasted_iota(jnp.int32, (h, C), 1)
    keep_e = (ci <= ri) if forward else (ci >= ri)
    keep_l = (cl <= rl) if forward else (cl >= rl)
    s_early = jnp.where(keep_e, s["s_early"], 0.0).astype(BF16)
    s_late = jnp.where(keep_l, s["s_late"], 0.0).astype(BF16)
    o_early = jnp.dot(s_early, s["vb"][early], preferred_element_type=F32)
    o_late = jnp.dot(s_late, s["vb"], preferred_element_type=F32)
    o_intra = jnp.concatenate([o_early, o_late] if forward else [o_late, o_early], axis=0)
    return dict(qd=s["qd"], e_all=s["e_all"], o_intra=o_intra, upd=_dot_tn(s["vb"], s["kd"]))


def _fast_stage_state(s, st):
    o = s["o_intra"] + _dot_nt(s["qd"], st.astype(BF16))
    return o, st * s["e_all"] + s["upd"]


def _scan_chunk(q, ng, k, v, big_ref, msk_ref, st, C, forward):
    psum =jnp.dot(big_ref[...], jnp.concatenate(_split3(ng), axis=0), preferred_element_type=F32)
    ex = jnp.exp(-psum)
    e_in = ex[0:C]
    e_out = ex[C:2 * C]
    nlev = msk_ref.shape[0] - 1
    o = _dot_nt((q * e_in).astype(BF16), st.astype(BF16))
    scores = msk_ref[0] * _dot_nt(q.astype(BF16), k.astype(BF16))
    for l in range(nlev):
        el = ex[(2 + l) * C:(3 + l) * C]
        scores = scores + msk_ref[1 + l] * _dot_nt((q * el).astype(BF16), (k * el).astype(BF16))
    o = o + jnp.dot(scores.astype(BF16), v.astype(BF16), preferred_element_type=F32)
    e_all = e_in[C - 1:C, :] if forward else e_in[0:1, :]
    st_new = st * e_all + _dot_tn(v.astype(BF16), (k * e_out).astype(BF16))
    return o, st_new


def _hgrn_kernel(*refs, L, C, U, with_out):
    if with_out:
        (q_ref, ngf_ref, kf_ref, ngb_ref, kb_ref, iv_ref, ga_ref, gw_ref, bigf_ref, mskf_ref,
         bigb_ref, mskb_ref, s0f_ref, s0b_ref, o_ref, sf_ref, sb_ref, sf_in, sb_in, of_scr, ob_scr) = refs
    else:
        (q_ref, ngf_ref, kf_ref, ngb_ref, kb_ref, iv_ref, bigf_ref, mskf_ref, bigb_ref, mskb_ref,
         s0f_ref, s0b_ref, sf_ref, sb_ref, sf_in, sb_in) = refs
    n = L // C
    sf_ref[...] = s0f_ref[...]
    sb_ref[...] = s0b_ref[...]
    trif_ref = bigf_ref.at[0:C, :]
    trib_ref = bigb_ref.at[0:C, :]

    def rows_of(it, u):
        c = it * U + u
        return (pl.ds(pl.multiple_of(c * C, C), C), pl.ds(pl.multiple_of((n - 1 - c) * C, C), C))

    def put(rf, rb, o_f, o_b):
        if with_out:
            of_scr[rf, :] = o_f
            ob_scr[rb, :] = o_b

    def body(it, carry):
        sf_in[...] = sf_ref[...]
        sb_in[...] = sb_ref[...]
        rows = [rows_of(it, u) for u in range(U)]
        fwd = [_fast_stage_sums(q_ref[rf, :], ngf_ref[rf, :], kf_ref[rf, :], iv_ref[rf, :], trif_ref)
               for rf, _ in rows]
        bwd = [_fast_stage_sums(q_ref[rb, :], ngb_ref[rb, :], kb_ref[rb, :], iv_ref[rb, :], trib_ref)
               for _, rb in rows]
        fwd = [_fast_stage_scores(s, C, True) for s in fwd]
        bwd = [_fast_stage_scores(s, C, False) for s in bwd]
        guard = functools.reduce(jnp.maximum, [s["guard"] for s in fwd + bwd])
        fwd = [_fast_stage_intra(s, C, True) for s in fwd]
        bwd = [_fast_stage_intra(s, C, False) for s in bwd]
        st_f, st_b = sf_ref[...], sb_ref[...]
        for u in range(U):
            o_f, st_f = _fast_stage_state(fwd[u], st_f)
            o_b, st_b = _fast_stage_state(bwd[u], st_b)
            put(rows[u][0], rows[u][1], o_f, o_b)
        sf_ref[...] = st_f
        sb_ref[...] = st_b

        @pl.when(jnp.logical_not(guard <= SCAN_GUARD))
        def _():
            sf_ref[...] = sf_in[...]
            sb_ref[...] = sb_in[...]

            def redo(u, carry2):
                rf, rb = rows_of(it, u)
                o_f, s_f = _scan_chunk(q_ref[rf, :], ngf_ref[rf, :], kf_ref[rf, :], iv_ref[rf, :],
                                       bigf_ref, mskf_ref, sf_ref[...], C, True)
                sf_ref[...] = s_f
                o_b, s_b = _scan_chunk(q_ref[rb, :], ngb_ref[rb, :], kb_ref[rb, :], iv_ref[rb, :],
                                       bigb_ref, mskb_ref, sb_ref[...], C, False)
                sb_ref[...] = s_b
                put(rf, rb, o_f, o_b)
                return carry2

            lax.fori_loop(0, U, redo, 0)

        return carry

    lax.fori_loop(0, n // U, body, 0)

    if with_out:
        gw = gw_ref[...]
        R = min(L, 512)

        def fin(r, carry):
            rows = pl.ds(pl.multiple_of(r * R, R), R)
            o = of_scr[rows, :] + ob_scr[rows, :]
            o = o * lax.rsqrt(jnp.mean(o * o, axis=-1, keepdims=True) + EPS) * gw
            ga = ga_ref[rows, :]
            o_ref[rows, :] = (o * (ga * jax.nn.sigmoid(ga))).astype(o_ref.dtype)
            return carry

        lax.fori_loop(0, L // R, fin, 0)


def _hgrn(px, g_norm_w, s0f, s0b, with_out):
    B, L, _ = px.shape
    C = SCAN_CHUNK
    H = A_HEADS
    bigf, mskf = _scan_consts(C, True)
    bigb, mskb = _scan_consts(C, False)
    bigf, bigb = jnp.asarray(bigf, BF16), jnp.asarray(bigb, BF16)
    mskf, mskb = jnp.asarray(mskf, F32), jnp.asarray(mskb, F32)

    def col(j):
        return pl.BlockSpec((None, L, LANES), lambda b, h, j=j: (b, 0, j * H + h))

    def const(a):
        return pl.BlockSpec(a.shape, lambda b, h, nd=a.ndim: (0,) * nd)

    st_spec = pl.BlockSpec((None, None, HEAD_DIM, HEAD_DIM), lambda b, h: (b, h, 0, 0))
    st_shape = jax.ShapeDtypeStruct((B, H, HEAD_DIM, HEAD_DIM), F32)
    in_specs = [col(j) for j in range(6)]
    args = [px] * 6
    if with_out:
        in_specs += [col(6), pl.BlockSpec((1, LANES), lambda b, h: (0, h))]
        args += [px, g_norm_w.reshape(1, D_A)]
    in_specs += [const(bigf), const(mskf), const(bigb), const(mskb), st_spec, st_spec]
    args += [bigf, mskf, bigb, mskb, s0f, s0b]
    out_specs = [st_spec, st_spec]
    out_shape = [st_shape, st_shape]
    scratch = [pltpu.VMEM((HEAD_DIM, HEAD_DIM), F32), pltpu.VMEM((HEAD_DIM, HEAD_DIM), F32)]
    if with_out:
        out_specs = [pl.BlockSpec((None, L, LANES), lambda b, h: (b, 0, h))] + out_specs
        out_shape = [jax.ShapeDtypeStruct((B, L, D_A), BF16)] + out_shape
        scratch += [pltpu.VMEM((L, LANES), F32), pltpu.VMEM((L, LANES), F32)]
    res = pl.pallas_call(
        functools.partial(_hgrn_kernel, L=L, C=C, U=min(SCAN_UNROLL, L // C), with_out=with_out),
        grid=(B, H),
        in_specs=in_specs,
        out_specs=out_specs,
        out_shape=out_shape,
        scratch_shapes=scratch,
        compiler_params=_cparams("arbitrary", "arbitrary"),
        name="hgrn2",
    )(*args)
    if with_out:
        return res[0], res[1], res[2]
    return None, res[0], res[1]


def _conv3(ref, r0, R, L, cp):
    T = 2 * SUBLANES
    cur = ref[pl.ds(r0, R), :].astype(F32)
    prev_tile = ref[pl.ds(pl.multiple_of(jnp.maximum(r0 - T, 0), T), T), :].astype(F32)
    next_tile = ref[pl.ds(pl.multiple_of(jnp.minimum(r0 + R, L - T), T), T), :].astype(F32)
    prow = jnp.where(r0 > 0, prev_tile[T - 1:T, :], 0.0)
    nrow = jnp.where(r0 + R < L, next_tile[0:1, :], 0.0)
    rid = lax.broadcasted_iota(jnp.int32, (R, LANES), 0)
    up = jnp.where(rid == 0, prow, pltpu.roll(cur, 1, 0))
    dn = jnp.where(rid == R - 1, nrow, pltpu.roll(cur, R - 1, 0))
    return cp[3:4, :] + up * cp[0:1, :] + cur * cp[1:2, :] + dn * cp[2:3, :]


def _hyena_pre_kernel(x0_ref, x1_ref, v_ref, gb_ref, c0_ref, c1_ref, c2_ref, vin_ref, m_ref, *, L):
    R = min(L, 256)
    c0 = c0_ref[...]
    c1 = c1_ref[...]
    c2 = c2_ref[...]

    def body(r, carry):
        r0 = pl.multiple_of(r * R, R)
        rows = pl.ds(r0, R)
        x0 = _conv3(x0_ref, r0, R, L, c0)
        x1 = _conv3(x1_ref, r0, R, L, c1)
        vv = _conv3(v_ref, r0, R, L, c2)
        gb = gb_ref[rows, :].astype(F32)
        vin_ref[rows, :] = vv * x1
        m_ref[rows, :] = (x0 * (gb * jax.nn.sigmoid(gb))).astype(m_ref.dtype)
        return carry

    lax.fori_loop(0, L // R, body, 0)


def _hyena_pre(px, conv_par):
    B, L, _ = px.shape
    nb = D_B // LANES

    def col(j):
        return pl.BlockSpec((None, L, LANES), lambda b, c, j=j: (b, 0, j * nb + c))

    def cpar(j):
        return pl.BlockSpec((SUBLANES, LANES), lambda b, c, j=j: (0, j * nb + c))

    out_spec = pl.BlockSpec((None, L, LANES), lambda b, c: (b, 0, c))
    out_shape = jax.ShapeDtypeStruct((B, L, D_B), F32)
    return pl.pallas_call(
        functools.partial(_hyena_pre_kernel, L=L),
        grid=(B, nb),
        in_specs=[col(0), col(1), col(2), col(3), cpar(0), cpar(1), cpar(2)],
        out_specs=[out_spec, out_spec],
        out_shape=[out_shape, jax.ShapeDtypeStruct((B, L, D_B), BF16)],
        compiler_params=_cparams("arbitrary", "arbitrary"),
        name="hyena_pre",
    )(px, px, px, px, conv_par, conv_par, conv_par)


def _pos_rows(L, order):
    f32 = np.float32
    t = np.linspace(0.0, 1.0, L, dtype=f32)[:, None]
    w = (f32(2.0 * math.pi) * np.arange(L, dtype=f32)[:, None]) / f32(L)
    f = np.linspace(1e-4, HY_BANDS - 1, HY_BANDS, dtype=f32)[None, :]
    z = np.concatenate([t, np.cos(f * w), -np.sin(f * w)], axis=-1).astype(f32)
    lag = np.arange(2 * L)
    src = np.where(lag < L, lag, 2 * L - lag) % L
    fwd = (lag < L).astype(f32)
    valid = (lag != L).astype(f32)
    rows = np.concatenate([z[src], fwd[:, None], valid[:, None],
                           np.zeros((2 * L, HY_WIDTH - HY_EMB - 2), f32)], axis=-1)
    return jnp.asarray(rows[order])


def _filter_kernel(z_ref, w1_ref, w2_ref, w3_ref, vec_ref, w4f_ref, w4b_ref, dl_ref, k_ref, h_scr,
                   *, n_rows):
    R = min(n_rows, 512)
    dl = dl_ref[...]

    @pl.when(pl.program_id(0) == 0)
    def _():
        vec = vec_ref[...]
        b1, b2, b3, fr = vec[0:1], vec[1:2], vec[2:3], vec[3:4]

        def mlp(r, carry):
            rows = pl.ds(pl.multiple_of(r * R, R), R)
            z = z_ref[rows, :]
            h = jnp.sin(fr * (jnp.dot(z, w1_ref[...], precision=HIGHEST, preferred_element_type=F32) + b1))
            h = jnp.sin(fr * (jnp.dot(h, w2_ref[...], precision=HIGHEST, preferred_element_type=F32) + b2))
            h_scr[rows, :] = jnp.sin(
                fr * (jnp.dot(h, w3_ref[...], precision=HIGHEST, preferred_element_type=F32) + b3))
            return carry

        lax.fori_loop(0, n_rows // R, mlp, 0)

    def body(r, acc):
        rows = pl.ds(pl.multiple_of(r * R, R), R)
        z = z_ref[rows, :]
        h = h_scr[rows, :]
        hf = jnp.dot(h, w4f_ref[...], precision=HIGHEST, preferred_element_type=F32)
        hb = jnp.dot(h, w4b_ref[...], precision=HIGHEST, preferred_element_type=F32)
        fwd = z[:, HY_EMB:HY_EMB + 1]
        valid = z[:, HY_EMB + 1:HY_EMB + 2]
        k = jnp.where(fwd > 0.5, hf, hb) * jnp.exp(-z[:, 0:1] * dl) * valid
        k_ref[rows, :] = k
        return acc + jnp.sum(jnp.abs(k), axis=0, keepdims=True)

    tot = lax.fori_loop(0, n_rows // R, body, jnp.zeros((1, LANES), F32))

    def scale(r, carry):
        rows = pl.ds(pl.multiple_of(r * R, R), R)
        k_ref[rows, :] = k_ref[rows, :] / tot
        return carry

    lax.fori_loop(0, n_rows // R, scale, 0)


def _hyena_filter(zrows, w1, b1, freq, w2, b2, w3, b3, w4):
    n_rows = zrows.shape[0]
    w1p = jnp.concatenate([w1, jnp.zeros((HY_WIDTH - HY_EMB, HY_WIDTH), F32)], axis=0)
    vec = jnp.concatenate([b1[None], b2[None], b3[None], freq[None],
                           jnp.zeros((SUBLANES - 4, HY_WIDTH), F32)], axis=0)
    deltas = jnp.abs(jnp.linspace(HY_MIN_DECAY, HY_MAX_DECAY, D_B, dtype=F32))[None, :]
    nb = D_B // LANES

    def const(shape):
        return pl.BlockSpec(shape, lambda c, nd=len(shape): (0,) * nd)

    return pl.pallas_call(
        functools.partial(_filter_kernel, n_rows=n_rows),
        grid=(nb,),
        in_specs=[const((n_rows, HY_WIDTH)), const((HY_WIDTH, HY_WIDTH)), const((HY_WIDTH, HY_WIDTH)),
                  const((HY_WIDTH, HY_WIDTH)), const((SUBLANES, HY_WIDTH)),
                  pl.BlockSpec((HY_WIDTH, LANES), lambda c: (0, c)),
                  pl.BlockSpec((HY_WIDTH, LANES), lambda c: (0, nb + c)),
                  pl.BlockSpec((1, LANES), lambda c: (0, c))],
        out_specs=pl.BlockSpec((n_rows, LANES), lambda c: (0, c)),
        out_shape=jax.ShapeDtypeStruct((n_rows, D_B), F32),
        scratch_shapes=[pltpu.VMEM((n_rows, HY_WIDTH), F32)],
        compiler_params=_cparams("arbitrary"),
        name="hyena_filter",
    )(zrows, w1p, w2, w3, vec, w4, w4, deltas)


def _cplx_block(gr, gi):
    return np.block([[gr, -gi], [gi, gr]])


def _interleave(n):
    return np.stack([np.arange(n), n + np.arange(n)], axis=1).reshape(-1)


def _hi_lo(a):
    a = jnp.asarray(a, F32)
    hi = a.astype(BF16)
    return hi, (a - hi.astype(F32)).astype(BF16)


def _dft_consts(L):
    N = 2 * L
    n2 = FFT_MINOR
    n1 = N // n2
    half = n1 // 2
    f1 = np.arange(n1)
    t2 = np.arange(n2)
    t1 = np.arange(n1)
    tt = n2 * t1[None, None, :] + t2[:, None, None]
    ang = -2.0 * np.pi * ((f1[None, :, None] * tt) % N) / N
    gr, gi = np.cos(ang), np.sin(ang)
    m1_data = np.stack([_cplx_block(gr[j][:, :half], gi[j][:, :half]) for j in range(n2)])
    m1_filt = np.concatenate([gr, gi], axis=1)
    m1_inv = np.stack([_cplx_block(gr[j][:, :half].T, -gi[j][:, :half].T) for j in range(n2)]) / N
    ang2 = -2.0 * np.pi * ((t2[:, None] * t2[None, :]) % n2) / n2
    f2 = _cplx_block(np.cos(ang2), np.sin(ang2))
    f2_inv = _cplx_block(np.cos(ang2), -np.sin(ang2))
    il1 = _interleave(n1)
    il2 = _interleave(n2)
    as_bf16 = lambda a: jnp.asarray(a, F32).astype(BF16)
    return dict(n1=n1, n2=n2, m1_filt=_hi_lo(m1_filt), f2=_hi_lo(f2[None]),
                m1_data=as_bf16(m1_data[:, il1, :]), f2_packed=as_bf16(f2[:, il2]),
                f2_inv_packed=as_bf16(f2_inv[il2, :]), m1_inv=as_bf16(m1_inv[:, :, il1]))


def _dft_single_consts(L):
    N = 2 * L
    f = np.arange(N)
    ang = -2.0 * np.pi * ((f[:, None] * f[None, :]) % N) / N
    gr, gi = np.cos(ang), np.sin(ang)
    fwd = _cplx_block(gr[:, :L], gi[:, :L])
    filt = np.concatenate([gr, gi], axis=0)
    inv = _cplx_block(gr[:L, :], -gi[:L, :]) / N
    return dict(fwd=_hi_lo(fwd[None]), filt=_hi_lo(filt[None]), inv=_hi_lo(inv[None]))


def _bmm_kernel(mh_ref, ml_ref, x_ref, o_ref, *, gblk, shared, precise):
    for j in range(gblk):
        jm = 0 if shared else j
        x = x_ref[j]
        xh = x.astype(BF16)
        acc = jnp.dot(mh_ref[jm], xh, preferred_element_type=F32)
        if precise:
            xl = (x - xh.astype(F32)).astype(BF16)
            acc = (acc + jnp.dot(mh_ref[jm], xl, preferred_element_type=F32)
                   + jnp.dot(ml_ref[jm], xh, preferred_element_type=F32))
        o_ref[j] = acc.astype(o_ref.dtype)


def _bmm_left(m, x, gblk, precise, out_dtype):
    mh, ml = m
    P, G, K, C = x.shape
    R = mh.shape[1]
    shared = mh.shape[0] == 1
    gblk = min(gblk, G)
    m_spec = (pl.BlockSpec((1, R, K), lambda g, p: (0, 0, 0)) if shared
              else pl.BlockSpec((gblk, R, K), lambda g, p: (g, 0, 0)))
    return pl.pallas_call(
        functools.partial(_bmm_kernel, gblk=gblk, shared=shared, precise=precise),
        grid=(G // gblk, P),
        in_specs=[m_spec, m_spec, pl.BlockSpec((None, gblk, K, C), lambda g, p: (p, g, 0, 0))],
        out_specs=pl.BlockSpec((None, gblk, R, C), lambda g, p: (p, g, 0, 0)),
        out_shape=jax.ShapeDtypeStruct((P, G, R, C), out_dtype),
        compiler_params=_cparams("arbitrary", "arbitrary"),
        name="dft_stage",
    )(mh, ml, x)


def _mid_kernel(fa_ref, fb_ref, kf_ref, d_ref, o_ref, *, gblk, nf):
    fa = fa_ref[...]
    fb = fb_ref[...]
    for j in range(gblk):
        xx = jnp.dot(fa, d_ref[j].astype(BF16), preferred_element_type=F32)
        xr, xi = xx[:nf], xx[nf:]
        kr, ki = kf_ref[j, :nf], kf_ref[j, nf:]
        yy = jnp.concatenate([xr * kr - xi * ki, xr * ki + xi * kr], axis=0)
        o_ref[j] = jnp.dot(fb, yy.astype(BF16), preferred_element_type=F32).astype(o_ref.dtype)


def _spectral_mid(fa, fb, kf, d, gblk, out_dtype):
    P, G, K, C = d.shape
    nf2 = fa.shape[0]
    Ko = fb.shape[0]
    gblk = min(gblk, G)
    return pl.pallas_call(
        functools.partial(_mid_kernel, gblk=gblk, nf=nf2 // 2),
        grid=(G // gblk, P),
        in_specs=[pl.BlockSpec(fa.shape, lambda g, p: (0, 0)),
                  pl.BlockSpec(fb.shape, lambda g, p: (0, 0)),
                  pl.BlockSpec((gblk, nf2, C), lambda g, p: (g, 0, 0)),
                  pl.BlockSpec((None, gblk, K, C), lambda g, p: (p, g, 0, 0))],
        out_specs=pl.BlockSpec((None, gblk, Ko, C), lambda g, p: (p, g, 0, 0)),
        out_shape=jax.ShapeDtypeStruct((P, G, Ko, C), out_dtype),
        compiler_params=_cparams("arbitrary", "arbitrary"),
        name="dft_mid",
    )(fa, fb, kf, d)


def _filter_first_kernel(mh_ref, ml_ref, x_ref, o_ref, *, gblk, n1):
    for j in range(gblk):
        x = x_ref[j]
        xh = x.astype(BF16)
        xl = (x - xh.astype(F32)).astype(BF16)
        acc = (jnp.dot(mh_ref[j], xh, preferred_element_type=F32)
               + jnp.dot(mh_ref[j], xl, preferred_element_type=F32)
               + jnp.dot(ml_ref[j], xh, preferred_element_type=F32))
        o_ref[:, 0, j, :] = acc[:n1]
        o_ref[:, 1, j, :] = acc[n1:]


def _filter_first_stage(m, x, n1, n2):
    mh, ml = m
    C = x.shape[-1]
    gblk = min(DFT_SLAB, n2)
    m_spec = pl.BlockSpec((gblk,) + mh.shape[1:], lambda g: (g, 0, 0))
    return pl.pallas_call(
        functools.partial(_filter_first_kernel, gblk=gblk, n1=n1),
        grid=(n2 // gblk,),
        in_specs=[m_spec, m_spec, pl.BlockSpec((gblk, n1, C), lambda g: (g, 0, 0))],
        out_specs=pl.BlockSpec((n1, 2, gblk, C), lambda g: (0, 0, g, 0)),
        out_shape=jax.ShapeDtypeStruct((n1, 2, n2, C), F32),
        compiler_params=_cparams("arbitrary"),
        name="dft_filter_first",
    )(mh, ml, x)


def _first_stage_kernel(m_ref, x_ref, o_ref, *, gblk):
    for j in range(gblk):
        x = jnp.concatenate([x_ref[0, :, j, :], x_ref[1, :, j, :]], axis=0).astype(BF16)
        acc = jnp.dot(m_ref[j], x, preferred_element_type=F32)
        o_ref[:, j, :] = pltpu.bitcast(acc.astype(BF16), jnp.uint32)


def _first_stage(m, x, n1, n2):
    _, P, half, _, C = x.shape
    gblk = min(DFT_SLAB, n2)
    return pl.pallas_call(
        functools.partial(_first_stage_kernel, gblk=gblk),
        grid=(n2 // gblk, P),
        in_specs=[pl.BlockSpec((gblk,) + m.shape[1:], lambda g, p: (g, 0, 0)),
                  pl.BlockSpec((2, None, half, gblk, C), lambda g, p: (0, p, 0, g, 0))],
        out_specs=pl.BlockSpec((None, n1, gblk, C), lambda g, p: (p, 0, g, 0)),
        out_shape=jax.ShapeDtypeStruct((P, n1, n2, C), jnp.uint32),
        compiler_params=_cparams("arbitrary", "arbitrary"),
        name="dft_first",
    )(m, x)


def _packed_mid_kernel(fa_ref, fb_ref, kf_ref, d_ref, o_ref, *, gblk, nf):
    fa = fa_ref[...]
    fb = fb_ref[...]
    for j in range(gblk):
        xx = jnp.dot(fa, pltpu.bitcast(d_ref[j], BF16), preferred_element_type=F32)
        xr, xi = xx[:nf], xx[nf:]
        kr, ki = kf_ref[j, :nf], kf_ref[j, nf:]
        yy = jnp.concatenate([xr * kr - xi * ki, xr * ki + xi * kr], axis=0)
        zz = jnp.dot(fb, yy.astype(BF16), preferred_element_type=F32)
        o_ref[j] = pltpu.bitcast(zz.astype(BF16), jnp.uint32)


def _packed_mid(fa, fb, kf, d, gblk):
    P, G, K, C = d.shape
    gblk = min(gblk, G)
    return pl.pallas_call(
        functools.partial(_packed_mid_kernel, gblk=gblk, nf=K),
        grid=(G // gblk, P),
        in_specs=[pl.BlockSpec(fa.shape, lambda g, p: (0, 0)),
                  pl.BlockSpec(fb.shape, lambda g, p: (0, 0)),
                  pl.BlockSpec((gblk, 2 * K, C), lambda g, p: (g, 0, 0)),
                  pl.BlockSpec((None, gblk, K, C), lambda g, p: (p, g, 0, 0))],
        out_specs=pl.BlockSpec((None, gblk, K, C), lambda g, p: (p, g, 0, 0)),
        out_shape=jax.ShapeDtypeStruct((P, G, K, C), jnp.uint32),
        compiler_params=_cparams("arbitrary", "arbitrary"),
        name="dft_mid",
    )(fa, fb, kf, d)


def _last_stage_kernel(m_ref, z_ref, y_ref, *, gblk, half):
    for j in range(gblk):
        z = pltpu.bitcast(z_ref[:, j, :], BF16)
        y = jnp.dot(m_ref[j], z, preferred_element_type=F32)
        y_ref[0, :, j, :] = y[:half]
        y_ref[1, :, j, :] = y[half:]


def _last_stage(m, z, half):
    P, n1, n2, C = z.shape
    gblk = min(DFT_SLAB, n2)
    return pl.pallas_call(
        functools.partial(_last_stage_kernel, gblk=gblk, half=half),
        grid=(n2 // gblk, P),
        in_specs=[pl.BlockSpec((gblk,) + m.shape[1:], lambda g, p: (g, 0, 0)),
                  pl.BlockSpec((None, n1, gblk, C), lambda g, p: (p, 0, g, 0))],
        out_specs=pl.BlockSpec((2, None, half, gblk, C), lambda g, p: (0, p, 0, g, 0)),
        out_shape=jax.ShapeDtypeStruct((2, P, half, n2, C), F32),
        compiler_params=_cparams("arbitrary", "arbitrary"),
        name="dft_last",
    )(m, z)


def _long_conv_two_stage(vin, filt_rows, dc):
    B, L, C = vin.shape
    n1, n2 = dc["n1"], dc["n2"]
    half = n1 // 2
    P = B // 2
    ka = _filter_first_stage(dc["m1_filt"], filt_rows.reshape(n2, n1, C), n1, n2)
    kf = _bmm_left(dc["f2"], ka.reshape(1, n1, 2 * n2, C), 8, True, F32)[0]
    a = _first_stage(dc["m1_data"], vin.reshape(2, P, half, n2, C), n1, n2)
    z = _packed_mid(dc["f2_packed"], dc["f2_inv_packed"], kf, a, 8)
    y = _last_stage(dc["m1_inv"], z, half)
    return y.reshape(B, L, C)


def _long_conv_single(vin, filt, dc):
    B, L, C = vin.shape
    P = B // 2
    kf = _bmm_left(dc["filt"], filt.reshape(1, 1, 2 * L, C), 1, True, F32)[0]
    d = vin.reshape(2, P, L, C).transpose(1, 0, 2, 3).reshape(P, 1, 2 * L, C)
    y = _spectral_mid(dc["fwd"][0][0], dc["inv"][0][0], kf, d, 1, F32)
    return y.reshape(P, 2, L, C).transpose(1, 0, 2, 3).reshape(B, L, C)


def _outproj_kernel(*refs, final):
    if final:
        oa_ref, y_ref, vin_ref, m_ref, hb_ref, x_ref, gt_ref, w_ref, fw_ref, o_ref = refs
    else:
        oa_ref, y_ref, vin_ref, m_ref, hb_ref, x_ref, gt_ref, w_ref, o_ref = refs
    vin = vin_ref[...]
    ob = (y_ref[...] + vin * hb_ref[...]) * m_ref[...]
    r = (jnp.dot(oa_ref[...], w_ref[0:D_A, :], preferred_element_type=F32)
         + jnp.dot(ob.astype(BF16), w_ref[D_A:D_A + D_B, :], preferred_element_type=F32))
    xn = x_ref[...] + gt_ref[...] * r
    if final:
        ms = jnp.mean(xn * xn, axis=-1, keepdims=True)
        xn = xn * lax.rsqrt(ms + EPS) * fw_ref[...]
    o_ref[...] = xn


def _out_proj(oa, y, vin, m, hy_bias, x, gt, w_bf16, final_w):
    B, L, D = x.shape
    tm = min(L, 512)
    final = final_w is not None
    half = lambda: pl.BlockSpec((None, tm, D_B), lambda b, i: (b, i, 0))
    in_specs = [half(), half(), half(), half(),
                pl.BlockSpec((1, D_B), lambda b, i: (0, 0)),
                pl.BlockSpec((None, tm, D), lambda b, i: (b, i, 0)),
                pl.BlockSpec((None, 1, D), lambda b, i: (b, 0, 0)),
                pl.BlockSpec((D, D), lambda b, i: (0, 0))]
    args = [oa, y, vin, m, hy_bias.reshape(1, D_B), x, gt, w_bf16]
    if final:
        in_specs.append(pl.BlockSpec((1, D), lambda b, i: (0, 0)))
        args.append(final_w.reshape(1, D))
    return pl.pallas_call(
        functools.partial(_outproj_kernel, final=final),
        grid=(B, L // tm),
        in_specs=in_specs,
        out_specs=pl.BlockSpec((None, tm, D), lambda b, i: (b, i, 0)),
        out_shape=jax.ShapeDtypeStruct((B, L, D), F32),
        compiler_params=_cparams("arbitrary", "arbitrary"),
        name="out_proj",
    )(*args)


def _gate_params(lb):
    rows = []
    for d in range(2):
        rows += [jnp.log(lb[d]), jnp.log1p(-lb[d]), 1.0 - lb[d]]
    rows += [jnp.zeros_like(lb[0])] * (SUBLANES - len(rows))
    return jnp.stack(rows, axis=0)


def kernel(x, c, ctx, c_ctx, norm_w, w_ada, b_ada, w_in, w_out, lb_logits, g_norm_w,
           conv_w, conv_b, hy_w1, hy_b1, hy_freq, hy_w2, hy_b2, hy_w3, hy_b3, hy_w4,
           hy_bias, final_norm_w):
    B, L_lat, D = x.shape
    L_ctx = ctx.shape[1]
    p_lb = jax.nn.softmax(lb_logits.astype(F32), axis=0)
    lbs = jnp.cumsum(p_lb, axis=0)
    lbs = lbs - lbs[0:1]

    n_rows = 2 * SUBLANES
    cc = jnp.zeros((n_rows, D), F32).at[:B].set(c).at[B].set(c_ctx)
    mod = _modulation(cc, w_ada, b_ada)

    dc_lat = _dft_consts(L_lat)
    dc_ctx = _dft_single_consts(L_ctx)
    n1, n2 = dc_lat["n1"], dc_lat["n2"]
    order_lat = (np.arange(n2)[:, None] + n2 * np.arange(n1)[None, :]).reshape(-1)
    z_lat = _pos_rows(L_lat, order_lat)
    z_ctx = _pos_rows(L_ctx, np.arange(2 * L_ctx))
    zero_state = jnp.zeros((B, A_HEADS, HEAD_DIM, HEAD_DIM), F32)

    for l in range(DEPTH):
        last = l == DEPTH - 1
        sh_x, sc_x, gt_x = [mod[l, :B, j * D:(j + 1) * D].reshape(B, 1, D) for j in range(3)]
        sh_c, sc_c, gt_c = [jnp.broadcast_to(mod[l, B, j * D:(j + 1) * D].reshape(1, 1, D), (B, 1, D))
                            for j in range(3)]
        w_in_l = w_in[l].astype(BF16)
        w_out_l = w_out[l].astype(BF16)
        gate_par = _gate_params(lbs[l])
        conv_par = jnp.concatenate([conv_w[l], conv_b[l][None],
                                    jnp.zeros((SUBLANES - 4, 3 * D_B), F32)], axis=0)
        filt_args = (hy_w1[l], hy_b1[l], hy_freq[l], hy_w2[l], hy_b2[l], hy_w3[l], hy_b3[l], hy_w4[l])

        if last:
            pc, _ = _in_proj(ctx, norm_w[l], sc_c, sh_c, gate_par, w_in_l[:, :4 * D_A])
            _, s_f, s_b = _hgrn(pc, g_norm_w[l], zero_state, zero_state, with_out=False)
        else:
            pc, pc_hy = _in_proj(ctx, norm_w[l], sc_c, sh_c, gate_par, w_in_l)
            oa_c, s_f, s_b = _hgrn(pc, g_norm_w[l], zero_state, zero_state, with_out=True)
            vin_c, m_c = _hyena_pre(pc_hy, conv_par)
            filt_c = _hyena_filter(z_ctx, *filt_args)
            y_c = _long_conv_single(vin_c, filt_c, dc_ctx)
            ctx = _out_proj(oa_c, y_c, vin_c, m_c, hy_bias[l], ctx, gt_c, w_out_l, None)

        px, px_hy = _in_proj(x, norm_w[l], sc_x, sh_x, gate_par, w_in_l)
        oa, _, _ = _hgrn(px, g_norm_w[l], s_f, s_b, with_out=True)
        vin, m = _hyena_pre(px_hy, conv_par)
        filt_x = _hyena_filter(z_lat, *filt_args)
        y = _long_conv_two_stage(vin, filt_x, dc_lat)
        x = _out_proj(oa, y, vin, m, hy_bias[l], x, gt_x, w_out_l, final_norm_w if last else None)

    return x
```

```python
import functools
import math

import numpy as np
import jax
import jax.numpy as jnp
from jax import lax
from jax.experimental import pallas as pl
from jax.experimental.pallas import tpu as pltpu

F32 = jnp.float32
BF16 = jnp.bfloat16
HIGHEST = lax.Precision.HIGHEST

D_MODEL = 1024
DEPTH = 2
D_A = 512
D_B = 512
A_HEADS = 4
HEAD_DIM = 128
N_IN = 5 * D_A + 4 * D_B
HY_EMB = 33
HY_BANDS = 16
HY_WIDTH = 64
HY_MIN_DECAY = math.log(1e-2) / 1.5
HY_MAX_DECAY = math.log(1e-2) / 0.3
EPS = 1e-6

LANES = 128
SUBLANES = 8
SCAN_CHUNK = 64
SCAN_UNROLL = 8
SCAN_GUARD = 80.0
DFT_SLAB = 32
FFT_MINOR = 128
VMEM_LIMIT = 56 * 1024 * 1024


def _cparams(*sem):
    return pltpu.CompilerParams(dimension_semantics=sem, vmem_limit_bytes=VMEM_LIMIT)


def _mod_kernel(c_ref, w_ref, b_ref, o_ref):
    cc = c_ref[...]
    s = cc * jax.nn.sigmoid(cc)
    o_ref[...] = jnp.dot(s, w_ref[...], precision=HIGHEST,
                         preferred_element_type=F32) + b_ref[...]


def _modulation(cc, w_ada, b_ada):
    R, D = cc.shape
    N = w_ada.shape[-1]
    tn = 768
    return pl.pallas_call(
        _mod_kernel,
        grid=(DEPTH, N // tn),
        in_specs=[
            pl.BlockSpec((R, D), lambda l, j: (0, 0)),
            pl.BlockSpec((None, D, tn), lambda l, j: (l, 0, j)),
            pl.BlockSpec((None, 1, tn), lambda l, j: (l, 0, j)),
        ],
        out_specs=pl.BlockSpec((None, R, tn), lambda l, j: (l, 0, j)),
        out_shape=jax.ShapeDtypeStruct((DEPTH, R, N), F32),
        compiler_params=_cparams("arbitrary", "arbitrary"),
        name="modulation",
    )(cc, w_ada, b_ada.reshape(DEPTH, 1, N))


def _gates(f, par):
    loglb = par[0:1, :]
    log1mlb = par[1:2, :]
    onemlb = par[2:3, :]
    s1 = jnp.log(1.0 + jnp.exp(-jnp.abs(f)))
    b = log1mlb + (jnp.minimum(f, 0.0) - s1)
    ng = -jnp.maximum(loglb, b) - jnp.log(1.0 + jnp.exp(-jnp.abs(loglb - b)))
    k = onemlb * jnp.exp(-(jnp.maximum(f, 0.0) + s1))
    return ng, k


N_SCAN_GROUPS = 5


def _inproj_kernel(x_ref, nw_ref, sc_ref, sh_ref, gp_ref, w_ref, oa_ref, *rest):
    ob_ref = rest[0] if len(rest) == 2 else None
    hx_ref = rest[-1]
    x = x_ref[...]
    ms = jnp.mean(x * x, axis=-1, keepdims=True)
    y = x * lax.rsqrt(ms + EPS) * nw_ref[...]
    hx_ref[...] = (y * (1.0 + sc_ref[...]) + sh_ref[...]).astype(BF16)
    gp = gp_ref[...]
    w = D_A
    for g in range(w_ref.shape[-1] // w):
        r = jnp.dot(hx_ref[...], w_ref[:, g * w:(g + 1) * w], preferred_element_type=F32)
        if g == 0:
            oa_ref[:, 0:w] = r * jax.nn.sigmoid(r)
        elif g in (1, 2):
            ng, k = _gates(r, gp[3 * (g - 1):3 * g])
            oa_ref[:, (2 * g - 1) * w:2 * g * w] = ng
            oa_ref[:, 2 * g * w:(2 * g + 1) * w] = k
        elif g < N_SCAN_GROUPS:
            oa_ref[:, (g + 2) * w:(g + 3) * w] = r
        else:
            ob_ref[:, (g - N_SCAN_GROUPS) * w:(g - N_SCAN_GROUPS + 1) * w] = r.astype(ob_ref.dtype)


def _in_proj(x, norm_w, sc, sh, gate_par, w_bf16):
    B, L, D = x.shape
    N = w_bf16.shape[1]
    n_a = min(N // D_A, N_SCAN_GROUPS)
    Na = (n_a + 2) * D_A
    Nb = N - n_a * D_A
    tm = min(L, 512)
    out_specs = [pl.BlockSpec((None, tm, Na), lambda b, i: (b, i, 0))]
    out_shape = [jax.ShapeDtypeStruct((B, L, Na), F32)]
    if Nb:
        out_specs.append(pl.BlockSpec((None, tm, Nb), lambda b, i: (b, i, 0)))
        out_shape.append(jax.ShapeDtypeStruct((B, L, Nb), BF16))
    res = pl.pallas_call(
        _inproj_kernel,
        grid=(B, L // tm),
        in_specs=[
            pl.BlockSpec((None, tm, D), lambda b, i: (b, i, 0)),
            pl.BlockSpec((1, D), lambda b, i: (0, 0)),
            pl.BlockSpec((None, 1, D), lambda b, i: (b, 0, 0)),
            pl.BlockSpec((None, 1, D), lambda b, i: (b, 0, 0)),
            pl.BlockSpec((SUBLANES, D_A), lambda b, i: (0, 0)),
            pl.BlockSpec((D, N), lambda b, i: (0, 0)),
        ],
        out_specs=out_specs,
        out_shape=out_shape,
        scratch_shapes=[pltpu.VMEM((tm, D), BF16)],
        compiler_params=_cparams("arbitrary", "arbitrary"),
        name="in_proj",
    )(x, norm_w.reshape(1, D), sc, sh, gate_par, w_bf16)
    return (res[0], res[1]) if Nb else (res[0], None)


def _scan_consts(C, forward):
    idx = np.arange(C)
    i = idx[:, None]
    t = idx[None, :]
    if forward:
        mats = [t <= i, t > i]
    else:
        mats = [t >= i, t < i]
    masks = [i == t]
    h = 1
    while h < C:
        P = 2 * h
        p = i % P
        m = i - p + h
        upper = p >= h
        same = (i // P) == (t // P)
        if forward:
            mats.append(np.where(upper, (t >= m) & (t <= i), (t >= i + 1) & (t <= m - 1)))
            masks.append(same & upper & ((t % P) < h))
        else:
            mats.append(np.where(upper, (t >= m) & (t <= i - 1), (t >= i) & (t <= m - 1)))
            masks.append(same & (~upper) & ((t % P) >= h))
        h = P
    big = np.concatenate([m_.astype(np.float32) for m_ in mats], axis=0)
    big = np.concatenate([big, big, big], axis=1)
    msk = np.stack([m_.astype(np.float32) for m_ in masks], axis=0)
    return big, msk


def _dot_nt(a, b):
    return lax.dot_general(a, b, (((1,), (1,)), ((), ())), preferred_element_type=F32)


def _dot_tn(a, b):
    return lax.dot_general(a, b, (((0,), (0,)), ((), ())), preferred_element_type=F32)


def _split3(a):
    hi = a.astype(BF16)
    r1 = a - hi.astype(F32)
    mid = r1.astype(BF16)
    lo = (r1 - mid.astype(F32)).astype(BF16)
    return hi, mid, lo


def _cumulative(tri_ref, ng):
    return jnp.dot(tri_ref[...], jnp.concatenate(_split3(ng), axis=0), preferred_element_type=F32)


def _fast_stage_sums(q, ng, k, v, tri_ref):
    return dict(q=q, k=k, vb=v.astype(BF16), cum=_cumulative(tri_ref, ng))


def _fast_stage_scores(s, C, forward):
    h = C // 2
    q, k, cum = s["q"], s["k"], s["cum"]
    if forward:
        early, late = slice(0, h), slice(h, C)
        ref, tot = cum[h - 1:h, :], cum[C - 1:C, :]
    else:
        early, late = slice(h, C), slice(0, h)
        ref, tot = cum[h:h + 1, :], cum[0:1, :]
    qd = (q * jnp.exp(-cum)).astype(BF16)
    dl = cum - ref
    k_early = (k[early] * jnp.exp(cum[early])).astype(BF16)
    q_late = (q[late] * jnp.exp(-dl[late])).astype(BF16)
    k_all = (k * jnp.exp(dl)).astype(BF16)
    kd = (k * jnp.exp(cum - tot)).astype(BF16)
    return dict(vb=s["vb"], qd=qd, kd=kd, e_all=jnp.exp(-tot),
                guard=jnp.maximum(cum[early], dl[late]),
                s_early=_dot_nt(qd[early], k_early), s_late=_dot_nt(q_late, k_all))


def _fast_stage_intra(s, C, forward):
    h = C // 2
    early = slice(0, h) if forward else slice(h, C)
    ri = lax.broadcasted_iota(jnp.int32, (h, h), 0)
    ci = lax.broadcasted_iota(jnp.int32, (h, h), 1)
    rl = lax.broadcasted_iota(jnp.int32, (h, C), 0) + (h if forward else 0)
    cl = lax.broadcasted_iota(jnp.int32, (h, C), 1)
    keep_e = (ci <= ri) if forward else (ci >= ri)
    keep_l = (cl <= rl) if forward else (cl >= rl)
    s_early = jnp.where(keep_e, s["s_early"], 0.0).astype(BF16)
    s_late = jnp.where(keep_l, s["s_late"], 0.0).astype(BF16)
    o_early = jnp.dot(s_early, s["vb"][early], preferred_element_type=F32)
    o_late = jnp.dot(s_late, s["vb"], preferred_element_type=F32)
    o_intra = jnp.concatenate([o_early, o_late] if forward else [o_late, o_early], axis=0)
    return dict(qd=s["qd"], e_all=s["e_all"], o_intra=o_intra, upd=_dot_tn(s["vb"], s["kd"]))


def _fast_stage_state(s, st):
    o = s["o_intra"] + _dot_nt(s["qd"], st.astype(BF16))
    return o, st * s["e_all"] + s["upd"]


def _scan_chunk(q, ng, k, v, big_ref, msk_ref, st, C, forward):
    psum =jnp.dot(big_ref[...], jnp.concatenate(_split3(ng), axis=0), preferred_element_type=F32)
    ex = jnp.exp(-psum)
    e_in = ex[0:C]
    e_out = ex[C:2 * C]
    nlev = msk_ref.shape[0] - 1
    o = _dot_nt((q * e_in).astype(BF16), st.astype(BF16))
    scores = msk_ref[0] * _dot_nt(q.astype(BF16), k.astype(BF16))
    for l in range(nlev):
        el = ex[(2 + l) * C:(3 + l) * C]
        scores = scores + msk_ref[1 + l] * _dot_nt((q * el).astype(BF16), (k * el).astype(BF16))
    o = o + jnp.dot(scores.astype(BF16), v.astype(BF16), preferred_element_type=F32)
    e_all = e_in[C - 1:C, :] if forward else e_in[0:1, :]
    st_new = st * e_all + _dot_tn(v.astype(BF16), (k * e_out).astype(BF16))
    return o, st_new


def _hgrn_kernel(*refs, L, C, U, with_out):
    if with_out:
        (q_ref, ngf_ref, kf_ref, ngb_ref, kb_ref, iv_ref, ga_ref, gw_ref, bigf_ref, mskf_ref,
         bigb_ref, mskb_ref, s0f_ref, s0b_ref, o_ref, sf_ref, sb_ref, sf_in, sb_in, of_scr, ob_scr) = refs
    else:
        (q_ref, ngf_ref, kf_ref, ngb_ref, kb_ref, iv_ref, bigf_ref, mskf_ref, bigb_ref, mskb_ref,
         s0f_ref, s0b_ref, sf_ref, sb_ref, sf_in, sb_in) = refs
    n = L // C
    sf_ref[...] = s0f_ref[...]
    sb_ref[...] = s0b_ref[...]
    trif_ref = bigf_ref.at[0:C, :]
    trib_ref = bigb_ref.at[0:C, :]

    def rows_of(it, u):
        c = it * U + u
        return (pl.ds(pl.multiple_of(c * C, C), C), pl.ds(pl.multiple_of((n - 1 - c) * C, C), C))

    def put(rf, rb, o_f, o_b):
        if with_out:
            of_scr[rf, :] = o_f
            ob_scr[rb, :] = o_b

    def body(it, carry):
        sf_in[...] = sf_ref[...]
        sb_in[...] = sb_ref[...]
        rows = [rows_of(it, u) for u in range(U)]
        fwd = [_fast_stage_sums(q_ref[rf, :], ngf_ref[rf, :], kf_ref[rf, :], iv_ref[rf, :], trif_ref)
               for rf, _ in rows]
        bwd = [_fast_stage_sums(q_ref[rb, :], ngb_ref[rb, :], kb_ref[rb, :], iv_ref[rb, :], trib_ref)
               for _, rb in rows]
        fwd = [_fast_stage_scores(s, C, True) for s in fwd]
        bwd = [_fast_stage_scores(s, C, False) for s in bwd]
        guard = jnp.max(functools.reduce(jnp.maximum, [s["guard"] for s in fwd + bwd]))
        fwd = [_fast_stage_intra(s, C, True) for s in fwd]
        bwd = [_fast_stage_intra(s, C, False) for s in bwd]
        st_f, st_b = sf_ref[...], sb_ref[...]
        for u in range(U):
            o_f, st_f = _fast_stage_state(fwd[u], st_f)
            o_b, st_b = _fast_stage_state(bwd[u], st_b)
            put(rows[u][0], rows[u][1], o_f, o_b)
        sf_ref[...] = st_f
        sb_ref[...] = st_b

        @pl.when(jnp.logical_not(guard <= SCAN_GUARD))
        def _():
            sf_ref[...] = sf_in[...]
            sb_ref[...] = sb_in[...]

            def redo(u, carry2):
                rf, rb = rows_of(it, u)
                o_f, s_f = _scan_chunk(q_ref[rf, :], ngf_ref[rf, :], kf_ref[rf, :], iv_ref[rf, :],
                                       bigf_ref, mskf_ref, sf_ref[...], C, True)
                sf_ref[...] = s_f
                o_b, s_b = _scan_chunk(q_ref[rb, :], ngb_ref[rb, :], kb_ref[rb, :], iv_ref[rb, :],
                                       bigb_ref, mskb_ref, sb_ref[...], C, False)
                sb_ref[...] = s_b
                put(rf, rb, o_f, o_b)
                return carry2

            lax.fori_loop(0, U, redo, 0)

        return carry

    lax.fori_loop(0, n // U, body, 0)

    if with_out:
        gw = gw_ref[...]
        R = min(L, 512)

        def fin(r, carry):
            rows = pl.ds(pl.multiple_of(r * R, R), R)
            o = of_scr[rows, :] + ob_scr[rows, :]
            o = o * lax.rsqrt(jnp.mean(o * o, axis=-1, keepdims=True) + EPS) * gw
            ga = ga_ref[rows, :]
            o_ref[rows, :] = (o * (ga * jax.nn.sigmoid(ga))).astype(o_ref.dtype)
            return carry

        lax.fori_loop(0, L // R, fin, 0)


def _hgrn(px, g_norm_w, s0f, s0b, with_out):
    B, L, _ = px.shape
    C = SCAN_CHUNK
    H = A_HEADS
    bigf, mskf = _scan_consts(C, True)
    bigb, mskb = _scan_consts(C, False)
    bigf, bigb = jnp.asarray(bigf, BF16), jnp.asarray(bigb, BF16)
    mskf, mskb = jnp.asarray(mskf, F32), jnp.asarray(mskb, F32)

    def col(j):
        return pl.BlockSpec((None, L, LANES), lambda b, h, j=j: (b, 0, j * H + h))

    def const(a):
        return pl.BlockSpec(a.shape, lambda b, h, nd=a.ndim: (0,) * nd)

    st_spec = pl.BlockSpec((None, None, HEAD_DIM, HEAD_DIM), lambda b, h: (b, h, 0, 0))
    st_shape = jax.ShapeDtypeStruct((B, H, HEAD_DIM, HEAD_DIM), F32)
    in_specs = [col(j) for j in range(6)]
    args = [px] * 6
    if with_out:
        in_specs += [col(6), pl.BlockSpec((1, LANES), lambda b, h: (0, h))]
        args += [px, g_norm_w.reshape(1, D_A)]
    in_specs += [const(bigf), const(mskf), const(bigb), const(mskb), st_spec, st_spec]
    args += [bigf, mskf, bigb, mskb, s0f, s0b]
    out_specs = [st_spec, st_spec]
    out_shape = [st_shape, st_shape]
    scratch = [pltpu.VMEM((HEAD_DIM, HEAD_DIM), F32), pltpu.VMEM((HEAD_DIM, HEAD_DIM), F32)]
    if with_out:
        out_specs = [pl.BlockSpec((None, L, LANES), lambda b, h: (b, 0, h))] + out_specs
        out_shape = [jax.ShapeDtypeStruct((B, L, D_A), BF16)] + out_shape
        scratch += [pltpu.VMEM((L, LANES), F32), pltpu.VMEM((L, LANES), F32)]
    res = pl.pallas_call(
        functools.partial(_hgrn_kernel, L=L, C=C, U=min(SCAN_UNROLL, L // C), with_out=with_out),
        grid=(B, H),
        in_specs=in_specs,
        out_specs=out_specs,
        out_shape=out_shape,
        scratch_shapes=scratch,
        compiler_params=_cparams("arbitrary", "arbitrary"),
        name="hgrn2",
    )(*args)
    if with_out:
        return res[0], res[1], res[2]
    return None, res[0], res[1]


def _conv3(ref, r0, R, L, cp):
    T = 2 * SUBLANES
    cur = ref[pl.ds(r0, R), :].astype(F32)
    prev_tile = ref[pl.ds(pl.multiple_of(jnp.maximum(r0 - T, 0), T), T), :].astype(F32)
    next_tile = ref[pl.ds(pl.multiple_of(jnp.minimum(r0 + R, L - T), T), T), :].astype(F32)
    prow = jnp.where(r0 > 0, prev_tile[T - 1:T, :], 0.0)
    nrow = jnp.where(r0 + R < L, next_tile[0:1, :], 0.0)
    rid = lax.broadcasted_iota(jnp.int32, (R, LANES), 0)
    up = jnp.where(rid == 0, prow, pltpu.roll(cur, 1, 0))
    dn = jnp.where(rid == R - 1, nrow, pltpu.roll(cur, R - 1, 0))
    return cp[3:4, :] + up * cp[0:1, :] + cur * cp[1:2, :] + dn * cp[2:3, :]


def _hyena_pre_kernel(x0_ref, x1_ref, v_ref, gb_ref, c0_ref, c1_ref, c2_ref, vin_ref, m_ref, *, L):
    R = min(L, 256)
    c0 = c0_ref[...]
    c1 = c1_ref[...]
    c2 = c2_ref[...]

    def body(r, carry):
        r0 = pl.multiple_of(r * R, R)
        rows = pl.ds(r0, R)
        x0 = _conv3(x0_ref, r0, R, L, c0)
        x1 = _conv3(x1_ref, r0, R, L, c1)
        vv = _conv3(v_ref, r0, R, L, c2)
        gb = gb_ref[rows, :].astype(F32)
        vin_ref[rows, :] = vv * x1
        m_ref[rows, :] = (x0 * (gb * jax.nn.sigmoid(gb))).astype(m_ref.dtype)
        return carry

    lax.fori_loop(0, L // R, body, 0)


def _hyena_pre(px, conv_par):
    B, L, _ = px.shape
    nb = D_B // LANES

    def col(j):
        return pl.BlockSpec((None, L, LANES), lambda b, c, j=j: (b, 0, j * nb + c))

    def cpar(j):
        return pl.BlockSpec((SUBLANES, LANES), lambda b, c, j=j: (0, j * nb + c))

    out_spec = pl.BlockSpec((None, L, LANES), lambda b, c: (b, 0, c))
    out_shape = jax.ShapeDtypeStruct((B, L, D_B), F32)
    return pl.pallas_call(
        functools.partial(_hyena_pre_kernel, L=L),
        grid=(B, nb),
        in_specs=[col(0), col(1), col(2), col(3), cpar(0), cpar(1), cpar(2)],
        out_specs=[out_spec, out_spec],
        out_shape=[out_shape, jax.ShapeDtypeStruct((B, L, D_B), BF16)],
        compiler_params=_cparams("arbitrary", "arbitrary"),
        name="hyena_pre",
    )(px, px, px, px, conv_par, conv_par, conv_par)


def _pos_rows(L, order):
    f32 = np.float32
    t = np.linspace(0.0, 1.0, L, dtype=f32)[:, None]
    w = (f32(2.0 * math.pi) * np.arange(L, dtype=f32)[:, None]) / f32(L)
    f = np.linspace(1e-4, HY_BANDS - 1, HY_BANDS, dtype=f32)[None, :]
    z = np.concatenate([t, np.cos(f * w), -np.sin(f * w)], axis=-1).astype(f32)
    lag = np.arange(2 * L)
    src = np.where(lag < L, lag, 2 * L - lag) % L
    fwd = (lag < L).astype(f32)
    valid = (lag != L).astype(f32)
    rows = np.concatenate([z[src], fwd[:, None], valid[:, None],
                           np.zeros((2 * L, HY_WIDTH - HY_EMB - 2), f32)], axis=-1)
    return jnp.asarray(rows[order])


def _filter_kernel(z_ref, zt_ref, w1_ref, w2_ref, w3_ref, vec_ref, w4f_ref, w4b_ref, dl_ref, k_ref,
                   h_scr, *, n_rows):
    R = min(n_rows, 512)
    dl = dl_ref[...]

    @pl.when(pl.program_id(0) == 0)
    def _():
        vec = vec_ref[...]
        b1, b2, b3, fr = vec[:, 0:1], vec[:, 1:2], vec[:, 2:3], vec[:, 3:4]

        def mlp(r, carry):
            h = jnp.sin(fr * (jnp.dot(w1_ref[...], zt_ref[r], precision=HIGHEST,
                                      preferred_element_type=F32) + b1))
            h = jnp.sin(fr * (jnp.dot(w2_ref[...], h, precision=HIGHEST, preferred_element_type=F32) + b2))
            h_scr[r] = jnp.sin(fr * (jnp.dot(w3_ref[...], h, precision=HIGHEST,
                                             preferred_element_type=F32) + b3))
            return carry

        lax.fori_loop(0, n_rows // R, mlp, 0)

    def body(r, acc):
        rows = pl.ds(pl.multiple_of(r * R, R), R)
        z = z_ref[rows, :]
        h = h_scr[r].T
        hf = jnp.dot(h, w4f_ref[...], precision=HIGHEST, preferred_element_type=F32)
        hb = jnp.dot(h, w4b_ref[...], precision=HIGHEST, preferred_element_type=F32)
        fwd = z[:, HY_EMB:HY_EMB + 1]
        valid = z[:, HY_EMB + 1:HY_EMB + 2]
        k = jnp.where(fwd > 0.5, hf, hb) * jnp.exp(-z[:, 0:1] * dl) * valid
        k_ref[rows, :] = k
        return acc + jnp.sum(jnp.abs(k), axis=0, keepdims=True)

    tot = lax.fori_loop(0, n_rows // R, body, jnp.zeros((1, LANES), F32))

    def scale(r, carry):
        rows = pl.ds(pl.multiple_of(r * R, R), R)
        k_ref[rows, :] = k_ref[rows, :] / tot
        return carry

    lax.fori_loop(0, n_rows // R, scale, 0)


def _hyena_filter(zrows, w1, b1, freq, w2, b2, w3, b3, w4):
    n_rows = zrows.shape[0]
    R = min(n_rows, 512)
    zt = zrows.reshape(n_rows // R, R, HY_WIDTH).transpose(0, 2, 1)
    w1t = jnp.concatenate([w1, jnp.zeros((HY_WIDTH - HY_EMB, HY_WIDTH), F32)], axis=0).T
    vec = jnp.concatenate([b1[:, None], b2[:, None], b3[:, None], freq[:, None],
                           jnp.zeros((HY_WIDTH, SUBLANES - 4), F32)], axis=1)
    deltas = jnp.abs(jnp.linspace(HY_MIN_DECAY, HY_MAX_DECAY, D_B, dtype=F32))[None, :]
    nb = D_B // LANES

    def const(shape):
        return pl.BlockSpec(shape, lambda c, nd=len(shape): (0,) * nd)

    return pl.pallas_call(
        functools.partial(_filter_kernel, n_rows=n_rows),
        grid=(nb,),
        in_specs=[const((n_rows, HY_WIDTH)), const(zt.shape), const((HY_WIDTH, HY_WIDTH)),
                  const((HY_WIDTH, HY_WIDTH)), const((HY_WIDTH, HY_WIDTH)), const((HY_WIDTH, SUBLANES)),
                  pl.BlockSpec((HY_WIDTH, LANES), lambda c: (0, c)),
                  pl.BlockSpec((HY_WIDTH, LANES), lambda c: (0, nb + c)),
                  pl.BlockSpec((1, LANES), lambda c: (0, c))],
        out_specs=pl.BlockSpec((n_rows, LANES), lambda c: (0, c)),
        out_shape=jax.ShapeDtypeStruct((n_rows, D_B), F32),
        scratch_shapes=[pltpu.VMEM((n_rows // R, HY_WIDTH, R), F32)],
        compiler_params=_cparams("arbitrary"),
        name="hyena_filter",
    )(zrows, zt, w1t, w2.T, w3.T, vec, w4, w4, deltas)


def _cplx_block(gr, gi):
    return np.block([[gr, -gi], [gi, gr]])


def _interleave(n):
    return np.stack([np.arange(n), n + np.arange(n)], axis=1).reshape(-1)


def _hi_lo(a):
    a = jnp.asarray(a, F32)
    hi = a.astype(BF16)
    return hi, (a - hi.astype(F32)).astype(BF16)


def _dft_consts(L):
    N = 2 * L
    n2 = FFT_MINOR
    n1 = N // n2
    half = n1 // 2
    f1 = np.arange(n1)
    t2 = np.arange(n2)
    t1 = np.arange(n1)
    tt = n2 * t1[None, None, :] + t2[:, None, None]
    ang = -2.0 * np.pi * ((f1[None, :, None] * tt) % N) / N
    gr, gi = np.cos(ang), np.sin(ang)
    m1_data = np.stack([_cplx_block(gr[j][:, :half], gi[j][:, :half]) for j in range(n2)])
    m1_filt = np.concatenate([gr, gi], axis=1)
    m1_inv = np.stack([_cplx_block(gr[j][:, :half].T, -gi[j][:, :half].T) for j in range(n2)]) / N
    ang2 = -2.0 * np.pi * ((t2[:, None] * t2[None, :]) % n2) / n2
    f2 = _cplx_block(np.cos(ang2), np.sin(ang2))
    f2_inv = _cplx_block(np.cos(ang2), -np.sin(ang2))
    il1 = _interleave(n1)
    il2 = _interleave(n2)
    as_bf16 = lambda a: jnp.asarray(a, F32).astype(BF16)
    return dict(n1=n1, n2=n2, m1_filt=_hi_lo(m1_filt), f2=_hi_lo(f2[None]),
                m1_data=as_bf16(m1_data[:, il1, :]), f2_packed=as_bf16(f2[:, il2]),
                f2_inv_packed=as_bf16(f2_inv[il2, :]), m1_inv=as_bf16(m1_inv[:, :, il1]))


def _dft_single_consts(L):
    N = 2 * L
    f = np.arange(N)
    ang = -2.0 * np.pi * ((f[:, None] * f[None, :]) % N) / N
    gr, gi = np.cos(ang), np.sin(ang)
    fwd = _cplx_block(gr[:, :L], gi[:, :L])
    filt = np.concatenate([gr, gi], axis=0)
    inv = _cplx_block(gr[:L, :], -gi[:L, :]) / N
    return dict(fwd=_hi_lo(fwd[None]), filt=_hi_lo(filt[None]), inv=_hi_lo(inv[None]))


def _bmm_kernel(mh_ref, ml_ref, x_ref, o_ref, *, gblk, shared, precise):
    for j in range(gblk):
        jm = 0 if shared else j
        x = x_ref[j]
        xh = x.astype(BF16)
        acc = jnp.dot(mh_ref[jm], xh, preferred_element_type=F32)
        if precise:
            xl = (x - xh.astype(F32)).astype(BF16)
            acc = (acc + jnp.dot(mh_ref[jm], xl, preferred_element_type=F32)
                   + jnp.dot(ml_ref[jm], xh, preferred_element_type=F32))
        o_ref[j] = acc.astype(o_ref.dtype)


def _bmm_left(m, x, gblk, precise, out_dtype):
    mh, ml = m
    P, G, K, C = x.shape
    R = mh.shape[1]
    shared = mh.shape[0] == 1
    gblk = min(gblk, G)
    m_spec = (pl.BlockSpec((1, R, K), lambda g, p: (0, 0, 0)) if shared
              else pl.BlockSpec((gblk, R, K), lambda g, p: (g, 0, 0)))
    return pl.pallas_call(
        functools.partial(_bmm_kernel, gblk=gblk, shared=shared, precise=precise),
        grid=(G // gblk, P),
        in_specs=[m_spec, m_spec, pl.BlockSpec((None, gblk, K, C), lambda g, p: (p, g, 0, 0))],
        out_specs=pl.BlockSpec((None, gblk, R, C), lambda g, p: (p, g, 0, 0)),
        out_shape=jax.ShapeDtypeStruct((P, G, R, C), out_dtype),
        compiler_params=_cparams("arbitrary", "arbitrary"),
        name="dft_stage",
    )(mh, ml, x)


def _mid_kernel(fa_ref, fb_ref, kf_ref, d_ref, o_ref, *, gblk, nf):
    fa = fa_ref[...]
    fb = fb_ref[...]
    for j in range(gblk):
        xx = jnp.dot(fa, d_ref[j].astype(BF16), preferred_element_type=F32)
        xr, xi = xx[:nf], xx[nf:]
        kr, ki = kf_ref[j, :nf], kf_ref[j, nf:]
        yy = jnp.concatenate([xr * kr - xi * ki, xr * ki + xi * kr], axis=0)
        o_ref[j] = jnp.dot(fb, yy.astype(BF16), preferred_element_type=F32).astype(o_ref.dtype)


def _spectral_mid(fa, fb, kf, d, gblk, out_dtype):
    P, G, K, C = d.shape
    nf2 = fa.shape[0]
    Ko = fb.shape[0]
    gblk = min(gblk, G)
    return pl.pallas_call(
        functools.partial(_mid_kernel, gblk=gblk, nf=nf2 // 2),
        grid=(G // gblk, P),
        in_specs=[pl.BlockSpec(fa.shape, lambda g, p: (0, 0)),
                  pl.BlockSpec(fb.shape, lambda g, p: (0, 0)),
                  pl.BlockSpec((gblk, nf2, C), lambda g, p: (g, 0, 0)),
                  pl.BlockSpec((None, gblk, K, C), lambda g, p: (p, g, 0, 0))],
        out_specs=pl.BlockSpec((None, gblk, Ko, C), lambda g, p: (p, g, 0, 0)),
        out_shape=jax.ShapeDtypeStruct((P, G, Ko, C), out_dtype),
        compiler_params=_cparams("arbitrary", "arbitrary"),
        name="dft_mid",
    )(fa, fb, kf, d)


def _filter_first_kernel(mh_ref, ml_ref, x_ref, o_ref, *, gblk, n1):
    for j in range(gblk):
        x = x_ref[j]
        xh = x.astype(BF16)
        xl = (x - xh.astype(F32)).astype(BF16)
        acc = (jnp.dot(mh_ref[j], xh, preferred_element_type=F32)
               + jnp.dot(mh_ref[j], xl, preferred_element_type=F32)
               + jnp.dot(ml_ref[j], xh, preferred_element_type=F32))
        o_ref[:, 0, j, :] = acc[:n1]
        o_ref[:, 1, j, :] = acc[n1:]


def _filter_first_stage(m, x, n1, n2):
    mh, ml = m
    C = x.shape[-1]
    gblk = min(DFT_SLAB, n2)
    m_spec = pl.BlockSpec((gblk,) + mh.shape[1:], lambda g: (g, 0, 0))
    return pl.pallas_call(
        functools.partial(_filter_first_kernel, gblk=gblk, n1=n1),
        grid=(n2 // gblk,),
        in_specs=[m_spec, m_spec, pl.BlockSpec((gblk, n1, C), lambda g: (g, 0, 0))],
        out_specs=pl.BlockSpec((n1, 2, gblk, C), lambda g: (0, 0, g, 0)),
        out_shape=jax.ShapeDtypeStruct((n1, 2, n2, C), F32),
        compiler_params=_cparams("arbitrary"),
        name="dft_filter_first",
    )(mh, ml, x)


def _first_stage_kernel(m_ref, x_ref, o_ref, *, gblk):
    for j in range(gblk):
        x = jnp.concatenate([x_ref[0, :, j, :], x_ref[1, :, j, :]], axis=0).astype(BF16)
        acc = jnp.dot(m_ref[j], x, preferred_element_type=F32)
        o_ref[:, j, :] = pltpu.bitcast(acc.astype(BF16), jnp.uint32)


def _first_stage(m, x, n1, n2):
    _, P, half, _, C = x.shape
    gblk = min(DFT_SLAB, n2)
    return pl.pallas_call(
        functools.partial(_first_stage_kernel, gblk=gblk),
        grid=(n2 // gblk, P),
        in_specs=[pl.BlockSpec((gblk,) + m.shape[1:], lambda g, p: (g, 0, 0)),
                  pl.BlockSpec((2, None, half, gblk, C), lambda g, p: (0, p, 0, g, 0))],
        out_specs=pl.BlockSpec((None, n1, gblk, C), lambda g, p: (p, 0, g, 0)),
        out_shape=jax.ShapeDtypeStruct((P, n1, n2, C), jnp.uint32),
        compiler_params=_cparams("arbitrary", "arbitrary"),
        name="dft_first",
    )(m, x)


def _packed_mid_kernel(fa_ref, fb_ref, kf_ref, d_ref, o_ref, *, gblk, nf):
    fa = fa_ref[...]
    fb = fb_ref[...]
    for j in range(gblk):
        xx = jnp.dot(fa, pltpu.bitcast(d_ref[j], BF16), preferred_element_type=F32)
        xr, xi = xx[:nf], xx[nf:]
        kr, ki = kf_ref[j, :nf], kf_ref[j, nf:]
        yy = jnp.concatenate([xr * kr - xi * ki, xr * ki + xi * kr], axis=0)
        zz = jnp.dot(fb, yy.astype(BF16), preferred_element_type=F32)
        o_ref[j] = pltpu.bitcast(zz.astype(BF16), jnp.uint32)


def _packed_mid(fa, fb, kf, d, gblk):
    P, G, K, C = d.shape
    gblk = min(gblk, G)
    return pl.pallas_call(
        functools.partial(_packed_mid_kernel, gblk=gblk, nf=K),
        grid=(G // gblk, P),
        in_specs=[pl.BlockSpec(fa.shape, lambda g, p: (0, 0)),
                  pl.BlockSpec(fb.shape, lambda g, p: (0, 0)),
                  pl.BlockSpec((gblk, 2 * K, C), lambda g, p: (g, 0, 0)),
                  pl.BlockSpec((None, gblk, K, C), lambda g, p: (p, g, 0, 0))],
        out_specs=pl.BlockSpec((None, gblk, K, C), lambda g, p: (p, g, 0, 0)),
        out_shape=jax.ShapeDtypeStruct((P, G, K, C), jnp.uint32),
        compiler_params=_cparams("arbitrary", "arbitrary"),
        name="dft_mid",
    )(fa, fb, kf, d)


def _last_stage_kernel(m_ref, z_ref, y_ref, *, gblk, half):
    for j in range(gblk):
        z = pltpu.bitcast(z_ref[:, j, :], BF16)
        y = jnp.dot(m_ref[j], z, preferred_element_type=F32)
        y_ref[0, :, j, :] = y[:half]
        y_ref[1, :, j, :] = y[half:]


def _last_stage(m, z, half):
    P, n1, n2, C = z.shape
    gblk = min(DFT_SLAB, n2)
    return pl.pallas_call(
        functools.partial(_last_stage_kernel, gblk=gblk, half=half),
        grid=(n2 // gblk, P),
        in_specs=[pl.BlockSpec((gblk,) + m.shape[1:], lambda g, p: (g, 0, 0)),
                  pl.BlockSpec((None, n1, gblk, C), lambda g, p: (p, 0, g, 0))],
        out_specs=pl.BlockSpec((2, None, half, gblk, C), lambda g, p: (0, p, 0, g, 0)),
        out_shape=jax.ShapeDtypeStruct((2, P, half, n2, C), F32),
        compiler_params=_cparams("arbitrary", "arbitrary"),
        name="dft_last",
    )(m, z)


def _long_conv_two_stage(vin, filt_rows, dc):
    B, L, C = vin.shape
    n1, n2 = dc["n1"], dc["n2"]
    half = n1 // 2
    P = B // 2
    ka = _filter_first_stage(dc["m1_filt"], filt_rows.reshape(n2, n1, C), n1, n2)
    kf = _bmm_left(dc["f2"], ka.reshape(1, n1, 2 * n2, C), 8, True, F32)[0]
    a = _first_stage(dc["m1_data"], vin.reshape(2, P, half, n2, C), n1, n2)
    z = _packed_mid(dc["f2_packed"], dc["f2_inv_packed"], kf, a, 8)
    y = _last_stage(dc["m1_inv"], z, half)
    return y.reshape(B, L, C)


def _long_conv_single(vin, filt, dc):
    B, L, C = vin.shape
    P = B // 2
    kf = _bmm_left(dc["filt"], filt.reshape(1, 1, 2 * L, C), 1, True, F32)[0]
    d = vin.reshape(2, P, L, C).transpose(1, 0, 2, 3).reshape(P, 1, 2 * L, C)
    y = _spectral_mid(dc["fwd"][0][0], dc["inv"][0][0], kf, d, 1, F32)
    return y.reshape(P, 2, L, C).transpose(1, 0, 2, 3).reshape(B, L, C)


def _outproj_kernel(*refs, final):
    if final:
        oa_ref, y_ref, vin_ref, m_ref, hb_ref, x_ref, gt_ref, w_ref, fw_ref, o_ref = refs
    else:
        oa_ref, y_ref, vin_ref, m_ref, hb_ref, x_ref, gt_ref, w_ref, o_ref = refs
    vin = vin_ref[...]
    ob = (y_ref[...] + vin * hb_ref[...]) * m_ref[...]
    r = (jnp.dot(oa_ref[...], w_ref[0:D_A, :], preferred_element_type=F32)
         + jnp.dot(ob.astype(BF16), w_ref[D_A:D_A + D_B, :], preferred_element_type=F32))
    xn = x_ref[...] + gt_ref[...] * r
    if final:
        ms = jnp.mean(xn * xn, axis=-1, keepdims=True)
        xn = xn * lax.rsqrt(ms + EPS) * fw_ref[...]
    o_ref[...] = xn


def _out_proj(oa, y, vin, m, hy_bias, x, gt, w_bf16, final_w):
    B, L, D = x.shape
    tm = min(L, 512)
    final = final_w is not None
    half = lambda: pl.BlockSpec((None, tm, D_B), lambda b, i: (b, i, 0))
    in_specs = [half(), half(), half(), half(),
                pl.BlockSpec((1, D_B), lambda b, i: (0, 0)),
                pl.BlockSpec((None, tm, D), lambda b, i: (b, i, 0)),
                pl.BlockSpec((None, 1, D), lambda b, i: (b, 0, 0)),
                pl.BlockSpec((D, D), lambda b, i: (0, 0))]
    args = [oa, y, vin, m, hy_bias.reshape(1, D_B), x, gt, w_bf16]
    if final:
        in_specs.append(pl.BlockSpec((1, D), lambda b, i: (0, 0)))
        args.append(final_w.reshape(1, D))
    return pl.pallas_call(
        functools.partial(_outproj_kernel, final=final),
        grid=(B, L // tm),
        in_specs=in_specs,
        out_specs=pl.BlockSpec((None, tm, D), lambda b, i: (b, i, 0)),
        out_shape=jax.ShapeDtypeStruct((B, L, D), F32),
        compiler_params=_cparams("arbitrary", "arbitrary"),
        name="out_proj",
    )(*args)


def _gate_params(lb):
    rows = []
    for d in range(2):
        rows += [jnp.log(lb[d]), jnp.log1p(-lb[d]), 1.0 - lb[d]]
    rows += [jnp.zeros_like(lb[0])] * (SUBLANES - len(rows))
    return jnp.stack(rows, axis=0)


def kernel(x, c, ctx, c_ctx, norm_w, w_ada, b_ada, w_in, w_out, lb_logits, g_norm_w,
           conv_w, conv_b, hy_w1, hy_b1, hy_freq, hy_w2, hy_b2, hy_w3, hy_b3, hy_w4,
           hy_bias, final_norm_w):
    B, L_lat, D = x.shape
    L_ctx = ctx.shape[1]
    p_lb = jax.nn.softmax(lb_logits.astype(F32), axis=0)
    lbs = jnp.cumsum(p_lb, axis=0)
    lbs = lbs - lbs[0:1]

    n_rows = 2 * SUBLANES
    cc = jnp.zeros((n_rows, D), F32).at[:B].set(c).at[B].set(c_ctx)
    mod = _modulation(cc, w_ada, b_ada)

    dc_lat = _dft_consts(L_lat)
    dc_ctx = _dft_single_consts(L_ctx)
    n1, n2 = dc_lat["n1"], dc_lat["n2"]
    order_lat = (np.arange(n2)[:, None] + n2 * np.arange(n1)[None, :]).reshape(-1)
    z_lat = _pos_rows(L_lat, order_lat)
    z_ctx = _pos_rows(L_ctx, np.arange(2 * L_ctx))
    zero_state = jnp.zeros((B, A_HEADS, HEAD_DIM, HEAD_DIM), F32)

    for l in range(DEPTH):
        last = l == DEPTH - 1
        sh_x, sc_x, gt_x = [mod[l, :B, j * D:(j + 1) * D].reshape(B, 1, D) for j in range(3)]
        sh_c, sc_c, gt_c = [mod[l, B, j * D:(j + 1) * D].reshape(1, 1, D) for j in range(3)]
        w_in_l = w_in[l].astype(BF16)
        w_out_l = w_out[l].astype(BF16)
        gate_par = _gate_params(lbs[l])
        conv_par = jnp.concatenate([conv_w[l], conv_b[l][None],
                                    jnp.zeros((SUBLANES - 4, 3 * D_B), F32)], axis=0)
        filt_args = (hy_w1[l], hy_b1[l], hy_freq[l], hy_w2[l], hy_b2[l], hy_w3[l], hy_b3[l], hy_w4[l])

        flat = lambda a: a.reshape(1, B * L_ctx, a.shape[-1])
        unflat = lambda a: a.reshape(B, L_ctx, a.shape[-1])
        if last:
            pc, _ = _in_proj(flat(ctx), norm_w[l], sc_c, sh_c, gate_par, w_in_l[:, :4 * D_A])
            _, s_f, s_b = _hgrn(unflat(pc), g_norm_w[l], zero_state, zero_state, with_out=False)
        else:
            pc, pc_hy = _in_proj(flat(ctx), norm_w[l], sc_c, sh_c, gate_par, w_in_l)
            oa_c, s_f, s_b = _hgrn(unflat(pc), g_norm_w[l], zero_state, zero_state, with_out=True)
            vin_c, m_c = _hyena_pre(unflat(pc_hy), conv_par)
            filt_c = _hyena_filter(z_ctx, *filt_args)
            y_c = _long_conv_single(vin_c, filt_c, dc_ctx)
            ctx = unflat(_out_proj(flat(oa_c), flat(y_c), flat(vin_c), flat(m_c), hy_bias[l],
                                   flat(ctx), gt_c, w_out_l, None))

        px, px_hy = _in_proj(x, norm_w[l], sc_x, sh_x, gate_par, w_in_l)
        oa, _, _ = _hgrn(px, g_norm_w[l], s_f, s_b, with_out=True)
        vin, m = _hyena_pre(px_hy, conv_par)
        filt_x = _hyena_filter(z_lat, *filt_args)
        y = _long_conv_two_stage(vin, filt_x, dc_lat)
        x = _out_proj(oa, y, vin, m, hy_bias[l], x, gt_x, w_out_l, final_norm_w if last else None)

    return x
```

```python
import functools
import math

import numpy as np
import jax
import jax.numpy as jnp
from jax import lax
from jax.experimental import pallas as pl
from jax.experimental.pallas import tpu as pltpu

F32 = jnp.float32
BF16 = jnp.bfloat16
HIGHEST = lax.Precision.HIGHEST

D_MODEL = 1024
DEPTH = 2
D_A = 512
D_B = 512
A_HEADS = 4
HEAD_DIM = 128
N_IN = 5 * D_A + 4 * D_B
HY_EMB = 33
HY_BANDS = 16
HY_WIDTH = 64
HY_MIN_DECAY = math.log(1e-2) / 1.5
HY_MAX_DECAY = math.log(1e-2) / 0.3
EPS = 1e-6

LANES = 128
SUBLANES = 8
SCAN_CHUNK = 64
SCAN_UNROLL = 8
SCAN_GUARD = 80.0
DFT_SLAB = 32
FFT_MINOR = 128
VMEM_LIMIT = 56 * 1024 * 1024


def _cparams(*sem):
    return pltpu.CompilerParams(dimension_semantics=sem, vmem_limit_bytes=VMEM_LIMIT)


def _mod_kernel(c_ref, w_ref, b_ref, o_ref):
    cc = c_ref[...]
    s = cc * jax.nn.sigmoid(cc)
    o_ref[...] = jnp.dot(s, w_ref[...], precision=HIGHEST,
                         preferred_element_type=F32) + b_ref[...]


def _modulation(cc, w_ada, b_ada):
    R, D = cc.shape
    N = w_ada.shape[-1]
    tn = 768
    return pl.pallas_call(
        _mod_kernel,
        grid=(DEPTH, N // tn),
        in_specs=[
            pl.BlockSpec((R, D), lambda l, j: (0, 0)),
            pl.BlockSpec((None, D, tn), lambda l, j: (l, 0, j)),
            pl.BlockSpec((None, 1, tn), lambda l, j: (l, 0, j)),
        ],
        out_specs=pl.BlockSpec((None, R, tn), lambda l, j: (l, 0, j)),
        out_shape=jax.ShapeDtypeStruct((DEPTH, R, N), F32),
        compiler_params=_cparams("arbitrary", "arbitrary"),
        name="modulation",
    )(cc, w_ada, b_ada.reshape(DEPTH, 1, N))


def _gates(f, par):
    loglb = par[0:1, :]
    log1mlb = par[1:2, :]
    onemlb = par[2:3, :]
    s1 = jnp.log(1.0 + jnp.exp(-jnp.abs(f)))
    b = log1mlb + (jnp.minimum(f, 0.0) - s1)
    ng = -jnp.maximum(loglb, b) - jnp.log(1.0 + jnp.exp(-jnp.abs(loglb - b)))
    k = onemlb * jnp.exp(-(jnp.maximum(f, 0.0) + s1))
    return ng, k


N_SCAN_GROUPS = 5


def _inproj_kernel(x_ref, nw_ref, sc_ref, sh_ref, gp_ref, w_ref, oa_ref, *rest):
    ob_ref = rest[0] if len(rest) == 2 else None
    hx_ref = rest[-1]
    x = x_ref[...]
    ms = jnp.mean(x * x, axis=-1, keepdims=True)
    y = x * lax.rsqrt(ms + EPS) * nw_ref[...]
    hx_ref[...] = (y * (1.0 + sc_ref[...]) + sh_ref[...]).astype(BF16)
    gp = gp_ref[...]
    w = D_A
    for g in range(w_ref.shape[-1] // w):
        r = jnp.dot(hx_ref[...], w_ref[:, g * w:(g + 1) * w], preferred_element_type=F32)
        if g == 0:
            oa_ref[:, 0:w] = r * jax.nn.sigmoid(r)
        elif g in (1, 2):
            ng, k = _gates(r, gp[3 * (g - 1):3 * g])
            oa_ref[:, (2 * g - 1) * w:2 * g * w] = ng
            oa_ref[:, 2 * g * w:(2 * g + 1) * w] = k
        elif g < N_SCAN_GROUPS:
            oa_ref[:, (g + 2) * w:(g + 3) * w] = r
        else:
            ob_ref[:, (g - N_SCAN_GROUPS) * w:(g - N_SCAN_GROUPS + 1) * w] = r.astype(ob_ref.dtype)


def _in_proj(x, norm_w, sc, sh, gate_par, w_bf16):
    B, L, D = x.shape
    N = w_bf16.shape[1]
    n_a = min(N // D_A, N_SCAN_GROUPS)
    Na = (n_a + 2) * D_A
    Nb = N - n_a * D_A
    tm = min(L, 512)
    out_specs = [pl.BlockSpec((None, tm, Na), lambda b, i: (b, i, 0))]
    out_shape = [jax.ShapeDtypeStruct((B, L, Na), F32)]
    if Nb:
        out_specs.append(pl.BlockSpec((None, tm, Nb), lambda b, i: (b, i, 0)))
        out_shape.append(jax.ShapeDtypeStruct((B, L, Nb), BF16))
    res = pl.pallas_call(
        _inproj_kernel,
        grid=(B, L // tm),
        in_specs=[
            pl.BlockSpec((None, tm, D), lambda b, i: (b, i, 0)),
            pl.BlockSpec((1, D), lambda b, i: (0, 0)),
            pl.BlockSpec((None, 1, D), lambda b, i: (b, 0, 0)),
            pl.BlockSpec((None, 1, D), lambda b, i: (b, 0, 0)),
            pl.BlockSpec((SUBLANES, D_A), lambda b, i: (0, 0)),
            pl.BlockSpec((D, N), lambda b, i: (0, 0)),
        ],
        out_specs=out_specs,
        out_shape=out_shape,
        scratch_shapes=[pltpu.VMEM((tm, D), BF16)],
        compiler_params=_cparams("arbitrary", "arbitrary"),
        name="in_proj",
    )(x, norm_w.reshape(1, D), sc, sh, gate_par, w_bf16)
    return (res[0], res[1]) if Nb else (res[0], None)


def _scan_consts(C, forward):
    idx = np.arange(C)
    i = idx[:, None]
    t = idx[None, :]
    if forward:
        mats = [t <= i, t > i]
    else:
        mats = [t >= i, t < i]
    masks = [i == t]
    h = 1
    while h < C:
        P = 2 * h
        p = i % P
        m = i - p + h
        upper = p >= h
        same = (i // P) == (t // P)
        if forward:
            mats.append(np.where(upper, (t >= m) & (t <= i), (t >= i + 1) & (t <= m - 1)))
            masks.append(same & upper & ((t % P) < h))
        else:
            mats.append(np.where(upper, (t >= m) & (t <= i - 1), (t >= i) & (t <= m - 1)))
            masks.append(same & (~upper) & ((t % P) >= h))
        h = P
    big = np.concatenate([m_.astype(np.float32) for m_ in mats], axis=0)
    big = np.concatenate([big, big, big], axis=1)
    msk = np.stack([m_.astype(np.float32) for m_ in masks], axis=0)
    return big, msk


def _dot_nt(a, b):
    return lax.dot_general(a, b, (((1,), (1,)), ((), ())), preferred_element_type=F32)


def _dot_tn(a, b):
    return lax.dot_general(a, b, (((0,), (0,)), ((), ())), preferred_element_type=F32)


def _split3(a):
    hi = a.astype(BF16)
    r1 = a - hi.astype(F32)
    mid = r1.astype(BF16)
    lo = (r1 - mid.astype(F32)).astype(BF16)
    return hi, mid, lo


def _cumulative(tri_ref, ng):
    return jnp.dot(tri_ref[...], jnp.concatenate(_split3(ng), axis=0), preferred_element_type=F32)


def _fast_stage_sums(q, ng, k, v, tri_ref):
    return dict(q=q, k=k, vb=v.astype(BF16), cum=_cumulative(tri_ref, ng))


def _fast_stage_scores(s, C, forward):
    h = C // 2
    q, k, cum = s["q"], s["k"], s["cum"]
    if forward:
        early, late = slice(0, h), slice(h, C)
        ref, tot = cum[h - 1:h, :], cum[C - 1:C, :]
    else:
        early, late = slice(h, C), slice(0, h)
        ref, tot = cum[h:h + 1, :], cum[0:1, :]
    qd = (q * jnp.exp(-cum)).astype(BF16)
    dl = cum - ref
    k_early = (k[early] * jnp.exp(cum[early])).astype(BF16)
    q_late = (q[late] * jnp.exp(-dl[late])).astype(BF16)
    k_all = (k * jnp.exp(dl)).astype(BF16)
    kd = (k * jnp.exp(cum - tot)).astype(BF16)
    return dict(vb=s["vb"], qd=qd, kd=kd, e_all=jnp.exp(-tot),
                guard=jnp.maximum(cum[early], dl[late]),
                s_early=_dot_nt(qd[early], k_early), s_late=_dot_nt(q_late, k_all))


def _fast_stage_intra(s, C, forward):
    h = C // 2
    early = slice(0, h) if forward else slice(h, C)
    ri = lax.broadcasted_iota(jnp.int32, (h, h), 0)
    ci = lax.broadcasted_iota(jnp.int32, (h, h), 1)
    rl = lax.broadcasted_iota(jnp.int32, (h, C), 0) + (h if forward else 0)
    cl = lax.broadcasted_iota(jnp.int32, (h, C), 1)
    keep_e = (ci <= ri) if forward else (ci >= ri)
    keep_l = (cl <= rl) if forward else (cl >= rl)
    s_early = jnp.where(keep_e, s["s_early"], 0.0).astype(BF16)
    s_late = jnp.where(keep_l, s["s_late"], 0.0).astype(BF16)
    o_early = jnp.dot(s_early, s["vb"][early], preferred_element_type=F32)
    o_late = jnp.dot(s_late, s["vb"], preferred_element_type=F32)
    o_intra = jnp.concatenate([o_early, o_late] if forward else [o_late, o_early], axis=0)
    return dict(qd=s["qd"], e_all=s["e_all"], o_intra=o_intra, upd=_dot_tn(s["vb"], s["kd"]))


def _fast_stage_state(s, st):
    o = s["o_intra"] + _dot_nt(s["qd"], st.astype(BF16))
    return o, st * s["e_all"] + s["upd"]


def _scan_chunk(q, ng, k, v, big_ref, msk_ref, st, C, forward):
    psum =jnp.dot(big_ref[...], jnp.concatenate(_split3(ng), axis=0), preferred_element_type=F32)
    ex = jnp.exp(-psum)
    e_in = ex[0:C]
    e_out = ex[C:2 * C]
    nlev = msk_ref.shape[0] - 1
    o = _dot_nt((q * e_in).astype(BF16), st.astype(BF16))
    scores = msk_ref[0] * _dot_nt(q.astype(BF16), k.astype(BF16))
    for l in range(nlev):
        el = ex[(2 + l) * C:(3 + l) * C]
        scores = scores + msk_ref[1 + l] * _dot_nt((q * el).astype(BF16), (k * el).astype(BF16))
    o = o + jnp.dot(scores.astype(BF16), v.astype(BF16), preferred_element_type=F32)
    e_all = e_in[C - 1:C, :] if forward else e_in[0:1, :]
    st_new = st * e_all + _dot_tn(v.astype(BF16), (k * e_out).astype(BF16))
    return o, st_new


def _hgrn_kernel(*refs, L, C, U, with_out):
    if with_out:
        (q_ref, ngf_ref, kf_ref, ngb_ref, kb_ref, iv_ref, ga_ref, gw_ref, bigf_ref, mskf_ref,
         bigb_ref, mskb_ref, s0f_ref, s0b_ref, o_ref, sf_ref, sb_ref, sf_in, sb_in, of_scr, ob_scr) = refs
    else:
        (q_ref, ngf_ref, kf_ref, ngb_ref, kb_ref, iv_ref, bigf_ref, mskf_ref, bigb_ref, mskb_ref,
         s0f_ref, s0b_ref, sf_ref, sb_ref, sf_in, sb_in) = refs
    n = L // C
    sf_ref[...] = s0f_ref[...]
    sb_ref[...] = s0b_ref[...]
    trif_ref = bigf_ref.at[0:C, :]
    trib_ref = bigb_ref.at[0:C, :]

    def rows_of(it, u):
        c = it * U + u
        return (pl.ds(pl.multiple_of(c * C, C), C), pl.ds(pl.multiple_of((n - 1 - c) * C, C), C))

    def put(rf, rb, o_f, o_b):
        if with_out:
            of_scr[rf, :] = o_f
            ob_scr[rb, :] = o_b

    def body(it, carry):
        sf_in[...] = sf_ref[...]
        sb_in[...] = sb_ref[...]
        rows = [rows_of(it, u) for u in range(U)]
        fwd = [_fast_stage_sums(q_ref[rf, :], ngf_ref[rf, :], kf_ref[rf, :], iv_ref[rf, :], trif_ref)
               for rf, _ in rows]
        bwd = [_fast_stage_sums(q_ref[rb, :], ngb_ref[rb, :], kb_ref[rb, :], iv_ref[rb, :], trib_ref)
               for _, rb in rows]
        fwd = [_fast_stage_scores(s, C, True) for s in fwd]
        bwd = [_fast_stage_scores(s, C, False) for s in bwd]
        guard = jnp.max(functools.reduce(jnp.maximum, [s["guard"] for s in fwd + bwd]))
        fwd = [_fast_stage_intra(s, C, True) for s in fwd]
        bwd = [_fast_stage_intra(s, C, False) for s in bwd]
        st_f, st_b = sf_ref[...], sb_ref[...]
        for u in range(U):
            o_f, st_f = _fast_stage_state(fwd[u], st_f)
            o_b, st_b = _fast_stage_state(bwd[u], st_b)
            put(rows[u][0], rows[u][1], o_f, o_b)
        sf_ref[...] = st_f
        sb_ref[...] = st_b

        @pl.when(jnp.logical_not(guard <= SCAN_GUARD))
        def _():
            sf_ref[...] = sf_in[...]
            sb_ref[...] = sb_in[...]

            def redo(u, carry2):
                rf, rb = rows_of(it, u)
                o_f, s_f = _scan_chunk(q_ref[rf, :], ngf_ref[rf, :], kf_ref[rf, :], iv_ref[rf, :],
                                       bigf_ref, mskf_ref, sf_ref[...], C, True)
                sf_ref[...] = s_f
                o_b, s_b = _scan_chunk(q_ref[rb, :], ngb_ref[rb, :], kb_ref[rb, :], iv_ref[rb, :],
                                       bigb_ref, mskb_ref, sb_ref[...], C, False)
                sb_ref[...] = s_b
                put(rf, rb, o_f, o_b)
                return carry2

            lax.fori_loop(0, U, redo, 0)

        return carry

    lax.fori_loop(0, n // U, body, 0)

    if with_out:
        gw = gw_ref[...]
        R = min(L, 512)

        def fin(r, carry):
            rows = pl.ds(pl.multiple_of(r * R, R), R)
            o = of_scr[rows, :] + ob_scr[rows, :]
            o = o * lax.rsqrt(jnp.mean(o * o, axis=-1, keepdims=True) + EPS) * gw
            ga = ga_ref[rows, :]
            o_ref[rows, :] = (o * (ga * jax.nn.sigmoid(ga))).astype(o_ref.dtype)
            return carry

        lax.fori_loop(0, L // R, fin, 0)


def _hgrn(px, g_norm_w, s0f, s0b, with_out):
    B, L, _ = px.shape
    C = SCAN_CHUNK
    H = A_HEADS
    bigf, mskf = _scan_consts(C, True)
    bigb, mskb = _scan_consts(C, False)
    bigf, bigb = jnp.asarray(bigf, BF16), jnp.asarray(bigb, BF16)
    mskf, mskb = jnp.asarray(mskf, F32), jnp.asarray(mskb, F32)

    def col(j):
        return pl.BlockSpec((None, L, LANES), lambda b, h, j=j: (b, 0, j * H + h))

    def const(a):
        return pl.BlockSpec(a.shape, lambda b, h, nd=a.ndim: (0,) * nd)

    st_spec = pl.BlockSpec((None, None, HEAD_DIM, HEAD_DIM), lambda b, h: (b, h, 0, 0))
    st_shape = jax.ShapeDtypeStruct((B, H, HEAD_DIM, HEAD_DIM), F32)
    in_specs = [col(j) for j in range(6)]
    args = [px] * 6
    if with_out:
        in_specs += [col(6), pl.BlockSpec((1, LANES), lambda b, h: (0, h))]
        args += [px, g_norm_w.reshape(1, D_A)]
    in_specs += [const(bigf), const(mskf), const(bigb), const(mskb), st_spec, st_spec]
    args += [bigf, mskf, bigb, mskb, s0f, s0b]
    out_specs = [st_spec, st_spec]
    out_shape = [st_shape, st_shape]
    scratch = [pltpu.VMEM((HEAD_DIM, HEAD_DIM), F32), pltpu.VMEM((HEAD_DIM, HEAD_DIM), F32)]
    if with_out:
        out_specs = [pl.BlockSpec((None, L, LANES), lambda b, h: (b, 0, h))] + out_specs
        out_shape = [jax.ShapeDtypeStruct((B, L, D_A), BF16)] + out_shape
        scratch += [pltpu.VMEM((L, LANES), F32), pltpu.VMEM((L, LANES), F32)]
    res = pl.pallas_call(
        functools.partial(_hgrn_kernel, L=L, C=C, U=min(SCAN_UNROLL, L // C), with_out=with_out),
        grid=(B, H),
        in_specs=in_specs,
        out_specs=out_specs,
        out_shape=out_shape,
        scratch_shapes=scratch,
        compiler_params=_cparams("arbitrary", "arbitrary"),
        name="hgrn2",
    )(*args)
    if with_out:
        return res[0], res[1], res[2]
    return None, res[0], res[1]


def _conv3(ref, r0, R, L, cp):
    T = 2 * SUBLANES
    cur = ref[pl.ds(r0, R), :].astype(F32)
    prev_tile = ref[pl.ds(pl.multiple_of(jnp.maximum(r0 - T, 0), T), T), :].astype(F32)
    next_tile = ref[pl.ds(pl.multiple_of(jnp.minimum(r0 + R, L - T), T), T), :].astype(F32)
    prow = jnp.where(r0 > 0, prev_tile[T - 1:T, :], 0.0)
    nrow = jnp.where(r0 + R < L, next_tile[0:1, :], 0.0)
    rid = lax.broadcasted_iota(jnp.int32, (R, LANES), 0)
    up = jnp.where(rid == 0, prow, pltpu.roll(cur, 1, 0))
    dn = jnp.where(rid == R - 1, nrow, pltpu.roll(cur, R - 1, 0))
    return cp[3:4, :] + up * cp[0:1, :] + cur * cp[1:2, :] + dn * cp[2:3, :]


def _hyena_pre_kernel(x0_ref, x1_ref, v_ref, gb_ref, c0_ref, c1_ref, c2_ref, vin_ref, m_ref, *, L):
    R = min(L, 256)
    c0 = c0_ref[...]
    c1 = c1_ref[...]
    c2 = c2_ref[...]

    def body(r, carry):
        r0 = pl.multiple_of(r * R, R)
        rows = pl.ds(r0, R)
        x0 = _conv3(x0_ref, r0, R, L, c0)
        x1 = _conv3(x1_ref, r0, R, L, c1)
        vv = _conv3(v_ref, r0, R, L, c2)
        gb = gb_ref[rows, :].astype(F32)
        vin_ref[rows, :] = vv * x1
        m_ref[rows, :] = (x0 * (gb * jax.nn.sigmoid(gb))).astype(m_ref.dtype)
        return carry

    lax.fori_loop(0, L // R, body, 0)


def _hyena_pre(px, conv_par):
    B, L, _ = px.shape
    nb = D_B // LANES

    def col(j):
        return pl.BlockSpec((None, L, LANES), lambda b, c, j=j: (b, 0, j * nb + c))

    def cpar(j):
        return pl.BlockSpec((SUBLANES, LANES), lambda b, c, j=j: (0, j * nb + c))

    out_spec = pl.BlockSpec((None, L, LANES), lambda b, c: (b, 0, c))
    out_shape = jax.ShapeDtypeStruct((B, L, D_B), F32)
    return pl.pallas_call(
        functools.partial(_hyena_pre_kernel, L=L),
        grid=(B, nb),
        in_specs=[col(0), col(1), col(2), col(3), cpar(0), cpar(1), cpar(2)],
        out_specs=[out_spec, out_spec],
        out_shape=[out_shape, jax.ShapeDtypeStruct((B, L, D_B), BF16)],
        compiler_params=_cparams("arbitrary", "arbitrary"),
        name="hyena_pre",
    )(px, px, px, px, conv_par, conv_par, conv_par)


def _pos_rows(L, order):
    f32 = np.float32
    t = np.linspace(0.0, 1.0, L, dtype=f32)[:, None]
    w = (f32(2.0 * math.pi) * np.arange(L, dtype=f32)[:, None]) / f32(L)
    f = np.linspace(1e-4, HY_BANDS - 1, HY_BANDS, dtype=f32)[None, :]
    z = np.concatenate([t, np.cos(f * w), -np.sin(f * w)], axis=-1).astype(f32)
    lag = np.arange(2 * L)
    src = np.where(lag < L, lag, 2 * L - lag) % L
    fwd = (lag < L).astype(f32)
    valid = (lag != L).astype(f32)
    rows = np.concatenate([z[src], fwd[:, None], valid[:, None],
                           np.zeros((2 * L, HY_WIDTH - HY_EMB - 2), f32)], axis=-1)
    return jnp.asarray(rows[order])


def _filter_kernel(z_ref, zt_ref, w1_ref, w2_ref, w3_ref, vec_ref, w4f_ref, w4b_ref, dl_ref, k_ref,
                   h_scr, *, n_rows):
    R = min(n_rows, 512)
    dl = dl_ref[...]

    @pl.when(pl.program_id(0) == 0)
    def _():
        vec = vec_ref[...]
        b1, b2, b3, fr = vec[:, 0:1], vec[:, 1:2], vec[:, 2:3], vec[:, 3:4]

        def mlp(r, carry):
            h = jnp.sin(fr * (jnp.dot(w1_ref[...], zt_ref[r], precision=HIGHEST,
                                      preferred_element_type=F32) + b1))
            h = jnp.sin(fr * (jnp.dot(w2_ref[...], h, precision=HIGHEST, preferred_element_type=F32) + b2))
            h_scr[r] = jnp.sin(fr * (jnp.dot(w3_ref[...], h, precision=HIGHEST,
                                             preferred_element_type=F32) + b3))
            return carry

        lax.fori_loop(0, n_rows // R, mlp, 0)

    def body(r, acc):
        rows = pl.ds(pl.multiple_of(r * R, R), R)
        z = z_ref[rows, :]
        h = h_scr[r].T
        hf = jnp.dot(h, w4f_ref[...], precision=HIGHEST, preferred_element_type=F32)
        hb = jnp.dot(h, w4b_ref[...], precision=HIGHEST, preferred_element_type=F32)
        fwd = z[:, HY_EMB:HY_EMB + 1]
        valid = z[:, HY_EMB + 1:HY_EMB + 2]
        k = jnp.where(fwd > 0.5, hf, hb) * jnp.exp(-z[:, 0:1] * dl) * valid
        k_ref[rows, :] = k
        return acc + jnp.sum(jnp.abs(k), axis=0, keepdims=True)

    tot = lax.fori_loop(0, n_rows // R, body, jnp.zeros((1, LANES), F32))

    def scale(r, carry):
        rows = pl.ds(pl.multiple_of(r * R, R), R)
        k_ref[rows, :] = k_ref[rows, :] / tot
        return carry

    lax.fori_loop(0, n_rows // R, scale, 0)


def _hyena_filter(zrows, w1, b1, freq, w2, b2, w3, b3, w4):
    n_rows = zrows.shape[0]
    R = min(n_rows, 512)
    zt = zrows.reshape(n_rows // R, R, HY_WIDTH).transpose(0, 2, 1)
    w1t = jnp.concatenate([w1, jnp.zeros((HY_WIDTH - HY_EMB, HY_WIDTH), F32)], axis=0).T
    vec = jnp.concatenate([b1[:, None], b2[:, None], b3[:, None], freq[:, None],
                           jnp.zeros((HY_WIDTH, SUBLANES - 4), F32)], axis=1)
    deltas = jnp.abs(jnp.linspace(HY_MIN_DECAY, HY_MAX_DECAY, D_B, dtype=F32))[None, :]
    nb = D_B // LANES

    def const(shape):
        return pl.BlockSpec(shape, lambda c, nd=len(shape): (0,) * nd)

    return pl.pallas_call(
        functools.partial(_filter_kernel, n_rows=n_rows),
        grid=(nb,),
        in_specs=[const((n_rows, HY_WIDTH)), const(zt.shape), const((HY_WIDTH, HY_WIDTH)),
                  const((HY_WIDTH, HY_WIDTH)), const((HY_WIDTH, HY_WIDTH)), const((HY_WIDTH, SUBLANES)),
                  pl.BlockSpec((HY_WIDTH, LANES), lambda c: (0, c)),
                  pl.BlockSpec((HY_WIDTH, LANES), lambda c: (0, nb + c)),
                  pl.BlockSpec((1, LANES), lambda c: (0, c))],
        out_specs=pl.BlockSpec((n_rows, LANES), lambda c: (0, c)),
        out_shape=jax.ShapeDtypeStruct((n_rows, D_B), F32),
        scratch_shapes=[pltpu.VMEM((n_rows // R, HY_WIDTH, R), F32)],
        compiler_params=_cparams("arbitrary"),
        name="hyena_filter",
    )(zrows, zt, w1t, w2.T, w3.T, vec, w4, w4, deltas)


def _cplx_block(gr, gi):
    return np.block([[gr, -gi], [gi, gr]])


def _interleave(n):
    return np.stack([np.arange(n), n + np.arange(n)], axis=1).reshape(-1)


def _hi_lo(a):
    a = jnp.asarray(a, F32)
    hi = a.astype(BF16)
    return hi, (a - hi.astype(F32)).astype(BF16)


def _dft_consts(L):
    N = 2 * L
    n2 = FFT_MINOR
    n1 = N // n2
    half = n1 // 2
    f1 = np.arange(n1)
    t2 = np.arange(n2)
    t1 = np.arange(n1)
    tt = n2 * t1[None, None, :] + t2[:, None, None]
    ang = -2.0 * np.pi * ((f1[None, :, None] * tt) % N) / N
    gr, gi = np.cos(ang), np.sin(ang)
    m1_data = np.stack([_cplx_block(gr[j][:, :half], gi[j][:, :half]) for j in range(n2)])
    m1_filt = np.concatenate([gr, gi], axis=1)
    m1_inv = np.stack([_cplx_block(gr[j][:, :half].T, -gi[j][:, :half].T) for j in range(n2)]) / N
    ang2 = -2.0 * np.pi * ((t2[:, None] * t2[None, :]) % n2) / n2
    f2 = _cplx_block(np.cos(ang2), np.sin(ang2))
    f2_inv = _cplx_block(np.cos(ang2), -np.sin(ang2))
    il1 = _interleave(n1)
    il2 = _interleave(n2)
    as_bf16 = lambda a: jnp.asarray(a, F32).astype(BF16)
    return dict(n1=n1, n2=n2, m1_filt=_hi_lo(m1_filt), f2=_hi_lo(f2[None]),
                m1_data=as_bf16(m1_data[:, il1, :]), f2_packed=as_bf16(f2[:, il2]),
                f2_inv_packed=as_bf16(f2_inv[il2, :]), m1_inv=as_bf16(m1_inv[:, :, il1]))


def _dft_single_consts(L):
    N = 2 * L
    f = np.arange(N)
    ang = -2.0 * np.pi * ((f[:, None] * f[None, :]) % N) / N
    gr, gi = np.cos(ang), np.sin(ang)
    fwd = _cplx_block(gr[:, :L], gi[:, :L])
    filt = np.concatenate([gr, gi], axis=0)
    inv = _cplx_block(gr[:L, :], -gi[:L, :]) / N
    return dict(fwd=_hi_lo(fwd[None]), filt=_hi_lo(filt[None]), inv=_hi_lo(inv[None]))


def _bmm_kernel(mh_ref, ml_ref, x_ref, o_ref, *, gblk, shared, precise):
    for j in range(gblk):
        jm = 0 if shared else j
        x = x_ref[j]
        xh = x.astype(BF16)
        acc = jnp.dot(mh_ref[jm], xh, preferred_element_type=F32)
        if precise:
            xl = (x - xh.astype(F32)).astype(BF16)
            acc = (acc + jnp.dot(mh_ref[jm], xl, preferred_element_type=F32)
                   + jnp.dot(ml_ref[jm], xh, preferred_element_type=F32))
        o_ref[j] = acc.astype(o_ref.dtype)


def _bmm_left(m, x, gblk, precise, out_dtype):
    mh, ml = m
    P, G, K, C = x.shape
    R = mh.shape[1]
    shared = mh.shape[0] == 1
    gblk = min(gblk, G)
    m_spec = (pl.BlockSpec((1, R, K), lambda g, p: (0, 0, 0)) if shared
              else pl.BlockSpec((gblk, R, K), lambda g, p: (g, 0, 0)))
    return pl.pallas_call(
        functools.partial(_bmm_kernel, gblk=gblk, shared=shared, precise=precise),
        grid=(G // gblk, P),
        in_specs=[m_spec, m_spec, pl.BlockSpec((None, gblk, K, C), lambda g, p: (p, g, 0, 0))],
        out_specs=pl.BlockSpec((None, gblk, R, C), lambda g, p: (p, g, 0, 0)),
        out_shape=jax.ShapeDtypeStruct((P, G, R, C), out_dtype),
        compiler_params=_cparams("arbitrary", "arbitrary"),
        name="dft_stage",
    )(mh, ml, x)


def _mid_kernel(fa_ref, fb_ref, kf_ref, hb_ref, d_ref, o_ref, *, gblk, nf):
    fa = fa_ref[...]
    fb = fb_ref[...]
    hb = hb_ref[...]
    for j in range(gblk):
        xx = jnp.dot(fa, d_ref[j].astype(BF16), preferred_element_type=F32)
        xr, xi = xx[:nf], xx[nf:]
        kr, ki = kf_ref[j, :nf] + hb, kf_ref[j, nf:]
        yy = jnp.concatenate([xr * kr - xi * ki, xr * ki + xi * kr], axis=0)
        o_ref[j] = jnp.dot(fb, yy.astype(BF16), preferred_element_type=F32).astype(o_ref.dtype)


def _spectral_mid(fa, fb, kf, hb, d, gblk, out_dtype):
    P, G, K, C = d.shape
    nf2 = fa.shape[0]
    Ko = fb.shape[0]
    gblk = min(gblk, G)
    return pl.pallas_call(
        functools.partial(_mid_kernel, gblk=gblk, nf=nf2 // 2),
        grid=(G // gblk, P),
        in_specs=[pl.BlockSpec(fa.shape, lambda g, p: (0, 0)),
                  pl.BlockSpec(fb.shape, lambda g, p: (0, 0)),
                  pl.BlockSpec((gblk, nf2, C), lambda g, p: (g, 0, 0)),
                  pl.BlockSpec((1, C), lambda g, p: (0, 0)),
                  pl.BlockSpec((None, gblk, K, C), lambda g, p: (p, g, 0, 0))],
        out_specs=pl.BlockSpec((None, gblk, Ko, C), lambda g, p: (p, g, 0, 0)),
        out_shape=jax.ShapeDtypeStruct((P, G, Ko, C), out_dtype),
        compiler_params=_cparams("arbitrary", "arbitrary"),
        name="dft_mid",
    )(fa, fb, kf, hb, d)


def _filter_first_kernel(mh_ref, ml_ref, x_ref, o_ref, *, gblk, n1):
    for j in range(gblk):
        x = x_ref[j]
        xh = x.astype(BF16)
        xl = (x - xh.astype(F32)).astype(BF16)
        acc = (jnp.dot(mh_ref[j], xh, preferred_element_type=F32)
               + jnp.dot(mh_ref[j], xl, preferred_element_type=F32)
               + jnp.dot(ml_ref[j], xh, preferred_element_type=F32))
        o_ref[:, 0, j, :] = acc[:n1]
        o_ref[:, 1, j, :] = acc[n1:]


def _filter_first_stage(m, x, n1, n2):
    mh, ml = m
    C = x.shape[-1]
    gblk = min(DFT_SLAB, n2)
    m_spec = pl.BlockSpec((gblk,) + mh.shape[1:], lambda g: (g, 0, 0))
    return pl.pallas_call(
        functools.partial(_filter_first_kernel, gblk=gblk, n1=n1),
        grid=(n2 // gblk,),
        in_specs=[m_spec, m_spec, pl.BlockSpec((gblk, n1, C), lambda g: (g, 0, 0))],
        out_specs=pl.BlockSpec((n1, 2, gblk, C), lambda g: (0, 0, g, 0)),
        out_shape=jax.ShapeDtypeStruct((n1, 2, n2, C), F32),
        compiler_params=_cparams("arbitrary"),
        name="dft_filter_first",
    )(mh, ml, x)


def _first_stage_kernel(m_ref, x_ref, o_ref, *, gblk):
    for j in range(gblk):
        x = jnp.concatenate([x_ref[0, :, j, :], x_ref[1, :, j, :]], axis=0).astype(BF16)
        acc = jnp.dot(m_ref[j], x, preferred_element_type=F32)
        o_ref[:, j, :] = pltpu.bitcast(acc.astype(BF16), jnp.uint32)


def _first_stage(m, x, n1, n2):
    _, P, half, _, C = x.shape
    gblk = min(DFT_SLAB, n2)
    return pl.pallas_call(
        functools.partial(_first_stage_kernel, gblk=gblk),
        grid=(n2 // gblk, P),
        in_specs=[pl.BlockSpec((gblk,) + m.shape[1:], lambda g, p: (g, 0, 0)),
                  pl.BlockSpec((2, None, half, gblk, C), lambda g, p: (0, p, 0, g, 0))],
        out_specs=pl.BlockSpec((None, n1, gblk, C), lambda g, p: (p, 0, g, 0)),
        out_shape=jax.ShapeDtypeStruct((P, n1, n2, C), jnp.uint32),
        compiler_params=_cparams("arbitrary", "arbitrary"),
        name="dft_first",
    )(m, x)


def _packed_mid_kernel(fa_ref, fb_ref, kf_ref, hb_ref, d_ref, o_ref, *, gblk, nf):
    fa = fa_ref[...]
    fb = fb_ref[...]
    hb = hb_ref[...]
    for j in range(gblk):
        xx = jnp.dot(fa, pltpu.bitcast(d_ref[j], BF16), preferred_element_type=F32)
        xr, xi = xx[:nf], xx[nf:]
        kr, ki = kf_ref[j, :nf] + hb, kf_ref[j, nf:]
        yy = jnp.concatenate([xr * kr - xi * ki, xr * ki + xi * kr], axis=0)
        zz = jnp.dot(fb, yy.astype(BF16), preferred_element_type=F32)
        o_ref[j] = pltpu.bitcast(zz.astype(BF16), jnp.uint32)


def _packed_mid(fa, fb, kf, hb, d, gblk):
    P, G, K, C = d.shape
    gblk = min(gblk, G)
    return pl.pallas_call(
        functools.partial(_packed_mid_kernel, gblk=gblk, nf=K),
        grid=(G // gblk, P),
        in_specs=[pl.BlockSpec(fa.shape, lambda g, p: (0, 0)),
                  pl.BlockSpec(fb.shape, lambda g, p: (0, 0)),
                  pl.BlockSpec((gblk, 2 * K, C), lambda g, p: (g, 0, 0)),
                  pl.BlockSpec((1, C), lambda g, p: (0, 0)),
                  pl.BlockSpec((None, gblk, K, C), lambda g, p: (p, g, 0, 0))],
        out_specs=pl.BlockSpec((None, gblk, K, C), lambda g, p: (p, g, 0, 0)),
        out_shape=jax.ShapeDtypeStruct((P, G, K, C), jnp.uint32),
        compiler_params=_cparams("arbitrary", "arbitrary"),
        name="dft_mid",
    )(fa, fb, kf, hb, d)


def _last_stage_kernel(m_ref, z_ref, y_ref, *, gblk, half):
    for j in range(gblk):
        z = pltpu.bitcast(z_ref[:, j, :], BF16)
        y = jnp.dot(m_ref[j], z, preferred_element_type=F32)
        y_ref[0, :, j, :] = y[:half]
        y_ref[1, :, j, :] = y[half:]


def _last_stage(m, z, half):
    P, n1, n2, C = z.shape
    gblk = min(DFT_SLAB, n2)
    return pl.pallas_call(
        functools.partial(_last_stage_kernel, gblk=gblk, half=half),
        grid=(n2 // gblk, P),
        in_specs=[pl.BlockSpec((gblk,) + m.shape[1:], lambda g, p: (g, 0, 0)),
                  pl.BlockSpec((None, n1, gblk, C), lambda g, p: (p, 0, g, 0))],
        out_specs=pl.BlockSpec((2, None, half, gblk, C), lambda g, p: (0, p, 0, g, 0)),
        out_shape=jax.ShapeDtypeStruct((2, P, half, n2, C), F32),
        compiler_params=_cparams("arbitrary", "arbitrary"),
        name="dft_last",
    )(m, z)


def _long_conv_two_stage(vin, filt_rows, hy_bias, dc):
    B, L, C = vin.shape
    n1, n2 = dc["n1"], dc["n2"]
    half = n1 // 2
    P = B // 2
    ka = _filter_first_stage(dc["m1_filt"], filt_rows.reshape(n2, n1, C), n1, n2)
    kf = _bmm_left(dc["f2"], ka.reshape(1, n1, 2 * n2, C), 8, True, F32)[0]
    a = _first_stage(dc["m1_data"], vin.reshape(2, P, half, n2, C), n1, n2)
    z = _packed_mid(dc["f2_packed"], dc["f2_inv_packed"], kf, hy_bias, a, 8)
    y = _last_stage(dc["m1_inv"], z, half)
    return y.reshape(B, L, C)


def _long_conv_single(vin, filt, hy_bias, dc):
    B, L, C = vin.shape
    P = B // 2
    kf = _bmm_left(dc["filt"], filt.reshape(1, 1, 2 * L, C), 1, True, F32)[0]
    d = vin.reshape(2, P, L, C).transpose(1, 0, 2, 3).reshape(P, 1, 2 * L, C)
    y = _spectral_mid(dc["fwd"][0][0], dc["inv"][0][0], kf, hy_bias, d, 1, F32)
    return y.reshape(P, 2, L, C).transpose(1, 0, 2, 3).reshape(B, L, C)


def _outproj_kernel(*refs, final):
    if final:
        oa_ref, y_ref, m_ref, x_ref, gt_ref, w_ref, fw_ref, o_ref = refs
    else:
        oa_ref, y_ref, m_ref, x_ref, gt_ref, w_ref, o_ref = refs
    ob = y_ref[...] * m_ref[...]
    r = (jnp.dot(oa_ref[...], w_ref[0:D_A, :], preferred_element_type=F32)
         + jnp.dot(ob.astype(BF16), w_ref[D_A:D_A + D_B, :], preferred_element_type=F32))
    xn = x_ref[...] + gt_ref[...] * r
    if final:
        ms = jnp.mean(xn * xn, axis=-1, keepdims=True)
        xn = xn * lax.rsqrt(ms + EPS) * fw_ref[...]
    o_ref[...] = xn


def _out_proj(oa, y, m, x, gt, w_bf16, final_w):
    B, L, D = x.shape
    tm = min(L, 512)
    final = final_w is not None
    half = lambda: pl.BlockSpec((None, tm, D_B), lambda b, i: (b, i, 0))
    in_specs = [half(), half(), half(),
                pl.BlockSpec((None, tm, D), lambda b, i: (b, i, 0)),
                pl.BlockSpec((None, 1, D), lambda b, i: (b, 0, 0)),
                pl.BlockSpec((D, D), lambda b, i: (0, 0))]
    args = [oa, y, m, x, gt, w_bf16]
    if final:
        in_specs.append(pl.BlockSpec((1, D), lambda b, i: (0, 0)))
        args.append(final_w.reshape(1, D))
    return pl.pallas_call(
        functools.partial(_outproj_kernel, final=final),
        grid=(B, L // tm),
        in_specs=in_specs,
        out_specs=pl.BlockSpec((None, tm, D), lambda b, i: (b, i, 0)),
        out_shape=jax.ShapeDtypeStruct((B, L, D), F32),
        compiler_params=_cparams("arbitrary", "arbitrary"),
        name="out_proj",
    )(*args)


def _gate_params(lb):
    rows = []
    for d in range(2):
        rows += [jnp.log(lb[d]), jnp.log1p(-lb[d]), 1.0 - lb[d]]
    rows += [jnp.zeros_like(lb[0])] * (SUBLANES - len(rows))
    return jnp.stack(rows, axis=0)


def kernel(x, c, ctx, c_ctx, norm_w, w_ada, b_ada, w_in, w_out, lb_logits, g_norm_w,
           conv_w, conv_b, hy_w1, hy_b1, hy_freq, hy_w2, hy_b2, hy_w3, hy_b3, hy_w4,
           hy_bias, final_norm_w):
    B, L_lat, D = x.shape
    L_ctx = ctx.shape[1]
    p_lb = jax.nn.softmax(lb_logits.astype(F32), axis=0)
    lbs = jnp.cumsum(p_lb, axis=0)
    lbs = lbs - lbs[0:1]

    n_rows = 2 * SUBLANES
    cc = jnp.zeros((n_rows, D), F32).at[:B].set(c).at[B].set(c_ctx)
    mod = _modulation(cc, w_ada, b_ada)

    dc_lat = _dft_consts(L_lat)
    dc_ctx = _dft_single_consts(L_ctx)
    n1, n2 = dc_lat["n1"], dc_lat["n2"]
    order_lat = (np.arange(n2)[:, None] + n2 * np.arange(n1)[None, :]).reshape(-1)
    z_lat = _pos_rows(L_lat, order_lat)
    z_ctx = _pos_rows(L_ctx, np.arange(2 * L_ctx))
    zero_state = jnp.zeros((B, A_HEADS, HEAD_DIM, HEAD_DIM), F32)

    for l in range(DEPTH):
        last = l == DEPTH - 1
        sh_x, sc_x, gt_x = [mod[l, :B, j * D:(j + 1) * D].reshape(B, 1, D) for j in range(3)]
        sh_c, sc_c, gt_c = [mod[l, B, j * D:(j + 1) * D].reshape(1, 1, D) for j in range(3)]
        w_in_l = w_in[l].astype(BF16)
        w_out_l = w_out[l].astype(BF16)
        gate_par = _gate_params(lbs[l])
        conv_par = jnp.concatenate([conv_w[l], conv_b[l][None],
                                    jnp.zeros((SUBLANES - 4, 3 * D_B), F32)], axis=0)
        filt_args = (hy_w1[l], hy_b1[l], hy_freq[l], hy_w2[l], hy_b2[l], hy_w3[l], hy_b3[l], hy_w4[l])

        flat = lambda a: a.reshape(1, B * L_ctx, a.shape[-1])
        unflat = lambda a: a.reshape(B, L_ctx, a.shape[-1])
        if last:
            pc, _ = _in_proj(flat(ctx), norm_w[l], sc_c, sh_c, gate_par, w_in_l[:, :4 * D_A])
            _, s_f, s_b = _hgrn(unflat(pc), g_norm_w[l], zero_state, zero_state, with_out=False)
        else:
            pc, pc_hy = _in_proj(flat(ctx), norm_w[l], sc_c, sh_c, gate_par, w_in_l)
            oa_c, s_f, s_b = _hgrn(unflat(pc), g_norm_w[l], zero_state, zero_state, with_out=True)
            vin_c, m_c = _hyena_pre(unflat(pc_hy), conv_par)
            filt_c = _hyena_filter(z_ctx, *filt_args)
            y_c = _long_conv_single(vin_c, filt_c, hy_bias[l].reshape(1, D_B), dc_ctx)
            ctx = unflat(_out_proj(flat(oa_c), flat(y_c), flat(m_c), flat(ctx), gt_c, w_out_l, None))

        px, px_hy = _in_proj(x, norm_w[l], sc_x, sh_x, gate_par, w_in_l)
        oa, _, _ = _hgrn(px, g_norm_w[l], s_f, s_b, with_out=True)
        vin, m = _hyena_pre(px_hy, conv_par)
        filt_x = _hyena_filter(z_lat, *filt_args)
        y = _long_conv_two_stage(vin, filt_x, hy_bias[l].reshape(1, D_B), dc_lat)
        x = _out_proj(oa, y, m, x, gt_x, w_out_l, final_norm_w if last else None)

    return x
```

```python
import functools
import math

import numpy as np
import jax
import jax.numpy as jnp
from jax import lax
from jax.experimental import pallas as pl
from jax.experimental.pallas import tpu as pltpu

F32 = jnp.float32
BF16 = jnp.bfloat16
HIGHEST = lax.Precision.HIGHEST

D_MODEL = 1024
DEPTH = 2
D_A = 512
D_B = 512
A_HEADS = 4
HEAD_DIM = 128
N_IN = 5 * D_A + 4 * D_B
HY_EMB = 33
HY_BANDS = 16
HY_WIDTH = 64
HY_MIN_DECAY = math.log(1e-2) / 1.5
HY_MAX_DECAY = math.log(1e-2) / 0.3
EPS = 1e-6

LANES = 128
SUBLANES = 8
SCAN_CHUNK = 64
SCAN_UNROLL = 8
SCAN_GUARD = 80.0
DFT_SLAB = 32
FFT_MINOR = 128
VMEM_LIMIT = 56 * 1024 * 1024


def _cparams(*sem):
    return pltpu.CompilerParams(dimension_semantics=sem, vmem_limit_bytes=VMEM_LIMIT)


def _mod_kernel(c_ref, w_ref, b_ref, o_ref):
    cc = c_ref[...]
    s = cc * jax.nn.sigmoid(cc)
    o_ref[...] = jnp.dot(s, w_ref[...], precision=HIGHEST,
                         preferred_element_type=F32) + b_ref[...]


def _modulation(cc, w_ada, b_ada):
    R, D = cc.shape
    N = w_ada.shape[-1]
    tn = 768
    return pl.pallas_call(
        _mod_kernel,
        grid=(DEPTH, N // tn),
        in_specs=[
            pl.BlockSpec((R, D), lambda l, j: (0, 0)),
            pl.BlockSpec((None, D, tn), lambda l, j: (l, 0, j)),
            pl.BlockSpec((None, 1, tn), lambda l, j: (l, 0, j)),
        ],
        out_specs=pl.BlockSpec((None, R, tn), lambda l, j: (l, 0, j)),
        out_shape=jax.ShapeDtypeStruct((DEPTH, R, N), F32),
        compiler_params=_cparams("arbitrary", "arbitrary"),
        name="modulation",
    )(cc, w_ada, b_ada.reshape(DEPTH, 1, N))


def _gates(f, par):
    loglb = par[0:1, :]
    log1mlb = par[1:2, :]
    onemlb = par[2:3, :]
    s1 = jnp.log(1.0 + jnp.exp(-jnp.abs(f)))
    b = log1mlb + (jnp.minimum(f, 0.0) - s1)
    ng = -jnp.maximum(loglb, b) - jnp.log(1.0 + jnp.exp(-jnp.abs(loglb - b)))
    k = onemlb * jnp.exp(-(jnp.maximum(f, 0.0) + s1))
    return ng, k


N_SCAN_GROUPS = 5


def _inproj_kernel(x_ref, nw_ref, sc_ref, sh_ref, gp_ref, w_ref, oa_ref, *rest):
    ob_ref = rest[0] if len(rest) == 2 else None
    hx_ref = rest[-1]
    x = x_ref[...]
    ms = jnp.mean(x * x, axis=-1, keepdims=True)
    y = x * lax.rsqrt(ms + EPS) * nw_ref[...]
    hx_ref[...] = (y * (1.0 + sc_ref[...]) + sh_ref[...]).astype(BF16)
    gp = gp_ref[...]
    w = D_A
    for g in range(w_ref.shape[-1] // w):
        r = jnp.dot(hx_ref[...], w_ref[:, g * w:(g + 1) * w], preferred_element_type=F32)
        if g == 0:
            oa_ref[:, 0:w] = r * jax.nn.sigmoid(r)
        elif g in (1, 2):
            ng, k = _gates(r, gp[3 * (g - 1):3 * g])
            oa_ref[:, (2 * g - 1) * w:2 * g * w] = ng
            oa_ref[:, 2 * g * w:(2 * g + 1) * w] = k
        elif g < N_SCAN_GROUPS:
            oa_ref[:, (g + 2) * w:(g + 3) * w] = r
        else:
            ob_ref[:, (g - N_SCAN_GROUPS) * w:(g - N_SCAN_GROUPS + 1) * w] = r.astype(ob_ref.dtype)


def _in_proj(x, norm_w, sc, sh, gate_par, w_bf16):
    B, L, D = x.shape
    N = w_bf16.shape[1]
    n_a = min(N // D_A, N_SCAN_GROUPS)
    Na = (n_a + 2) * D_A
    Nb = N - n_a * D_A
    tm = min(L, 512)
    out_specs = [pl.BlockSpec((None, tm, Na), lambda b, i: (b, i, 0))]
    out_shape = [jax.ShapeDtypeStruct((B, L, Na), F32)]
    if Nb:
        out_specs.append(pl.BlockSpec((None, tm, Nb), lambda b, i: (b, i, 0)))
        out_shape.append(jax.ShapeDtypeStruct((B, L, Nb), BF16))
    res = pl.pallas_call(
        _inproj_kernel,
        grid=(B, L // tm),
        in_specs=[
            pl.BlockSpec((None, tm, D), lambda b, i: (b, i, 0)),
            pl.BlockSpec((1, D), lambda b, i: (0, 0)),
            pl.BlockSpec((None, 1, D), lambda b, i: (b, 0, 0)),
            pl.BlockSpec((None, 1, D), lambda b, i: (b, 0, 0)),
            pl.BlockSpec((SUBLANES, D_A), lambda b, i: (0, 0)),
            pl.BlockSpec((D, N), lambda b, i: (0, 0)),
        ],
        out_specs=out_specs,
        out_shape=out_shape,
        scratch_shapes=[pltpu.VMEM((tm, D), BF16)],
        compiler_params=_cparams("arbitrary", "arbitrary"),
        name="in_proj",
    )(x, norm_w.reshape(1, D), sc, sh, gate_par, w_bf16)
    return (res[0], res[1]) if Nb else (res[0], None)


def _scan_consts(C, forward):
    idx = np.arange(C)
    i = idx[:, None]
    t = idx[None, :]
    if forward:
        mats = [t <= i, t > i]
    else:
        mats = [t >= i, t < i]
    masks = [i == t]
    h = 1
    while h < C:
        P = 2 * h
        p = i % P
        m = i - p + h
        upper = p >= h
        same = (i // P) == (t // P)
        if forward:
            mats.append(np.where(upper, (t >= m) & (t <= i), (t >= i + 1) & (t <= m - 1)))
            masks.append(same & upper & ((t % P) < h))
        else:
            mats.append(np.where(upper, (t >= m) & (t <= i - 1), (t >= i) & (t <= m - 1)))
            masks.append(same & (~upper) & ((t % P) >= h))
        h = P
    big = np.concatenate([m_.astype(np.float32) for m_ in mats], axis=0)
    big = np.concatenate([big, big, big], axis=1)
    msk = np.stack([m_.astype(np.float32) for m_ in masks], axis=0)
    return big, msk


def _dot_nt(a, b):
    return lax.dot_general(a, b, (((1,), (1,)), ((), ())), preferred_element_type=F32)


def _dot_tn(a, b):
    return lax.dot_general(a, b, (((0,), (0,)), ((), ())), preferred_element_type=F32)


def _split3(a):
    hi = a.astype(BF16)
    r1 = a - hi.astype(F32)
    mid = r1.astype(BF16)
    lo = (r1 - mid.astype(F32)).astype(BF16)
    return hi, mid, lo


def _cumulative(tri_ref, ng):
    return jnp.dot(tri_ref[...], jnp.concatenate(_split3(ng), axis=0), preferred_element_type=F32)


def _fast_stage_sums(q, ng, k, v, tri_ref):
    return dict(q=q, k=k, vb=v.astype(BF16), cum=_cumulative(tri_ref, ng))


def _fast_stage_scores(s, C, forward):
    h = C // 2
    q, k, cum = s["q"], s["k"], s["cum"]
    if forward:
        early, late = slice(0, h), slice(h, C)
        ref, tot = cum[h - 1:h, :], cum[C - 1:C, :]
    else:
        early, late = slice(h, C), slice(0, h)
        ref, tot = cum[h:h + 1, :], cum[0:1, :]
    qd = (q * jnp.exp(-cum)).astype(BF16)
    dl = cum - ref
    k_early = (k[early] * jnp.exp(cum[early])).astype(BF16)
    q_late = (q[late] * jnp.exp(-dl[late])).astype(BF16)
    k_all = (k * jnp.exp(dl)).astype(BF16)
    kd = (k * jnp.exp(cum - tot)).astype(BF16)
    return dict(vb=s["vb"], qd=qd, kd=kd, e_all=jnp.exp(-tot),
                guard=jnp.maximum(cum[early], dl[late]),
                s_early=_dot_nt(qd[early], k_early), s_late=_dot_nt(q_late, k_all))


def _fast_stage_intra(s, C, forward):
    h = C // 2
    early = slice(0, h) if forward else slice(h, C)
    ri = lax.broadcasted_iota(jnp.int32, (h, h), 0)
    ci = lax.broadcasted_iota(jnp.int32, (h, h), 1)
    rl = lax.broadcasted_iota(jnp.int32, (h, C), 0) + (h if forward else 0)
    cl = lax.broadcasted_iota(jnp.int32, (h, C), 1)
    keep_e = (ci <= ri) if forward else (ci >= ri)
    keep_l = (cl <= rl) if forward else (cl >= rl)
    s_early = jnp.where(keep_e, s["s_early"], 0.0).astype(BF16)
    s_late = jnp.where(keep_l, s["s_late"], 0.0).astype(BF16)
    o_early = jnp.dot(s_early, s["vb"][early], preferred_element_type=F32)
    o_late = jnp.dot(s_late, s["vb"], preferred_element_type=F32)
    o_intra = jnp.concatenate([o_early, o_late] if forward else [o_late, o_early], axis=0)
    return dict(qd=s["qd"], e_all=s["e_all"], o_intra=o_intra, upd=_dot_tn(s["vb"], s["kd"]))


def _fast_stage_state(s, st):
    o = s["o_intra"] + _dot_nt(s["qd"], st.astype(BF16))
    return o, st * s["e_all"] + s["upd"]


def _scan_chunk(q, ng, k, v, big_ref, msk_ref, st, C, forward):
    psum =jnp.dot(big_ref[...], jnp.concatenate(_split3(ng), axis=0), preferred_element_type=F32)
    ex = jnp.exp(-psum)
    e_in = ex[0:C]
    e_out = ex[C:2 * C]
    nlev = msk_ref.shape[0] - 1
    o = _dot_nt((q * e_in).astype(BF16), st.astype(BF16))
    scores = msk_ref[0] * _dot_nt(q.astype(BF16), k.astype(BF16))
    for l in range(nlev):
        el = ex[(2 + l) * C:(3 + l) * C]
        scores = scores + msk_ref[1 + l] * _dot_nt((q * el).astype(BF16), (k * el).astype(BF16))
    o = o + jnp.dot(scores.astype(BF16), v.astype(BF16), preferred_element_type=F32)
    e_all = e_in[C - 1:C, :] if forward else e_in[0:1, :]
    st_new = st * e_all + _dot_tn(v.astype(BF16), (k * e_out).astype(BF16))
    return o, st_new


def _hgrn_kernel(*refs, L, C, U, with_out):
    if with_out:
        (q_ref, ngf_ref, kf_ref, ngb_ref, kb_ref, iv_ref, ga_ref, gw_ref, bigf_ref, mskf_ref,
         bigb_ref, mskb_ref, s0f_ref, s0b_ref, o_ref, sf_ref, sb_ref, sf_in, sb_in, of_scr, ob_scr) = refs
    else:
        (q_ref, ngf_ref, kf_ref, ngb_ref, kb_ref, iv_ref, bigf_ref, mskf_ref, bigb_ref, mskb_ref,
         s0f_ref, s0b_ref, sf_ref, sb_ref, sf_in, sb_in) = refs
    n = L // C
    sf_ref[...] = s0f_ref[...]
    sb_ref[...] = s0b_ref[...]
    trif_ref = bigf_ref.at[0:C, :]
    trib_ref = bigb_ref.at[0:C, :]

    def rows_of(it, u):
        c = it * U + u
        return (pl.ds(pl.multiple_of(c * C, C), C), pl.ds(pl.multiple_of((n - 1 - c) * C, C), C))

    def put(rf, rb, o_f, o_b):
        if with_out:
            of_scr[rf, :] = o_f
            ob_scr[rb, :] = o_b

    def body(it, carry):
        sf_in[...] = sf_ref[...]
        sb_in[...] = sb_ref[...]
        rows = [rows_of(it, u) for u in range(U)]
        fwd = [_fast_stage_sums(q_ref[rf, :], ngf_ref[rf, :], kf_ref[rf, :], iv_ref[rf, :], trif_ref)
               for rf, _ in rows]
        bwd = [_fast_stage_sums(q_ref[rb, :], ngb_ref[rb, :], kb_ref[rb, :], iv_ref[rb, :], trib_ref)
               for _, rb in rows]
        fwd = [_fast_stage_scores(s, C, True) for s in fwd]
        bwd = [_fast_stage_scores(s, C, False) for s in bwd]
        guard = jnp.max(functools.reduce(jnp.maximum, [s["guard"] for s in fwd + bwd]))
        fwd = [_fast_stage_intra(s, C, True) for s in fwd]
        bwd = [_fast_stage_intra(s, C, False) for s in bwd]
        st_f, st_b = sf_ref[...], sb_ref[...]
        for u in range(U):
            o_f, st_f = _fast_stage_state(fwd[u], st_f)
            o_b, st_b = _fast_stage_state(bwd[u], st_b)
            put(rows[u][0], rows[u][1], o_f, o_b)
        sf_ref[...] = st_f
        sb_ref[...] = st_b

        @pl.when(jnp.logical_not(guard <= SCAN_GUARD))
        def _():
            sf_ref[...] = sf_in[...]
            sb_ref[...] = sb_in[...]

            def redo(u, carry2):
                rf, rb = rows_of(it, u)
                o_f, s_f = _scan_chunk(q_ref[rf, :], ngf_ref[rf, :], kf_ref[rf, :], iv_ref[rf, :],
                                       bigf_ref, mskf_ref, sf_ref[...], C, True)
                sf_ref[...] = s_f
                o_b, s_b = _scan_chunk(q_ref[rb, :], ngb_ref[rb, :], kb_ref[rb, :], iv_ref[rb, :],
                                       bigb_ref, mskb_ref, sb_ref[...], C, False)
                sb_ref[...] = s_b
                put(rf, rb, o_f, o_b)
                return carry2

            lax.fori_loop(0, U, redo, 0)

        return carry

    lax.fori_loop(0, n // U, body, 0)

    if with_out:
        gw = gw_ref[...]
        R = min(L, 512)

        def fin(r, carry):
            rows = pl.ds(pl.multiple_of(r * R, R), R)
            o = of_scr[rows, :] + ob_scr[rows, :]
            o = o * lax.rsqrt(jnp.mean(o * o, axis=-1, keepdims=True) + EPS) * gw
            ga = ga_ref[rows, :]
            o_ref[rows, :] = (o * (ga * jax.nn.sigmoid(ga))).astype(o_ref.dtype)
            return carry

        lax.fori_loop(0, L // R, fin, 0)


def _hgrn(px, g_norm_w, s0f, s0b, with_out):
    B, L, _ = px.shape
    C = SCAN_CHUNK
    H = A_HEADS
    bigf, mskf = _scan_consts(C, True)
    bigb, mskb = _scan_consts(C, False)
    bigf, bigb = jnp.asarray(bigf, BF16), jnp.asarray(bigb, BF16)
    mskf, mskb = jnp.asarray(mskf, F32), jnp.asarray(mskb, F32)

    def col(j):
        return pl.BlockSpec((None, L, LANES), lambda b, h, j=j: (b, 0, j * H + h))

    def const(a):
        return pl.BlockSpec(a.shape, lambda b, h, nd=a.ndim: (0,) * nd)

    st_spec = pl.BlockSpec((None, None, HEAD_DIM, HEAD_DIM), lambda b, h: (b, h, 0, 0))
    st_shape = jax.ShapeDtypeStruct((B, H, HEAD_DIM, HEAD_DIM), F32)
    in_specs = [col(j) for j in range(6)]
    args = [px] * 6
    if with_out:
        in_specs += [col(6), pl.BlockSpec((1, LANES), lambda b, h: (0, h))]
        args += [px, g_norm_w.reshape(1, D_A)]
    in_specs += [const(bigf), const(mskf), const(bigb), const(mskb), st_spec, st_spec]
    args += [bigf, mskf, bigb, mskb, s0f, s0b]
    out_specs = [st_spec, st_spec]
    out_shape = [st_shape, st_shape]
    scratch = [pltpu.VMEM((HEAD_DIM, HEAD_DIM), F32), pltpu.VMEM((HEAD_DIM, HEAD_DIM), F32)]
    if with_out:
        out_specs = [pl.BlockSpec((None, L, LANES), lambda b, h: (b, 0, h))] + out_specs
        out_shape = [jax.ShapeDtypeStruct((B, L, D_A), BF16)] + out_shape
        scratch += [pltpu.VMEM((L, LANES), F32), pltpu.VMEM((L, LANES), F32)]
    res = pl.pallas_call(
        functools.partial(_hgrn_kernel, L=L, C=C, U=min(SCAN_UNROLL, L // C), with_out=with_out),
        grid=(B, H),
        in_specs=in_specs,
        out_specs=out_specs,
        out_shape=out_shape,
        scratch_shapes=scratch,
        compiler_params=_cparams("arbitrary", "arbitrary"),
        name="hgrn2",
    )(*args)
    if with_out:
        return res[0], res[1], res[2]
    return None, res[0], res[1]


def _conv3(ref, r0, R, L, cp):
    T = 2 * SUBLANES
    cur = ref[pl.ds(r0, R), :].astype(F32)
    prev_tile = ref[pl.ds(pl.multiple_of(jnp.maximum(r0 - T, 0), T), T), :].astype(F32)
    next_tile = ref[pl.ds(pl.multiple_of(jnp.minimum(r0 + R, L - T), T), T), :].astype(F32)
    prow = jnp.where(r0 > 0, prev_tile[T - 1:T, :], 0.0)
    nrow = jnp.where(r0 + R < L, next_tile[0:1, :], 0.0)
    rid = lax.broadcasted_iota(jnp.int32, (R, LANES), 0)
    up = jnp.where(rid == 0, prow, pltpu.roll(cur, 1, 0))
    dn = jnp.where(rid == R - 1, nrow, pltpu.roll(cur, R - 1, 0))
    return cp[3:4, :] + up * cp[0:1, :] + cur * cp[1:2, :] + dn * cp[2:3, :]


def _hyena_pre_kernel(x0_ref, x1_ref, v_ref, gb_ref, c0_ref, c1_ref, c2_ref, vin_ref, m_ref, *, L):
    R = min(L, 256)
    c0 = c0_ref[...]
    c1 = c1_ref[...]
    c2 = c2_ref[...]

    def body(r, carry):
        r0 = pl.multiple_of(r * R, R)
        rows = pl.ds(r0, R)
        x0 = _conv3(x0_ref, r0, R, L, c0)
        x1 = _conv3(x1_ref, r0, R, L, c1)
        vv = _conv3(v_ref, r0, R, L, c2)
        gb = gb_ref[rows, :].astype(F32)
        vin_ref[rows, :] = vv * x1
        m_ref[rows, :] = (x0 * (gb * jax.nn.sigmoid(gb))).astype(m_ref.dtype)
        return carry

    lax.fori_loop(0, L // R, body, 0)


def _hyena_pre(px, conv_par):
    B, L, _ = px.shape
    nb = D_B // LANES

    def col(j):
        return pl.BlockSpec((None, L, LANES), lambda b, c, j=j: (b, 0, j * nb + c))

    def cpar(j):
        return pl.BlockSpec((SUBLANES, LANES), lambda b, c, j=j: (0, j * nb + c))

    out_spec = pl.BlockSpec((None, L, LANES), lambda b, c: (b, 0, c))
    out_shape = jax.ShapeDtypeStruct((B, L, D_B), F32)
    return pl.pallas_call(
        functools.partial(_hyena_pre_kernel, L=L),
        grid=(B, nb),
        in_specs=[col(0), col(1), col(2), col(3), cpar(0), cpar(1), cpar(2)],
        out_specs=[out_spec, out_spec],
        out_shape=[out_shape, jax.ShapeDtypeStruct((B, L, D_B), BF16)],
        compiler_params=_cparams("arbitrary", "arbitrary"),
        name="hyena_pre",
    )(px, px, px, px, conv_par, conv_par, conv_par)


def _pos_rows(L, order):
    f32 = np.float32
    t = np.linspace(0.0, 1.0, L, dtype=f32)[:, None]
    w = (f32(2.0 * math.pi) * np.arange(L, dtype=f32)[:, None]) / f32(L)
    f = np.linspace(1e-4, HY_BANDS - 1, HY_BANDS, dtype=f32)[None, :]
    z = np.concatenate([t, np.cos(f * w), -np.sin(f * w)], axis=-1).astype(f32)
    lag = np.arange(2 * L)
    src = np.where(lag < L, lag, 2 * L - lag) % L
    fwd = (lag < L).astype(f32)
    valid = (lag != L).astype(f32)
    rows = np.concatenate([z[src], fwd[:, None], valid[:, None],
                           np.zeros((2 * L, HY_WIDTH - HY_EMB - 2), f32)], axis=-1)
    return jnp.asarray(rows[order])


def _filter_kernel(z_ref, zt_ref, w1_ref, w2_ref, w3_ref, vec_ref, w4h_ref, w4l_ref, dl_ref, k_ref,
                   *, n_rows):
    R = min(n_rows, 512)
    S = min(R, LANES)
    dl = dl_ref[...]
    vec = vec_ref[...]
    b1, b2, b3, fr = vec[:, 0:1], vec[:, 1:2], vec[:, 2:3], vec[:, 3:4]

    def body(r, acc):
        h = jnp.sin(fr * (jnp.dot(w1_ref[...], zt_ref[r], precision=HIGHEST,
                                  preferred_element_type=F32) + b1))
        h = jnp.sin(fr * (jnp.dot(w2_ref[...], h, precision=HIGHEST, preferred_element_type=F32) + b2))
        h = jnp.sin(fr * (jnp.dot(w3_ref[...], h, precision=HIGHEST, preferred_element_type=F32) + b3))
        for s0 in range(0, R, S):
            rows = pl.ds(pl.multiple_of(r * R + s0, S), S)
            z = z_ref[rows, :]
            hs = h[:, s0:s0 + S].T
            hh = hs.astype(BF16)
            hl = (hs - hh.astype(F32)).astype(BF16)
            hfb = (jnp.dot(hh, w4h_ref[...], preferred_element_type=F32)
                   + jnp.dot(hl, w4h_ref[...], preferred_element_type=F32)
                   + jnp.dot(hh, w4l_ref[...], preferred_element_type=F32))
            fwd = z[:, HY_EMB:HY_EMB + 1]
            valid = z[:, HY_EMB + 1:HY_EMB + 2]
            k = (jnp.where(fwd > 0.5, hfb[:, :D_B], hfb[:, D_B:])
                 * jnp.exp(-z[:, 0:1] * dl) * valid)
            k_ref[rows, :] = k
            acc = acc + jnp.sum(jnp.abs(k), axis=0, keepdims=True)
        return acc

    tot = lax.fori_loop(0, n_rows // R, body, jnp.zeros((1, D_B), F32))

    def scale(r, carry):
        rows = pl.ds(pl.multiple_of(r * R, R), R)
        k_ref[rows, :] = k_ref[rows, :] / tot
        return carry

    lax.fori_loop(0, n_rows // R, scale, 0)


def _hyena_filter(zrows, w1, b1, freq, w2, b2, w3, b3, w4):
    n_rows = zrows.shape[0]
    R = min(n_rows, 512)
    zt = zrows.reshape(n_rows // R, R, HY_WIDTH).transpose(0, 2, 1)
    w1t = jnp.concatenate([w1, jnp.zeros((HY_WIDTH - HY_EMB, HY_WIDTH), F32)], axis=0).T
    vec = jnp.concatenate([b1[:, None], b2[:, None], b3[:, None], freq[:, None],
                           jnp.zeros((HY_WIDTH, SUBLANES - 4), F32)], axis=1)
    deltas = jnp.abs(jnp.linspace(HY_MIN_DECAY, HY_MAX_DECAY, D_B, dtype=F32))[None, :]
    w4h, w4l = _hi_lo(w4)

    def const(shape):
        return pl.BlockSpec(shape, lambda c, nd=len(shape): (0,) * nd)

    return pl.pallas_call(
        functools.partial(_filter_kernel, n_rows=n_rows),
        grid=(1,),
        in_specs=[const((n_rows, HY_WIDTH)), const(zt.shape), const((HY_WIDTH, HY_WIDTH)),
                  const((HY_WIDTH, HY_WIDTH)), const((HY_WIDTH, HY_WIDTH)), const((HY_WIDTH, SUBLANES)),
                  const((HY_WIDTH, 2 * D_B)), const((HY_WIDTH, 2 * D_B)), const((1, D_B))],
        out_specs=const((n_rows, D_B)),
        out_shape=jax.ShapeDtypeStruct((n_rows, D_B), F32),
        compiler_params=_cparams("arbitrary"),
        name="hyena_filter",
    )(zrows, zt, w1t, w2.T, w3.T, vec, w4h, w4l, deltas)


def _cplx_block(gr, gi):
    return np.block([[gr, -gi], [gi, gr]])


def _interleave(n):
    return np.stack([np.arange(n), n + np.arange(n)], axis=1).reshape(-1)


def _hi_lo(a):
    a = jnp.asarray(a, F32)
    hi = a.astype(BF16)
    return hi, (a - hi.astype(F32)).astype(BF16)


def _dft_consts(L):
    N = 2 * L
    n2 = FFT_MINOR
    n1 = N // n2
    half = n1 // 2
    f1 = np.arange(n1)
    t2 = np.arange(n2)
    t1 = np.arange(n1)
    tt = n2 * t1[None, None, :] + t2[:, None, None]
    ang = -2.0 * np.pi * ((f1[None, :, None] * tt) % N) / N
    gr, gi = np.cos(ang), np.sin(ang)
    m1_data = np.stack([_cplx_block(gr[j][:, :half], gi[j][:, :half]) for j in range(n2)])
    m1_filt = np.concatenate([gr, gi], axis=1)
    m1_inv = np.stack([_cplx_block(gr[j][:, :half].T, -gi[j][:, :half].T) for j in range(n2)]) / N
    ang2 = -2.0 * np.pi * ((t2[:, None] * t2[None, :]) % n2) / n2
    f2 = _cplx_block(np.cos(ang2), np.sin(ang2))
    f2_inv = _cplx_block(np.cos(ang2), -np.sin(ang2))
    il1 = _interleave(n1)
    il2 = _interleave(n2)
    as_bf16 = lambda a: jnp.asarray(a, F32).astype(BF16)
    return dict(n1=n1, n2=n2, m1_filt=_hi_lo(m1_filt), f2=_hi_lo(f2[None]),
                m1_data=as_bf16(m1_data[:, il1, :]), f2_packed=as_bf16(f2[:, il2]),
                f2_inv_packed=as_bf16(f2_inv[il2, :]), m1_inv=as_bf16(m1_inv[:, :, il1]))


def _dft_single_consts(L):
    N = 2 * L
    f = np.arange(N)
    ang = -2.0 * np.pi * ((f[:, None] * f[None, :]) % N) / N
    gr, gi = np.cos(ang), np.sin(ang)
    fwd = _cplx_block(gr[:, :L], gi[:, :L])
    filt = np.concatenate([gr, gi], axis=0)
    inv = _cplx_block(gr[:L, :], -gi[:L, :]) / N
    return dict(fwd=_hi_lo(fwd[None]), filt=_hi_lo(filt[None]), inv=_hi_lo(inv[None]))


def _bmm_kernel(mh_ref, ml_ref, x_ref, o_ref, *, gblk, shared, precise):
    for j in range(gblk):
        jm = 0 if shared else j
        x = x_ref[j]
        xh = x.astype(BF16)
        acc = jnp.dot(mh_ref[jm], xh, preferred_element_type=F32)
        if precise:
            xl = (x - xh.astype(F32)).astype(BF16)
            acc = (acc + jnp.dot(mh_ref[jm], xl, preferred_element_type=F32)
                   + jnp.dot(ml_ref[jm], xh, preferred_element_type=F32))
        o_ref[j] = acc.astype(o_ref.dtype)


def _bmm_left(m, x, gblk, precise, out_dtype):
    mh, ml = m
    P, G, K, C = x.shape
    R = mh.shape[1]
    shared = mh.shape[0] == 1
    gblk = min(gblk, G)
    m_spec = (pl.BlockSpec((1, R, K), lambda g, p: (0, 0, 0)) if shared
              else pl.BlockSpec((gblk, R, K), lambda g, p: (g, 0, 0)))
    return pl.pallas_call(
        functools.partial(_bmm_kernel, gblk=gblk, shared=shared, precise=precise),
        grid=(G // gblk, P),
        in_specs=[m_spec, m_spec, pl.BlockSpec((None, gblk, K, C), lambda g, p: (p, g, 0, 0))],
        out_specs=pl.BlockSpec((None, gblk, R, C), lambda g, p: (p, g, 0, 0)),
        out_shape=jax.ShapeDtypeStruct((P, G, R, C), out_dtype),
        compiler_params=_cparams("arbitrary", "arbitrary"),
        name="dft_stage",
    )(mh, ml, x)


def _mid_kernel(fa_ref, fb_ref, kf_ref, hb_ref, d_ref, o_ref, *, gblk, nf):
    fa = fa_ref[...]
    fb = fb_ref[...]
    hb = hb_ref[...]
    for j in range(gblk):
        xx = jnp.dot(fa, d_ref[j].astype(BF16), preferred_element_type=F32)
        xr, xi = xx[:nf], xx[nf:]
        kr, ki = kf_ref[j, :nf] + hb, kf_ref[j, nf:]
        yy = jnp.concatenate([xr * kr - xi * ki, xr * ki + xi * kr], axis=0)
        o_ref[j] = jnp.dot(fb, yy.astype(BF16), preferred_element_type=F32).astype(o_ref.dtype)


def _spectral_mid(fa, fb, kf, hb, d, gblk, out_dtype):
    P, G, K, C = d.shape
    nf2 = fa.shape[0]
    Ko = fb.shape[0]
    gblk = min(gblk, G)
    return pl.pallas_call(
        functools.partial(_mid_kernel, gblk=gblk, nf=nf2 // 2),
        grid=(G // gblk, P),
        in_specs=[pl.BlockSpec(fa.shape, lambda g, p: (0, 0)),
                  pl.BlockSpec(fb.shape, lambda g, p: (0, 0)),
                  pl.BlockSpec((gblk, nf2, C), lambda g, p: (g, 0, 0)),
                  pl.BlockSpec((1, C), lambda g, p: (0, 0)),
                  pl.BlockSpec((None, gblk, K, C), lambda g, p: (p, g, 0, 0))],
        out_specs=pl.BlockSpec((None, gblk, Ko, C), lambda g, p: (p, g, 0, 0)),
        out_shape=jax.ShapeDtypeStruct((P, G, Ko, C), out_dtype),
        compiler_params=_cparams("arbitrary", "arbitrary"),
        name="dft_mid",
    )(fa, fb, kf, hb, d)


def _filter_first_kernel(mh_ref, ml_ref, x_ref, o_ref, *, gblk, n1):
    for j in range(gblk):
        x = x_ref[j]
        xh = x.astype(BF16)
        xl = (x - xh.astype(F32)).astype(BF16)
        acc = (jnp.dot(mh_ref[j], xh, preferred_element_type=F32)
               + jnp.dot(mh_ref[j], xl, preferred_element_type=F32)
               + jnp.dot(ml_ref[j], xh, preferred_element_type=F32))
        o_ref[:, 0, j, :] = acc[:n1]
        o_ref[:, 1, j, :] = acc[n1:]


def _filter_first_stage(m, x, n1, n2):
    mh, ml = m
    C = x.shape[-1]
    gblk = min(DFT_SLAB, n2)
    m_spec = pl.BlockSpec((gblk,) + mh.shape[1:], lambda g: (g, 0, 0))
    return pl.pallas_call(
        functools.partial(_filter_first_kernel, gblk=gblk, n1=n1),
        grid=(n2 // gblk,),
        in_specs=[m_spec, m_spec, pl.BlockSpec((gblk, n1, C), lambda g: (g, 0, 0))],
        out_specs=pl.BlockSpec((n1, 2, gblk, C), lambda g: (0, 0, g, 0)),
        out_shape=jax.ShapeDtypeStruct((n1, 2, n2, C), F32),
        compiler_params=_cparams("arbitrary"),
        name="dft_filter_first",
    )(mh, ml, x)


def _first_stage_kernel(m_ref, x_ref, o_ref, *, gblk):
    for j in range(gblk):
        x = jnp.concatenate([x_ref[0, :, j, :], x_ref[1, :, j, :]], axis=0).astype(BF16)
        acc = jnp.dot(m_ref[j], x, preferred_element_type=F32)
        o_ref[:, j, :] = pltpu.bitcast(acc.astype(BF16), jnp.uint32)


def _first_stage(m, x, n1, n2):
    _, P, half, _, C = x.shape
    gblk = min(DFT_SLAB, n2)
    return pl.pallas_call(
        functools.partial(_first_stage_kernel, gblk=gblk),
        grid=(n2 // gblk, P),
        in_specs=[pl.BlockSpec((gblk,) + m.shape[1:], lambda g, p: (g, 0, 0)),
                  pl.BlockSpec((2, None, half, gblk, C), lambda g, p: (0, p, 0, g, 0))],
        out_specs=pl.BlockSpec((None, n1, gblk, C), lambda g, p: (p, 0, g, 0)),
        out_shape=jax.ShapeDtypeStruct((P, n1, n2, C), jnp.uint32),
        compiler_params=_cparams("arbitrary", "arbitrary"),
        name="dft_first",
    )(m, x)


def _packed_mid_kernel(fa_ref, fb_ref, kf_ref, hb_ref, d_ref, o_ref, *, gblk, nf):
    fa = fa_ref[...]
    fb = fb_ref[...]
    hb = hb_ref[...]
    for j in range(gblk):
        xx = jnp.dot(fa, pltpu.bitcast(d_ref[j], BF16), preferred_element_type=F32)
        xr, xi = xx[:nf], xx[nf:]
        kr, ki = kf_ref[j, :nf] + hb, kf_ref[j, nf:]
        yy = jnp.concatenate([xr * kr - xi * ki, xr * ki + xi * kr], axis=0)
        zz = jnp.dot(fb, yy.astype(BF16), preferred_element_type=F32)
        o_ref[j] = pltpu.bitcast(zz.astype(BF16), jnp.uint32)


def _packed_mid(fa, fb, kf, hb, d, gblk):
    P, G, K, C = d.shape
    gblk = min(gblk, G)
    return pl.pallas_call(
        functools.partial(_packed_mid_kernel, gblk=gblk, nf=K),
        grid=(G // gblk, P),
        in_specs=[pl.BlockSpec(fa.shape, lambda g, p: (0, 0)),
                  pl.BlockSpec(fb.shape, lambda g, p: (0, 0)),
                  pl.BlockSpec((gblk, 2 * K, C), lambda g, p: (g, 0, 0)),
                  pl.BlockSpec((1, C), lambda g, p: (0, 0)),
                  pl.BlockSpec((None, gblk, K, C), lambda g, p: (p, g, 0, 0))],
        out_specs=pl.BlockSpec((None, gblk, K, C), lambda g, p: (p, g, 0, 0)),
        out_shape=jax.ShapeDtypeStruct((P, G, K, C), jnp.uint32),
        compiler_params=_cparams("arbitrary", "arbitrary"),
        name="dft_mid",
    )(fa, fb, kf, hb, d)


def _last_stage_kernel(m_ref, z_ref, y_ref, *, gblk, half):
    for j in range(gblk):
        z = pltpu.bitcast(z_ref[:, j, :], BF16)
        y = jnp.dot(m_ref[j], z, preferred_element_type=F32)
        y_ref[0, :, j, :] = y[:half]
        y_ref[1, :, j, :] = y[half:]


def _last_stage(m, z, half):
    P, n1, n2, C = z.shape
    gblk = min(DFT_SLAB, n2)
    return pl.pallas_call(
        functools.partial(_last_stage_kernel, gblk=gblk, half=half),
        grid=(n2 // gblk, P),
        in_specs=[pl.BlockSpec((gblk,) + m.shape[1:], lambda g, p: (g, 0, 0)),
                  pl.BlockSpec((None, n1, gblk, C), lambda g, p: (p, 0, g, 0))],
        out_specs=pl.BlockSpec((2, None, half, gblk, C), lambda g, p: (0, p, 0, g, 0)),
        out_shape=jax.ShapeDtypeStruct((2, P, half, n2, C), F32),
        compiler_params=_cparams("arbitrary", "arbitrary"),
        name="dft_last",
    )(m, z)


def _long_conv_two_stage(vin, filt_rows, hy_bias, dc):
    B, L, C = vin.shape
    n1, n2 = dc["n1"], dc["n2"]
    half = n1 // 2
    P = B // 2
    ka = _filter_first_stage(dc["m1_filt"], filt_rows.reshape(n2, n1, C), n1, n2)
    kf = _bmm_left(dc["f2"], ka.reshape(1, n1, 2 * n2, C), 8, True, F32)[0]
    a = _first_stage(dc["m1_data"], vin.reshape(2, P, half, n2, C), n1, n2)
    z = _packed_mid(dc["f2_packed"], dc["f2_inv_packed"], kf, hy_bias, a, 8)
    y = _last_stage(dc["m1_inv"], z, half)
    return y.reshape(B, L, C)


def _long_conv_single(vin, filt, hy_bias, dc):
    B, L, C = vin.shape
    P = B // 2
    kf = _bmm_left(dc["filt"], filt.reshape(1, 1, 2 * L, C), 1, True, F32)[0]
    d = vin.reshape(2, P, L, C).transpose(1, 0, 2, 3).reshape(P, 1, 2 * L, C)
    y = _spectral_mid(dc["fwd"][0][0], dc["inv"][0][0], kf, hy_bias, d, 1, F32)
    return y.reshape(P, 2, L, C).transpose(1, 0, 2, 3).reshape(B, L, C)


def _outproj_kernel(*refs, final):
    if final:
        oa_ref, y_ref, m_ref, x_ref, gt_ref, w_ref, fw_ref, o_ref = refs
    else:
        oa_ref, y_ref, m_ref, x_ref, gt_ref, w_ref, o_ref = refs
    ob = y_ref[...] * m_ref[...]
    r = (jnp.dot(oa_ref[...], w_ref[0:D_A, :], preferred_element_type=F32)
         + jnp.dot(ob.astype(BF16), w_ref[D_A:D_A + D_B, :], preferred_element_type=F32))
    xn = x_ref[...] + gt_ref[...] * r
    if final:
        ms = jnp.mean(xn * xn, axis=-1, keepdims=True)
        xn = xn * lax.rsqrt(ms + EPS) * fw_ref[...]
    o_ref[...] = xn


def _out_proj(oa, y, m, x, gt, w_bf16, final_w):
    B, L, D = x.shape
    tm = min(L, 512)
    final = final_w is not None
    half = lambda: pl.BlockSpec((None, tm, D_B), lambda b, i: (b, i, 0))
    in_specs = [half(), half(), half(),
                pl.BlockSpec((None, tm, D), lambda b, i: (b, i, 0)),
                pl.BlockSpec((None, 1, D), lambda b, i: (b, 0, 0)),
                pl.BlockSpec((D, D), lambda b, i: (0, 0))]
    args = [oa, y, m, x, gt, w_bf16]
    if final:
        in_specs.append(pl.BlockSpec((1, D), lambda b, i: (0, 0)))
        args.append(final_w.reshape(1, D))
    return pl.pallas_call(
        functools.partial(_outproj_kernel, final=final),
        grid=(B, L // tm),
        in_specs=in_specs,
        out_specs=pl.BlockSpec((None, tm, D), lambda b, i: (b, i, 0)),
        out_shape=jax.ShapeDtypeStruct((B, L, D), F32),
        compiler_params=_cparams("arbitrary", "arbitrary"),
        name="out_proj",
    )(*args)


def _gate_params(lb):
    rows = []
    for d in range(2):
        rows += [jnp.log(lb[d]), jnp.log1p(-lb[d]), 1.0 - lb[d]]
    rows += [jnp.zeros_like(lb[0])] * (SUBLANES - len(rows))
    return jnp.stack(rows, axis=0)


def kernel(x, c, ctx, c_ctx, norm_w, w_ada, b_ada, w_in, w_out, lb_logits, g_norm_w,
           conv_w, conv_b, hy_w1, hy_b1, hy_freq, hy_w2, hy_b2, hy_w3, hy_b3, hy_w4,
           hy_bias, final_norm_w):
    B, L_lat, D = x.shape
    L_ctx = ctx.shape[1]
    p_lb = jax.nn.softmax(lb_logits.astype(F32), axis=0)
    lbs = jnp.cumsum(p_lb, axis=0)
    lbs = lbs - lbs[0:1]

    n_rows = 2 * SUBLANES
    cc = jnp.zeros((n_rows, D), F32).at[:B].set(c).at[B].set(c_ctx)
    mod = _modulation(cc, w_ada, b_ada)

    dc_lat = _dft_consts(L_lat)
    dc_ctx = _dft_single_consts(L_ctx)
    n1, n2 = dc_lat["n1"], dc_lat["n2"]
    order_lat = (np.arange(n2)[:, None] + n2 * np.arange(n1)[None, :]).reshape(-1)
    z_lat = _pos_rows(L_lat, order_lat)
    z_ctx = _pos_rows(L_ctx, np.arange(2 * L_ctx))
    zero_state = jnp.zeros((B, A_HEADS, HEAD_DIM, HEAD_DIM), F32)

    for l in range(DEPTH):
        last = l == DEPTH - 1
        sh_x, sc_x, gt_x = [mod[l, :B, j * D:(j + 1) * D].reshape(B, 1, D) for j in range(3)]
        sh_c, sc_c, gt_c = [mod[l, B, j * D:(j + 1) * D].reshape(1, 1, D) for j in range(3)]
        w_in_l = w_in[l].astype(BF16)
        w_out_l = w_out[l].astype(BF16)
        gate_par = _gate_params(lbs[l])
        conv_par = jnp.concatenate([conv_w[l], conv_b[l][None],
                                    jnp.zeros((SUBLANES - 4, 3 * D_B), F32)], axis=0)
        filt_args = (hy_w1[l], hy_b1[l], hy_freq[l], hy_w2[l], hy_b2[l], hy_w3[l], hy_b3[l], hy_w4[l])

        flat = lambda a: a.reshape(1, B * L_ctx, a.shape[-1])
        unflat = lambda a: a.reshape(B, L_ctx, a.shape[-1])
        if last:
            pc, _ = _in_proj(flat(ctx), norm_w[l], sc_c, sh_c, gate_par, w_in_l[:, :4 * D_A])
            _, s_f, s_b = _hgrn(unflat(pc), g_norm_w[l], zero_state, zero_state, with_out=False)
        else:
            pc, pc_hy = _in_proj(flat(ctx), norm_w[l], sc_c, sh_c, gate_par, w_in_l)
            oa_c, s_f, s_b = _hgrn(unflat(pc), g_norm_w[l], zero_state, zero_state, with_out=True)
            vin_c, m_c = _hyena_pre(unflat(pc_hy), conv_par)
            filt_c = _hyena_filter(z_ctx, *filt_args)
            y_c = _long_conv_single(vin_c, filt_c, hy_bias[l].reshape(1, D_B), dc_ctx)
            ctx = unflat(_out_proj(flat(oa_c), flat(y_c), flat(m_c), flat(ctx), gt_c, w_out_l, None))

        px, px_hy = _in_proj(x, norm_w[l], sc_x, sh_x, gate_par, w_in_l)
        oa, _, _ = _hgrn(px, g_norm_w[l], s_f, s_b, with_out=True)
        vin, m = _hyena_pre(px_hy, conv_par)
        filt_x = _hyena_filter(z_lat, *filt_args)
        y = _long_conv_two_stage(vin, filt_x, hy_bias[l].reshape(1, D_B), dc_lat)
        x = _out_proj(oa, y, m, x, gt_x, w_out_l, final_norm_w if last else None)

    return x
```

```python
import functools
import math

import numpy as np
import jax
import jax.numpy as jnp
from jax import lax
from jax.experimental import pallas as pl
from jax.experimental.pallas import tpu as pltpu

F32 = jnp.float32
BF16 = jnp.bfloat16
HIGHEST = lax.Precision.HIGHEST

D_MODEL = 1024
DEPTH = 2
D_A = 512
D_B = 512
A_HEADS = 4
HEAD_DIM = 128
N_IN = 5 * D_A + 4 * D_B
HY_EMB = 33
HY_BANDS = 16
HY_WIDTH = 64
HY_MIN_DECAY = math.log(1e-2) / 1.5
HY_MAX_DECAY = math.log(1e-2) / 0.3
EPS = 1e-6

LANES = 128
SUBLANES = 8
SCAN_CHUNK = 64
SCAN_UNROLL = 8
SCAN_GUARD = 80.0
DFT_SLAB = 32
FFT_MINOR = 128
VMEM_LIMIT = 56 * 1024 * 1024


def _cparams(*sem):
    return pltpu.CompilerParams(dimension_semantics=sem, vmem_limit_bytes=VMEM_LIMIT)


def _mod_kernel(c_ref, w_ref, b_ref, o_ref):
    cc = c_ref[...]
    s = cc * jax.nn.sigmoid(cc)
    o_ref[...] = jnp.dot(s, w_ref[...], precision=HIGHEST,
                         preferred_element_type=F32) + b_ref[...]


def _modulation(cc, w_ada, b_ada):
    R, D = cc.shape
    N = w_ada.shape[-1]
    tn = 768
    return pl.pallas_call(
        _mod_kernel,
        grid=(DEPTH, N // tn),
        in_specs=[
            pl.BlockSpec((R, D), lambda l, j: (0, 0)),
            pl.BlockSpec((None, D, tn), lambda l, j: (l, 0, j)),
            pl.BlockSpec((None, 1, tn), lambda l, j: (l, 0, j)),
        ],
        out_specs=pl.BlockSpec((None, R, tn), lambda l, j: (l, 0, j)),
        out_shape=jax.ShapeDtypeStruct((DEPTH, R, N), F32),
        compiler_params=_cparams("arbitrary", "arbitrary"),
        name="modulation",
    )(cc, w_ada, b_ada.reshape(DEPTH, 1, N))


def _gates(f, par):
    loglb = par[0:1, :]
    log1mlb = par[1:2, :]
    onemlb = par[2:3, :]
    s1 = jnp.log(1.0 + jnp.exp(-jnp.abs(f)))
    b = log1mlb + (jnp.minimum(f, 0.0) - s1)
    ng = -jnp.maximum(loglb, b) - jnp.log(1.0 + jnp.exp(-jnp.abs(loglb - b)))
    k = onemlb * jnp.exp(-(jnp.maximum(f, 0.0) + s1))
    return ng, k


N_SCAN_GROUPS = 5


def _inproj_kernel(x_ref, nw_ref, sc_ref, sh_ref, gp_ref, w_ref, oa_ref, *rest):
    ob_ref = rest[0] if len(rest) == 2 else None
    hx_ref = rest[-1]
    x = x_ref[...]
    ms = jnp.mean(x * x, axis=-1, keepdims=True)
    y = x * lax.rsqrt(ms + EPS) * nw_ref[...]
    hx_ref[...] = (y * (1.0 + sc_ref[...]) + sh_ref[...]).astype(BF16)
    gp = gp_ref[...]
    w = D_A
    for g in range(w_ref.shape[-1] // w):
        r = jnp.dot(hx_ref[...], w_ref[:, g * w:(g + 1) * w], preferred_element_type=F32)
        if g == 0:
            oa_ref[:, 0:w] = r * jax.nn.sigmoid(r)
        elif g in (1, 2):
            ng, k = _gates(r, gp[3 * (g - 1):3 * g])
            oa_ref[:, (2 * g - 1) * w:2 * g * w] = ng
            oa_ref[:, 2 * g * w:(2 * g + 1) * w] = k
        elif g < N_SCAN_GROUPS:
            oa_ref[:, (g + 2) * w:(g + 3) * w] = r
        else:
            ob_ref[:, (g - N_SCAN_GROUPS) * w:(g - N_SCAN_GROUPS + 1) * w] = r.astype(ob_ref.dtype)


def _in_proj(x, norm_w, sc, sh, gate_par, w_bf16):
    B, L, D = x.shape
    N = w_bf16.shape[1]
    n_a = min(N // D_A, N_SCAN_GROUPS)
    Na = (n_a + 2) * D_A
    Nb = N - n_a * D_A
    tm = min(L, 512)
    out_specs = [pl.BlockSpec((None, tm, Na), lambda b, i: (b, i, 0))]
    out_shape = [jax.ShapeDtypeStruct((B, L, Na), F32)]
    if Nb:
        out_specs.append(pl.BlockSpec((None, tm, Nb), lambda b, i: (b, i, 0)))
        out_shape.append(jax.ShapeDtypeStruct((B, L, Nb), BF16))
    res = pl.pallas_call(
        _inproj_kernel,
        grid=(B, L // tm),
        in_specs=[
            pl.BlockSpec((None, tm, D), lambda b, i: (b, i, 0)),
            pl.BlockSpec((1, D), lambda b, i: (0, 0)),
            pl.BlockSpec((None, 1, D), lambda b, i: (b, 0, 0)),
            pl.BlockSpec((None, 1, D), lambda b, i: (b, 0, 0)),
            pl.BlockSpec((SUBLANES, D_A), lambda b, i: (0, 0)),
            pl.BlockSpec((D, N), lambda b, i: (0, 0)),
        ],
        out_specs=out_specs,
        out_shape=out_shape,
        scratch_shapes=[pltpu.VMEM((tm, D), BF16)],
        compiler_params=_cparams("arbitrary", "arbitrary"),
        name="in_proj",
    )(x, norm_w.reshape(1, D), sc, sh, gate_par, w_bf16)
    return (res[0], res[1]) if Nb else (res[0], None)


def _scan_consts(C, forward):
    idx = np.arange(C)
    i = idx[:, None]
    t = idx[None, :]
    if forward:
        mats = [t <= i, t > i]
    else:
        mats = [t >= i, t < i]
    masks = [i == t]
    h = 1
    while h < C:
        P = 2 * h
        p = i % P
        m = i - p + h
        upper = p >= h
        same = (i // P) == (t // P)
        if forward:
            mats.append(np.where(upper, (t >= m) & (t <= i), (t >= i + 1) & (t <= m - 1)))
            masks.append(same & upper & ((t % P) < h))
        else:
            mats.append(np.where(upper, (t >= m) & (t <= i - 1), (t >= i) & (t <= m - 1)))
            masks.append(same & (~upper) & ((t % P) >= h))
        h = P
    big = np.concatenate([m_.astype(np.float32) for m_ in mats], axis=0)
    big = np.concatenate([big, big, big], axis=1)
    msk = np.stack([m_.astype(np.float32) for m_ in masks], axis=0)
    return big, msk


def _dot_nt(a, b):
    return lax.dot_general(a, b, (((1,), (1,)), ((), ())), preferred_element_type=F32)


def _dot_tn(a, b):
    return lax.dot_general(a, b, (((0,), (0,)), ((), ())), preferred_element_type=F32)


def _split3(a):
    hi = a.astype(BF16)
    r1 = a - hi.astype(F32)
    mid = r1.astype(BF16)
    lo = (r1 - mid.astype(F32)).astype(BF16)
    return hi, mid, lo


def _cumulative(tri_ref, ng):
    return jnp.dot(tri_ref[...], jnp.concatenate(_split3(ng), axis=0), preferred_element_type=F32)


def _fast_stage_sums(q, ng, k, v, tri_ref):
    return dict(q=q, k=k, vb=v.astype(BF16), cum=_cumulative(tri_ref, ng))


def _fast_stage_scores(s, C, forward):
    h = C // 2
    q, k, cum = s["q"], s["k"], s["cum"]
    if forward:
        early, late = slice(0, h), slice(h, C)
        ref, tot = cum[h - 1:h, :], cum[C - 1:C, :]
    else:
        early, late = slice(h, C), slice(0, h)
        ref, tot = cum[h:h + 1, :], cum[0:1, :]
    qd = (q * jnp.exp(-cum)).astype(BF16)
    dl = cum - ref
    k_early = (k[early] * jnp.exp(cum[early])).astype(BF16)
    q_late = (q[late] * jnp.exp(-dl[late])).astype(BF16)
    k_all = (k * jnp.exp(dl)).astype(BF16)
    kd = (k * jnp.exp(cum - tot)).astype(BF16)
    return dict(vb=s["vb"], qd=qd, kd=kd, e_all=jnp.exp(-tot),
                guard=jnp.maximum(cum[early], dl[late]),
                s_early=_dot_nt(qd[early], k_early), s_late=_dot_nt(q_late, k_all))


def _fast_stage_intra(s, C, forward):
    h = C // 2
    early = slice(0, h) if forward else slice(h, C)
    ri = lax.broadcasted_iota(jnp.int32, (h, h), 0)
    ci = lax.broadcasted_iota(jnp.int32, (h, h), 1)
    rl = lax.broadcasted_iota(jnp.int32, (h, C), 0) + (h if forward else 0)
    cl = lax.broadcasted_iota(jnp.int32, (h, C), 1)
    keep_e = (ci <= ri) if forward else (ci >= ri)
    keep_l = (cl <= rl) if forward else (cl >= rl)
    s_early = jnp.where(keep_e, s["s_early"], 0.0).astype(BF16)
    s_late = jnp.where(keep_l, s["s_late"], 0.0).astype(BF16)
    o_early = jnp.dot(s_early, s["vb"][early], preferred_element_type=F32)
    o_late = jnp.dot(s_late, s["vb"], preferred_element_type=F32)
    o_intra = jnp.concatenate([o_early, o_late] if forward else [o_late, o_early], axis=0)
    return dict(qd=s["qd"], e_all=s["e_all"], o_intra=o_intra, upd=_dot_tn(s["vb"], s["kd"]))


def _fast_stage_state(s, st):
    o = s["o_intra"] + _dot_nt(s["qd"], st.astype(BF16))
    return o, st * s["e_all"] + s["upd"]


def _scan_chunk(q, ng, k, v, big_ref, msk_ref, st, C, forward):
    psum =jnp.dot(big_ref[...], jnp.concatenate(_split3(ng), axis=0), preferred_element_type=F32)
    ex = jnp.exp(-psum)
    e_in = ex[0:C]
    e_out = ex[C:2 * C]
    nlev = msk_ref.shape[0] - 1
    o = _dot_nt((q * e_in).astype(BF16), st.astype(BF16))
    scores = msk_ref[0] * _dot_nt(q.astype(BF16), k.astype(BF16))
    for l in range(nlev):
        el = ex[(2 + l) * C:(3 + l) * C]
        scores = scores + msk_ref[1 + l] * _dot_nt((q * el).astype(BF16), (k * el).astype(BF16))
    o = o + jnp.dot(scores.astype(BF16), v.astype(BF16), preferred_element_type=F32)
    e_all = e_in[C - 1:C, :] if forward else e_in[0:1, :]
    st_new = st * e_all + _dot_tn(v.astype(BF16), (k * e_out).astype(BF16))
    return o, st_new


def _hgrn_kernel(*refs, L, C, U, with_out):
    if with_out:
        (q_ref, ngf_ref, kf_ref, ngb_ref, kb_ref, iv_ref, ga_ref, gw_ref, bigf_ref, mskf_ref,
         bigb_ref, mskb_ref, s0f_ref, s0b_ref, o_ref, sf_ref, sb_ref, sf_in, sb_in, of_scr, ob_scr) = refs
    else:
        (q_ref, ngf_ref, kf_ref, ngb_ref, kb_ref, iv_ref, bigf_ref, mskf_ref, bigb_ref, mskb_ref,
         s0f_ref, s0b_ref, sf_ref, sb_ref, sf_in, sb_in) = refs
    n = L // C
    sf_ref[...] = s0f_ref[...]
    sb_ref[...] = s0b_ref[...]
    trif_ref = bigf_ref.at[0:C, :]
    trib_ref = bigb_ref.at[0:C, :]

    def rows_of(it, u):
        c = it * U + u
        return (pl.ds(pl.multiple_of(c * C, C), C), pl.ds(pl.multiple_of((n - 1 - c) * C, C), C))

    def put(rf, rb, o_f, o_b):
        if with_out:
            of_scr[rf, :] = o_f
            ob_scr[rb, :] = o_b

    def body(it, carry):
        sf_in[...] = sf_ref[...]
        sb_in[...] = sb_ref[...]
        rows = [rows_of(it, u) for u in range(U)]
        fwd = [_fast_stage_sums(q_ref[rf, :], ngf_ref[rf, :], kf_ref[rf, :], iv_ref[rf, :], trif_ref)
               for rf, _ in rows]
        bwd = [_fast_stage_sums(q_ref[rb, :], ngb_ref[rb, :], kb_ref[rb, :], iv_ref[rb, :], trib_ref)
               for _, rb in rows]
        fwd = [_fast_stage_scores(s, C, True) for s in fwd]
        bwd = [_fast_stage_scores(s, C, False) for s in bwd]
        guard = jnp.max(functools.reduce(jnp.maximum, [s["guard"] for s in fwd + bwd]))
        fwd = [_fast_stage_intra(s, C, True) for s in fwd]
        bwd = [_fast_stage_intra(s, C, False) for s in bwd]
        st_f, st_b = sf_ref[...], sb_ref[...]
        for u in range(U):
            o_f, st_f = _fast_stage_state(fwd[u], st_f)
            o_b, st_b = _fast_stage_state(bwd[u], st_b)
            put(rows[u][0], rows[u][1], o_f, o_b)
        sf_ref[...] = st_f
        sb_ref[...] = st_b

        @pl.when(jnp.logical_not(guard <= SCAN_GUARD))
        def _():
            sf_ref[...] = sf_in[...]
            sb_ref[...] = sb_in[...]

            def redo(u, carry2):
                rf, rb = rows_of(it, u)
                o_f, s_f = _scan_chunk(q_ref[rf, :], ngf_ref[rf, :], kf_ref[rf, :], iv_ref[rf, :],
                                       bigf_ref, mskf_ref, sf_ref[...], C, True)
                sf_ref[...] = s_f
                o_b, s_b = _scan_chunk(q_ref[rb, :], ngb_ref[rb, :], kb_ref[rb, :], iv_ref[rb, :],
                                       bigb_ref, mskb_ref, sb_ref[...], C, False)
                sb_ref[...] = s_b
                put(rf, rb, o_f, o_b)
                return carry2

            lax.fori_loop(0, U, redo, 0)

        return carry

    lax.fori_loop(0, n // U, body, 0)

    if with_out:
        gw = gw_ref[...]
        R = min(L, 512)

        def fin(r, carry):
            rows = pl.ds(pl.multiple_of(r * R, R), R)
            o = of_scr[rows, :] + ob_scr[rows, :]
            o = o * lax.rsqrt(jnp.mean(o * o, axis=-1, keepdims=True) + EPS) * gw
            ga = ga_ref[rows, :]
            o_ref[rows, :] = (o * (ga * jax.nn.sigmoid(ga))).astype(o_ref.dtype)
            return carry

        lax.fori_loop(0, L // R, fin, 0)


def _hgrn(px, g_norm_w, s0f, s0b, with_out):
    B, L, _ = px.shape
    C = SCAN_CHUNK
    H = A_HEADS
    bigf, mskf = _scan_consts(C, True)
    bigb, mskb = _scan_consts(C, False)
    bigf, bigb = jnp.asarray(bigf, BF16), jnp.asarray(bigb, BF16)
    mskf, mskb = jnp.asarray(mskf, F32), jnp.asarray(mskb, F32)

    def col(j):
        return pl.BlockSpec((None, L, LANES), lambda b, h, j=j: (b, 0, j * H + h))

    def const(a):
        return pl.BlockSpec(a.shape, lambda b, h, nd=a.ndim: (0,) * nd)

    st_spec = pl.BlockSpec((None, None, HEAD_DIM, HEAD_DIM), lambda b, h: (b, h, 0, 0))
    st_shape = jax.ShapeDtypeStruct((B, H, HEAD_DIM, HEAD_DIM), F32)
    in_specs = [col(j) for j in range(6)]
    args = [px] * 6
    if with_out:
        in_specs += [col(6), pl.BlockSpec((1, LANES), lambda b, h: (0, h))]
        args += [px, g_norm_w.reshape(1, D_A)]
    in_specs += [const(bigf), const(mskf), const(bigb), const(mskb), st_spec, st_spec]
    args += [bigf, mskf, bigb, mskb, s0f, s0b]
    out_specs = [st_spec, st_spec]
    out_shape = [st_shape, st_shape]
    scratch = [pltpu.VMEM((HEAD_DIM, HEAD_DIM), F32), pltpu.VMEM((HEAD_DIM, HEAD_DIM), F32)]
    if with_out:
        out_specs = [pl.BlockSpec((None, L, LANES), lambda b, h: (b, 0, h))] + out_specs
        out_shape = [jax.ShapeDtypeStruct((B, L, D_A), BF16)] + out_shape
        scratch += [pltpu.VMEM((L, LANES), F32), pltpu.VMEM((L, LANES), F32)]
    res = pl.pallas_call(
        functools.partial(_hgrn_kernel, L=L, C=C, U=min(SCAN_UNROLL, L // C), with_out=with_out),
        grid=(B, H),
        in_specs=in_specs,
        out_specs=out_specs,
        out_shape=out_shape,
        scratch_shapes=scratch,
        compiler_params=_cparams("arbitrary", "arbitrary"),
        name="hgrn2",
    )(*args)
    if with_out:
        return res[0], res[1], res[2]
    return None, res[0], res[1]


def _conv3(ref, r0, R, L, cp):
    T = 2 * SUBLANES
    cur = ref[pl.ds(r0, R), :].astype(F32)
    prev_tile = ref[pl.ds(pl.multiple_of(jnp.maximum(r0 - T, 0), T), T), :].astype(F32)
    next_tile = ref[pl.ds(pl.multiple_of(jnp.minimum(r0 + R, L - T), T), T), :].astype(F32)
    prow = jnp.where(r0 > 0, prev_tile[T - 1:T, :], 0.0)
    nrow = jnp.where(r0 + R < L, next_tile[0:1, :], 0.0)
    rid = lax.broadcasted_iota(jnp.int32, (R, LANES), 0)
    up = jnp.where(rid == 0, prow, pltpu.roll(cur, 1, 0))
    dn = jnp.where(rid == R - 1, nrow, pltpu.roll(cur, R - 1, 0))
    return cp[3:4, :] + up * cp[0:1, :] + cur * cp[1:2, :] + dn * cp[2:3, :]


def _hyena_pre_kernel(x0_ref, x1_ref, v_ref, gb_ref, c0_ref, c1_ref, c2_ref, vin_ref, m_ref, *, L):
    R = min(L, 256)
    c0 = c0_ref[...]
    c1 = c1_ref[...]
    c2 = c2_ref[...]

    def body(r, carry):
        r0 = pl.multiple_of(r * R, R)
        rows = pl.ds(r0, R)
        x0 = _conv3(x0_ref, r0, R, L, c0)
        x1 = _conv3(x1_ref, r0, R, L, c1)
        vv = _conv3(v_ref, r0, R, L, c2)
        gb = gb_ref[rows, :].astype(F32)
        vin_ref[rows, :] = vv * x1
        m_ref[rows, :] = (x0 * (gb * jax.nn.sigmoid(gb))).astype(m_ref.dtype)
        return carry

    lax.fori_loop(0, L // R, body, 0)


def _hyena_pre(px, conv_par):
    B, L, _ = px.shape
    nb = D_B // LANES

    def col(j):
        return pl.BlockSpec((None, L, LANES), lambda b, c, j=j: (b, 0, j * nb + c))

    def cpar(j):
        return pl.BlockSpec((SUBLANES, LANES), lambda b, c, j=j: (0, j * nb + c))

    out_spec = pl.BlockSpec((None, L, LANES), lambda b, c: (b, 0, c))
    out_shape = jax.ShapeDtypeStruct((B, L, D_B), F32)
    return pl.pallas_call(
        functools.partial(_hyena_pre_kernel, L=L),
        grid=(B, nb),
        in_specs=[col(0), col(1), col(2), col(3), cpar(0), cpar(1), cpar(2)],
        out_specs=[out_spec, out_spec],
        out_shape=[out_shape, jax.ShapeDtypeStruct((B, L, D_B), BF16)],
        compiler_params=_cparams("arbitrary", "arbitrary"),
        name="hyena_pre",
    )(px, px, px, px, conv_par, conv_par, conv_par)


def _pos_rows(L, order):
    f32 = np.float32
    t = np.linspace(0.0, 1.0, L, dtype=f32)[:, None]
    w = (f32(2.0 * math.pi) * np.arange(L, dtype=f32)[:, None]) / f32(L)
    f = np.linspace(1e-4, HY_BANDS - 1, HY_BANDS, dtype=f32)[None, :]
    z = np.concatenate([t, np.cos(f * w), -np.sin(f * w)], axis=-1).astype(f32)
    lag = np.arange(2 * L)
    src = np.where(lag < L, lag, 2 * L - lag) % L
    fwd = (lag < L).astype(f32)
    valid = (lag != L).astype(f32)
    rows = np.concatenate([z[src], fwd[:, None], valid[:, None],
                           np.zeros((2 * L, HY_WIDTH - HY_EMB - 2), f32)], axis=-1)
    return jnp.asarray(rows[order])


def _filter_kernel(z_ref, zt_ref, w1_ref, w2_ref, w3_ref, vec_ref, w4h_ref, w4l_ref, dl_ref, k_ref,
                   *, n_rows):
    R = min(n_rows, 512)
    S = min(R, LANES)
    dl = dl_ref[...]
    vec = vec_ref[...]
    b1, b2, b3, fr = vec[:, 0:1], vec[:, 1:2], vec[:, 2:3], vec[:, 3:4]

    def body(r, acc):
        h = jnp.sin(fr * (jnp.dot(w1_ref[...], zt_ref[r], precision=HIGHEST,
                                  preferred_element_type=F32) + b1))
        h = jnp.sin(fr * (jnp.dot(w2_ref[...], h, precision=HIGHEST, preferred_element_type=F32) + b2))
        h = jnp.sin(fr * (jnp.dot(w3_ref[...], h, precision=HIGHEST, preferred_element_type=F32) + b3))
        for s0 in range(0, R, S):
            rows = pl.ds(pl.multiple_of(r * R + s0, S), S)
            z = z_ref[rows, :]
            hs = h[:, s0:s0 + S].T
            hh = hs.astype(BF16)
            hl = (hs - hh.astype(F32)).astype(BF16)
            hfb = (jnp.dot(hh, w4h_ref[...], preferred_element_type=F32)
                   + jnp.dot(hl, w4h_ref[...], preferred_element_type=F32)
                   + jnp.dot(hh, w4l_ref[...], preferred_element_type=F32))
            fwd = z[:, HY_EMB:HY_EMB + 1]
            valid = z[:, HY_EMB + 1:HY_EMB + 2]
            k = (jnp.where(fwd > 0.5, hfb[:, :D_B], hfb[:, D_B:])
                 * jnp.exp(-z[:, 0:1] * dl) * valid)
            k_ref[rows, :] = k
            acc = acc + jnp.sum(jnp.abs(k), axis=0, keepdims=True)
        return acc

    tot = lax.fori_loop(0, n_rows // R, body, jnp.zeros((1, D_B), F32))

    def scale(r, carry):
        rows = pl.ds(pl.multiple_of(r * R, R), R)
        k_ref[rows, :] = k_ref[rows, :] / tot
        return carry

    lax.fori_loop(0, n_rows // R, scale, 0)


def _hyena_filter(zrows, w1, b1, freq, w2, b2, w3, b3, w4):
    n_rows = zrows.shape[0]
    R = min(n_rows, 512)
    zt = zrows.reshape(n_rows // R, R, HY_WIDTH).transpose(0, 2, 1)
    w1t = jnp.concatenate([w1, jnp.zeros((HY_WIDTH - HY_EMB, HY_WIDTH), F32)], axis=0).T
    vec = jnp.concatenate([b1[:, None], b2[:, None], b3[:, None], freq[:, None],
                           jnp.zeros((HY_WIDTH, SUBLANES - 4), F32)], axis=1)
    deltas = jnp.abs(jnp.linspace(HY_MIN_DECAY, HY_MAX_DECAY, D_B, dtype=F32))[None, :]
    w4h, w4l = _hi_lo(w4)

    def const(shape):
        return pl.BlockSpec(shape, lambda c, nd=len(shape): (0,) * nd)

    return pl.pallas_call(
        functools.partial(_filter_kernel, n_rows=n_rows),
        grid=(1,),
        in_specs=[const((n_rows, HY_WIDTH)), const(zt.shape), const((HY_WIDTH, HY_WIDTH)),
                  const((HY_WIDTH, HY_WIDTH)), const((HY_WIDTH, HY_WIDTH)), const((HY_WIDTH, SUBLANES)),
                  const((HY_WIDTH, 2 * D_B)), const((HY_WIDTH, 2 * D_B)), const((1, D_B))],
        out_specs=const((n_rows, D_B)),
        out_shape=jax.ShapeDtypeStruct((n_rows, D_B), F32),
        compiler_params=_cparams("arbitrary"),
        name="hyena_filter",
    )(zrows, zt, w1t, w2.T, w3.T, vec, w4h, w4l, deltas)


def _cplx_block(gr, gi):
    return np.block([[gr, -gi], [gi, gr]])


def _interleave(n):
    return np.stack([np.arange(n), n + np.arange(n)], axis=1).reshape(-1)


def _hi_lo(a):
    a = jnp.asarray(a, F32)
    hi = a.astype(BF16)
    return hi, (a - hi.astype(F32)).astype(BF16)


def _dft_consts(L):
    N = 2 * L
    n2 = FFT_MINOR
    n1 = N // n2
    half = n1 // 2
    f1 = np.arange(n1)
    t2 = np.arange(n2)
    t1 = np.arange(n1)
    tt = n2 * t1[None, None, :] + t2[:, None, None]
    ang = -2.0 * np.pi * ((f1[None, :, None] * tt) % N) / N
    gr, gi = np.cos(ang), np.sin(ang)
    m1_data = np.stack([_cplx_block(gr[j][:, :half], gi[j][:, :half]) for j in range(n2)])
    m1_filt = np.concatenate([gr, gi], axis=1)
    m1_inv = np.stack([_cplx_block(gr[j][:, :half].T, -gi[j][:, :half].T) for j in range(n2)]) / N
    ang2 = -2.0 * np.pi * ((t2[:, None] * t2[None, :]) % n2) / n2
    f2 = _cplx_block(np.cos(ang2), np.sin(ang2))
    f2_inv = _cplx_block(np.cos(ang2), -np.sin(ang2))
    il1 = _interleave(n1)
    il2 = _interleave(n2)
    as_bf16 = lambda a: jnp.asarray(a, F32).astype(BF16)
    return dict(n1=n1, n2=n2, m1_filt=_hi_lo(m1_filt), f2=_hi_lo(f2[None]),
                m1_data=as_bf16(m1_data[:, il1, :]), f2_packed=as_bf16(f2[:, il2]),
                f2_inv_packed=as_bf16(f2_inv[il2, :]), m1_inv=as_bf16(m1_inv[:, :, il1]))


def _dft_single_consts(L):
    N = 2 * L
    f = np.arange(N)
    ang = -2.0 * np.pi * ((f[:, None] * f[None, :]) % N) / N
    gr, gi = np.cos(ang), np.sin(ang)
    fwd = _cplx_block(gr[:, :L], gi[:, :L])
    filt = np.concatenate([gr, gi], axis=0)
    inv = _cplx_block(gr[:L, :], -gi[:L, :]) / N
    return dict(fwd=_hi_lo(fwd[None]), filt=_hi_lo(filt[None]), inv=_hi_lo(inv[None]))


def _bmm_kernel(mh_ref, ml_ref, x_ref, o_ref, *, gblk, shared, precise):
    for j in range(gblk):
        jm = 0 if shared else j
        x = x_ref[j]
        xh = x.astype(BF16)
        acc = jnp.dot(mh_ref[jm], xh, preferred_element_type=F32)
        if precise:
            xl = (x - xh.astype(F32)).astype(BF16)
            acc = (acc + jnp.dot(mh_ref[jm], xl, preferred_element_type=F32)
                   + jnp.dot(ml_ref[jm], xh, preferred_element_type=F32))
        o_ref[j] = acc.astype(o_ref.dtype)


def _bmm_left(m, x, gblk, precise, out_dtype):
    mh, ml = m
    P, G, K, C = x.shape
    R = mh.shape[1]
    shared = mh.shape[0] == 1
    gblk = min(gblk, G)
    m_spec = (pl.BlockSpec((1, R, K), lambda g, p: (0, 0, 0)) if shared
              else pl.BlockSpec((gblk, R, K), lambda g, p: (g, 0, 0)))
    return pl.pallas_call(
        functools.partial(_bmm_kernel, gblk=gblk, shared=shared, precise=precise),
        grid=(G // gblk, P),
        in_specs=[m_spec, m_spec, pl.BlockSpec((None, gblk, K, C), lambda g, p: (p, g, 0, 0))],
        out_specs=pl.BlockSpec((None, gblk, R, C), lambda g, p: (p, g, 0, 0)),
        out_shape=jax.ShapeDtypeStruct((P, G, R, C), out_dtype),
        compiler_params=_cparams("arbitrary", "arbitrary"),
        name="dft_stage",
    )(mh, ml, x)


def _mid_kernel(fa_ref, fb_ref, kf_ref, hb_ref, d_ref, o_ref, *, gblk, nf):
    fa = fa_ref[...]
    fb = fb_ref[...]
    hb = hb_ref[...]
    for j in range(gblk):
        xx = jnp.dot(fa, d_ref[j].astype(BF16), preferred_element_type=F32)
        xr, xi = xx[:nf], xx[nf:]
        kr, ki = kf_ref[j, :nf] + hb, kf_ref[j, nf:]
        yy = jnp.concatenate([xr * kr - xi * ki, xr * ki + xi * kr], axis=0)
        o_ref[j] = jnp.dot(fb, yy.astype(BF16), preferred_element_type=F32).astype(o_ref.dtype)


def _spectral_mid(fa, fb, kf, hb, d, gblk, out_dtype):
    P, G, K, C = d.shape
    nf2 = fa.shape[0]
    Ko = fb.shape[0]
    gblk = min(gblk, G)
    return pl.pallas_call(
        functools.partial(_mid_kernel, gblk=gblk, nf=nf2 // 2),
        grid=(G // gblk, P),
        in_specs=[pl.BlockSpec(fa.shape, lambda g, p: (0, 0)),
                  pl.BlockSpec(fb.shape, lambda g, p: (0, 0)),
                  pl.BlockSpec((gblk, nf2, C), lambda g, p: (g, 0, 0)),
                  pl.BlockSpec((1, C), lambda g, p: (0, 0)),
                  pl.BlockSpec((None, gblk, K, C), lambda g, p: (p, g, 0, 0))],
        out_specs=pl.BlockSpec((None, gblk, Ko, C), lambda g, p: (p, g, 0, 0)),
        out_shape=jax.ShapeDtypeStruct((P, G, Ko, C), out_dtype),
        compiler_params=_cparams("arbitrary", "arbitrary"),
        name="dft_mid",
    )(fa, fb, kf, hb, d)


def _filter_first_kernel(mh_ref, ml_ref, x_ref, o_ref, *, gblk, n1):
    for j in range(gblk):
        x = x_ref[j]
        xh = x.astype(BF16)
        xl = (x - xh.astype(F32)).astype(BF16)
        acc = (jnp.dot(mh_ref[j], xh, preferred_element_type=F32)
               + jnp.dot(mh_ref[j], xl, preferred_element_type=F32)
               + jnp.dot(ml_ref[j], xh, preferred_element_type=F32))
        o_ref[:, 0, j, :] = acc[:n1]
        o_ref[:, 1, j, :] = acc[n1:]


def _filter_first_stage(m, x, n1, n2):
    mh, ml = m
    C = x.shape[-1]
    gblk = min(DFT_SLAB, n2)
    m_spec = pl.BlockSpec((gblk,) + mh.shape[1:], lambda g: (g, 0, 0))
    return pl.pallas_call(
        functools.partial(_filter_first_kernel, gblk=gblk, n1=n1),
        grid=(n2 // gblk,),
        in_specs=[m_spec, m_spec, pl.BlockSpec((gblk, n1, C), lambda g: (g, 0, 0))],
        out_specs=pl.BlockSpec((n1, 2, gblk, C), lambda g: (0, 0, g, 0)),
        out_shape=jax.ShapeDtypeStruct((n1, 2, n2, C), F32),
        compiler_params=_cparams("arbitrary"),
        name="dft_filter_first",
    )(mh, ml, x)


def _first_stage_kernel(m_ref, x_ref, o_ref, *, gblk):
    for j in range(gblk):
        x = jnp.concatenate([x_ref[0, :, j, :], x_ref[1, :, j, :]], axis=0).astype(BF16)
        acc = jnp.dot(m_ref[j], x, preferred_element_type=F32)
        o_ref[:, j, :] = pltpu.bitcast(acc.astype(BF16), jnp.uint32)


def _first_stage(m, x, n1, n2):
    _, P, half, _, C = x.shape
    gblk = min(DFT_SLAB, n2)
    return pl.pallas_call(
        functools.partial(_first_stage_kernel, gblk=gblk),
        grid=(n2 // gblk, P),
        in_specs=[pl.BlockSpec((gblk,) + m.shape[1:], lambda g, p: (g, 0, 0)),
                  pl.BlockSpec((2, None, half, gblk, C), lambda g, p: (0, p, 0, g, 0))],
        out_specs=pl.BlockSpec((None, n1, gblk, C), lambda g, p: (p, 0, g, 0)),
        out_shape=jax.ShapeDtypeStruct((P, n1, n2, C), jnp.uint32),
        compiler_params=_cparams("arbitrary", "arbitrary"),
        name="dft_first",
    )(m, x)


def _packed_mid_kernel(fa_ref, fb_ref, kf_ref, hb_ref, d_ref, o_ref, *, gblk, nf):
    fa = fa_ref[...]
    fb = fb_ref[...]
    hb = hb_ref[...]
    for j in range(gblk):
        xx = jnp.dot(fa, pltpu.bitcast(d_ref[j], BF16), preferred_element_type=F32)
        xr, xi = xx[:nf], xx[nf:]
        kr, ki = kf_ref[j, :nf] + hb, kf_ref[j, nf:]
        yy = jnp.concatenate([xr * kr - xi * ki, xr * ki + xi * kr], axis=0)
        zz = jnp.dot(fb, yy.astype(BF16), preferred_element_type=F32)
        o_ref[j] = pltpu.bitcast(zz.astype(BF16), jnp.uint32)


def _packed_mid(fa, fb, kf, hb, d, gblk):
    P, G, K, C = d.shape
    gblk = min(gblk, G)
    return pl.pallas_call(
        functools.partial(_packed_mid_kernel, gblk=gblk, nf=K),
        grid=(G // gblk, P),
        in_specs=[pl.BlockSpec(fa.shape, lambda g, p: (0, 0)),
                  pl.BlockSpec(fb.shape, lambda g, p: (0, 0)),
                  pl.BlockSpec((gblk, 2 * K, C), lambda g, p: (g, 0, 0)),
                  pl.BlockSpec((1, C), lambda g, p: (0, 0)),
                  pl.BlockSpec((None, gblk, K, C), lambda g, p: (p, g, 0, 0))],
        out_specs=pl.BlockSpec((None, gblk, K, C), lambda g, p: (p, g, 0, 0)),
        out_shape=jax.ShapeDtypeStruct((P, G, K, C), jnp.uint32),
        compiler_params=_cparams("arbitrary", "arbitrary"),
        name="dft_mid",
    )(fa, fb, kf, hb, d)


def _last_stage_kernel(m_ref, z_ref, y_ref, *, gblk, half):
    for j in range(gblk):
        z = pltpu.bitcast(z_ref[:, j, :], BF16)
        y = jnp.dot(m_ref[j], z, preferred_element_type=F32)
        y_ref[0, :, j, :] = y[:half]
        y_ref[1, :, j, :] = y[half:]


def _last_stage(m, z, half):
    P, n1, n2, C = z.shape
    gblk = min(DFT_SLAB, n2)
    return pl.pallas_call(
        functools.partial(_last_stage_kernel, gblk=gblk, half=half),
        grid=(n2 // gblk, P),
        in_specs=[pl.BlockSpec((gblk,) + m.shape[1:], lambda g, p: (g, 0, 0)),
                  pl.BlockSpec((None, n1, gblk, C), lambda g, p: (p, 0, g, 0))],
        out_specs=pl.BlockSpec((2, None, half, gblk, C), lambda g, p: (0, p, 0, g, 0)),
        out_shape=jax.ShapeDtypeStruct((2, P, half, n2, C), F32),
        compiler_params=_cparams("arbitrary", "arbitrary"),
        name="dft_last",
    )(m, z)


def _long_conv_two_stage(vin, filt_rows, hy_bias, dc):
    B, L, C = vin.shape
    n1, n2 = dc["n1"], dc["n2"]
    half = n1 // 2
    P = B // 2
    ka = _filter_first_stage(dc["m1_filt"], filt_rows.reshape(n2, n1, C), n1, n2)
    kf = _bmm_left(dc["f2"], ka.reshape(1, n1, 2 * n2, C), 8, True, F32)[0]
    a = _first_stage(dc["m1_data"], vin.reshape(2, P, half, n2, C), n1, n2)
    z = _packed_mid(dc["f2_packed"], dc["f2_inv_packed"], kf, hy_bias, a, 16)
    y = _last_stage(dc["m1_inv"], z, half)
    return y.reshape(B, L, C)


def _long_conv_single(vin, filt, hy_bias, dc):
    B, L, C = vin.shape
    P = B // 2
    kf = _bmm_left(dc["filt"], filt.reshape(1, 1, 2 * L, C), 1, True, F32)[0]
    d = vin.reshape(2, P, L, C).transpose(1, 0, 2, 3).reshape(P, 1, 2 * L, C)
    y = _spectral_mid(dc["fwd"][0][0], dc["inv"][0][0], kf, hy_bias, d, 1, F32)
    return y.reshape(P, 2, L, C).transpose(1, 0, 2, 3).reshape(B, L, C)


def _outproj_kernel(*refs, final):
    if final:
        oa_ref, y_ref, m_ref, x_ref, gt_ref, w_ref, fw_ref, o_ref = refs
    else:
        oa_ref, y_ref, m_ref, x_ref, gt_ref, w_ref, o_ref = refs
    ob = y_ref[...] * m_ref[...]
    r = (jnp.dot(oa_ref[...], w_ref[0:D_A, :], preferred_element_type=F32)
         + jnp.dot(ob.astype(BF16), w_ref[D_A:D_A + D_B, :], preferred_element_type=F32))
    xn = x_ref[...] + gt_ref[...] * r
    if final:
        ms = jnp.mean(xn * xn, axis=-1, keepdims=True)
        xn = xn * lax.rsqrt(ms + EPS) * fw_ref[...]
    o_ref[...] = xn


def _out_proj(oa, y, m, x, gt, w_bf16, final_w):
    B, L, D = x.shape
    tm = min(L, 1024)
    final = final_w is not None
    half = lambda: pl.BlockSpec((None, tm, D_B), lambda b, i: (b, i, 0))
    in_specs = [half(), half(), half(),
                pl.BlockSpec((None, tm, D), lambda b, i: (b, i, 0)),
                pl.BlockSpec((None, 1, D), lambda b, i: (b, 0, 0)),
                pl.BlockSpec((D, D), lambda b, i: (0, 0))]
    args = [oa, y, m, x, gt, w_bf16]
    if final:
        in_specs.append(pl.BlockSpec((1, D), lambda b, i: (0, 0)))
        args.append(final_w.reshape(1, D))
    return pl.pallas_call(
        functools.partial(_outproj_kernel, final=final),
        grid=(B, L // tm),
        in_specs=in_specs,
        out_specs=pl.BlockSpec((None, tm, D), lambda b, i: (b, i, 0)),
        out_shape=jax.ShapeDtypeStruct((B, L, D), F32),
        compiler_params=_cparams("arbitrary", "arbitrary"),
        name="out_proj",
    )(*args)


def _gate_params(lb):
    rows = []
    for d in range(2):
        rows += [jnp.log(lb[d]), jnp.log1p(-lb[d]), 1.0 - lb[d]]
    rows += [jnp.zeros_like(lb[0])] * (SUBLANES - len(rows))
    return jnp.stack(rows, axis=0)


def kernel(x, c, ctx, c_ctx, norm_w, w_ada, b_ada, w_in, w_out, lb_logits, g_norm_w,
           conv_w, conv_b, hy_w1, hy_b1, hy_freq, hy_w2, hy_b2, hy_w3, hy_b3, hy_w4,
           hy_bias, final_norm_w):
    B, L_lat, D = x.shape
    L_ctx = ctx.shape[1]
    p_lb = jax.nn.softmax(lb_logits.astype(F32), axis=0)
    lbs = jnp.cumsum(p_lb, axis=0)
    lbs = lbs - lbs[0:1]

    n_rows = 2 * SUBLANES
    cc = jnp.zeros((n_rows, D), F32).at[:B].set(c).at[B].set(c_ctx)
    mod = _modulation(cc, w_ada, b_ada)

    dc_lat = _dft_consts(L_lat)
    dc_ctx = _dft_single_consts(L_ctx)
    n1, n2 = dc_lat["n1"], dc_lat["n2"]
    order_lat = (np.arange(n2)[:, None] + n2 * np.arange(n1)[None, :]).reshape(-1)
    z_lat = _pos_rows(L_lat, order_lat)
    z_ctx = _pos_rows(L_ctx, np.arange(2 * L_ctx))
    zero_state = jnp.zeros((B, A_HEADS, HEAD_DIM, HEAD_DIM), F32)

    for l in range(DEPTH):
        last = l == DEPTH - 1
        sh_x, sc_x, gt_x = [mod[l, :B, j * D:(j + 1) * D].reshape(B, 1, D) for j in range(3)]
        sh_c, sc_c, gt_c = [mod[l, B, j * D:(j + 1) * D].reshape(1, 1, D) for j in range(3)]
        w_in_l = w_in[l].astype(BF16)
        w_out_l = w_out[l].astype(BF16)
        gate_par = _gate_params(lbs[l])
        conv_par = jnp.concatenate([conv_w[l], conv_b[l][None],
                                    jnp.zeros((SUBLANES - 4, 3 * D_B), F32)], axis=0)
        filt_args = (hy_w1[l], hy_b1[l], hy_freq[l], hy_w2[l], hy_b2[l], hy_w3[l], hy_b3[l], hy_w4[l])

        flat = lambda a: a.reshape(1, B * L_ctx, a.shape[-1])
        unflat = lambda a: a.reshape(B, L_ctx, a.shape[-1])
        if last:
            pc, _ = _in_proj(flat(ctx), norm_w[l], sc_c, sh_c, gate_par, w_in_l[:, :4 * D_A])
            _, s_f, s_b = _hgrn(unflat(pc), g_norm_w[l], zero_state, zero_state, with_out=False)
        else:
            pc, pc_hy = _in_proj(flat(ctx), norm_w[l], sc_c, sh_c, gate_par, w_in_l)
            oa_c, s_f, s_b = _hgrn(unflat(pc), g_norm_w[l], zero_state, zero_state, with_out=True)
            vin_c, m_c = _hyena_pre(unflat(pc_hy), conv_par)
            filt_c = _hyena_filter(z_ctx, *filt_args)
            y_c = _long_conv_single(vin_c, filt_c, hy_bias[l].reshape(1, D_B), dc_ctx)
            ctx = unflat(_out_proj(flat(oa_c), flat(y_c), flat(m_c), flat(ctx), gt_c, w_out_l, None))

        px, px_hy = _in_proj(x, norm_w[l], sc_x, sh_x, gate_par, w_in_l)
        oa, _, _ = _hgrn(px, g_norm_w[l], s_f, s_b, with_out=True)
        vin, m = _hyena_pre(px_hy, conv_par)
        filt_x = _hyena_filter(z_lat, *filt_args)
        y = _long_conv_two_stage(vin, filt_x, hy_bias[l].reshape(1, D_B), dc_lat)
        x = _out_proj(oa, y, m, x, gt_x, w_out_l, final_norm_w if last else None)

    return x
```

```python
import functools
import math

import numpy as np
import jax
import jax.numpy as jnp
from jax import lax
from jax.experimental import pallas as pl
from jax.experimental.pallas import tpu as pltpu

F32 = jnp.float32
BF16 = jnp.bfloat16
HIGHEST = lax.Precision.HIGHEST

D_MODEL = 1024
DEPTH = 2
D_A = 512
D_B = 512
A_HEADS = 4
HEAD_DIM = 128
N_IN = 5 * D_A + 4 * D_B
HY_EMB = 33
HY_BANDS = 16
HY_WIDTH = 64
HY_MIN_DECAY = math.log(1e-2) / 1.5
HY_MAX_DECAY = math.log(1e-2) / 0.3
EPS = 1e-6

LANES = 128
SUBLANES = 8
SCAN_CHUNK = 64
SCAN_UNROLL = 8
SCAN_GUARD = 80.0
DFT_SLAB = 64
FFT_MINOR = 128
VMEM_LIMIT = 56 * 1024 * 1024


def _cparams(*sem):
    return pltpu.CompilerParams(dimension_semantics=sem, vmem_limit_bytes=VMEM_LIMIT)


def _mod_kernel(c_ref, w_ref, b_ref, o_ref):
    cc = c_ref[...]
    s = cc * jax.nn.sigmoid(cc)
    o_ref[...] = jnp.dot(s, w_ref[...], precision=HIGHEST,
                         preferred_element_type=F32) + b_ref[...]


def _modulation(cc, w_ada, b_ada):
    R, D = cc.shape
    N = w_ada.shape[-1]
    tn = 768
    return pl.pallas_call(
        _mod_kernel,
        grid=(DEPTH, N // tn),
        in_specs=[
            pl.BlockSpec((R, D), lambda l, j: (0, 0)),
            pl.BlockSpec((None, D, tn), lambda l, j: (l, 0, j)),
            pl.BlockSpec((None, 1, tn), lambda l, j: (l, 0, j)),
        ],
        out_specs=pl.BlockSpec((None, R, tn), lambda l, j: (l, 0, j)),
        out_shape=jax.ShapeDtypeStruct((DEPTH, R, N), F32),
        compiler_params=_cparams("arbitrary", "arbitrary"),
        name="modulation",
    )(cc, w_ada, b_ada.reshape(DEPTH, 1, N))


def _gates(f, par):
    loglb = par[0:1, :]
    log1mlb = par[1:2, :]
    onemlb = par[2:3, :]
    s1 = jnp.log(1.0 + jnp.exp(-jnp.abs(f)))
    b = log1mlb + (jnp.minimum(f, 0.0) - s1)
    ng = -jnp.maximum(loglb, b) - jnp.log(1.0 + jnp.exp(-jnp.abs(loglb - b)))
    k = onemlb * jnp.exp(-(jnp.maximum(f, 0.0) + s1))
    return ng, k


N_SCAN_GROUPS = 5


def _inproj_kernel(x_ref, nw_ref, sc_ref, sh_ref, gp_ref, w_ref, oa_ref, *rest):
    ob_ref = rest[0] if len(rest) == 2 else None
    hx_ref = rest[-1]
    x = x_ref[...]
    ms = jnp.mean(x * x, axis=-1, keepdims=True)
    y = x * lax.rsqrt(ms + EPS) * nw_ref[...]
    hx_ref[...] = (y * (1.0 + sc_ref[...]) + sh_ref[...]).astype(BF16)
    gp = gp_ref[...]
    w = D_A
    for g in range(w_ref.shape[-1] // w):
        r = jnp.dot(hx_ref[...], w_ref[:, g * w:(g + 1) * w], preferred_element_type=F32)
        if g == 0:
            oa_ref[:, 0:w] = r * jax.nn.sigmoid(r)
        elif g in (1, 2):
            ng, k = _gates(r, gp[3 * (g - 1):3 * g])
            oa_ref[:, (2 * g - 1) * w:2 * g * w] = ng
            oa_ref[:, 2 * g * w:(2 * g + 1) * w] = k
        elif g < N_SCAN_GROUPS:
            oa_ref[:, (g + 2) * w:(g + 3) * w] = r
        else:
            ob_ref[:, (g - N_SCAN_GROUPS) * w:(g - N_SCAN_GROUPS + 1) * w] = r.astype(ob_ref.dtype)


def _in_proj(x, norm_w, sc, sh, gate_par, w_bf16):
    B, L, D = x.shape
    N = w_bf16.shape[1]
    n_a = min(N // D_A, N_SCAN_GROUPS)
    Na = (n_a + 2) * D_A
    Nb = N - n_a * D_A
    tm = min(L, 512)
    out_specs = [pl.BlockSpec((None, tm, Na), lambda b, i: (b, i, 0))]
    out_shape = [jax.ShapeDtypeStruct((B, L, Na), F32)]
    if Nb:
        out_specs.append(pl.BlockSpec((None, tm, Nb), lambda b, i: (b, i, 0)))
        out_shape.append(jax.ShapeDtypeStruct((B, L, Nb), BF16))
    res = pl.pallas_call(
        _inproj_kernel,
        grid=(B, L // tm),
        in_specs=[
            pl.BlockSpec((None, tm, D), lambda b, i: (b, i, 0)),
            pl.BlockSpec((1, D), lambda b, i: (0, 0)),
            pl.BlockSpec((None, 1, D), lambda b, i: (b, 0, 0)),
            pl.BlockSpec((None, 1, D), lambda b, i: (b, 0, 0)),
            pl.BlockSpec((SUBLANES, D_A), lambda b, i: (0, 0)),
            pl.BlockSpec((D, N), lambda b, i: (0, 0)),
        ],
        out_specs=out_specs,
        out_shape=out_shape,
        scratch_shapes=[pltpu.VMEM((tm, D), BF16)],
        compiler_params=_cparams("arbitrary", "arbitrary"),
        name="in_proj",
    )(x, norm_w.reshape(1, D), sc, sh, gate_par, w_bf16)
    return (res[0], res[1]) if Nb else (res[0], None)


def _scan_consts(C, forward):
    idx = np.arange(C)
    i = idx[:, None]
    t = idx[None, :]
    if forward:
        mats = [t <= i, t > i]
    else:
        mats = [t >= i, t < i]
    masks = [i == t]
    h = 1
    while h < C:
        P = 2 * h
        p = i % P
        m = i - p + h
        upper = p >= h
        same = (i // P) == (t // P)
        if forward:
            mats.append(np.where(upper, (t >= m) & (t <= i), (t >= i + 1) & (t <= m - 1)))
            masks.append(same & upper & ((t % P) < h))
        else:
            mats.append(np.where(upper, (t >= m) & (t <= i - 1), (t >= i) & (t <= m - 1)))
            masks.append(same & (~upper) & ((t % P) >= h))
        h = P
    big = np.concatenate([m_.astype(np.float32) for m_ in mats], axis=0)
    big = np.concatenate([big, big, big], axis=1)
    msk = np.stack([m_.astype(np.float32) for m_ in masks], axis=0)
    return big, msk


def _dot_nt(a, b):
    return lax.dot_general(a, b, (((1,), (1,)), ((), ())), preferred_element_type=F32)


def _dot_tn(a, b):
    return lax.dot_general(a, b, (((0,), (0,)), ((), ())), preferred_element_type=F32)


def _split3(a):
    hi = a.astype(BF16)
    r1 = a - hi.astype(F32)
    mid = r1.astype(BF16)
    lo = (r1 - mid.astype(F32)).astype(BF16)
    return hi, mid, lo


def _cumulative(tri_ref, ng):
    return jnp.dot(tri_ref[...], jnp.concatenate(_split3(ng), axis=0), preferred_element_type=F32)


def _fast_stage_sums(q, ng, k, v, tri_ref):
    return dict(q=q, k=k, vb=v.astype(BF16), cum=_cumulative(tri_ref, ng))


def _fast_stage_scores(s, C, forward):
    h = C // 2
    q, k, cum = s["q"], s["k"], s["cum"]
    if forward:
        early, late = slice(0, h), slice(h, C)
        ref, tot = cum[h - 1:h, :], cum[C - 1:C, :]
    else:
        early, late = slice(h, C), slice(0, h)
        ref, tot = cum[h:h + 1, :], cum[0:1, :]
    qd = (q * jnp.exp(-cum)).astype(BF16)
    dl = cum - ref
    k_early = (k[early] * jnp.exp(cum[early])).astype(BF16)
    q_late = (q[late] * jnp.exp(-dl[late])).astype(BF16)
    k_all = (k * jnp.exp(dl)).astype(BF16)
    kd = (k * jnp.exp(cum - tot)).astype(BF16)
    return dict(vb=s["vb"], qd=qd, kd=kd, e_all=jnp.exp(-tot),
                guard=jnp.maximum(cum[early], dl[late]),
                s_early=_dot_nt(qd[early], k_early), s_late=_dot_nt(q_late, k_all))


def _fast_stage_intra(s, C, forward):
    h = C // 2
    early = slice(0, h) if forward else slice(h, C)
    ri = lax.broadcasted_iota(jnp.int32, (h, h), 0)
    ci = lax.broadcasted_iota(jnp.int32, (h, h), 1)
    rl = lax.broadcasted_iota(jnp.int32, (h, C), 0) + (h if forward else 0)
    cl = lax.broadcasted_iota(jnp.int32, (h, C), 1)
    keep_e = (ci <= ri) if forward else (ci >= ri)
    keep_l = (cl <= rl) if forward else (cl >= rl)
    s_early = jnp.where(keep_e, s["s_early"], 0.0).astype(BF16)
    s_late = jnp.where(keep_l, s["s_late"], 0.0).astype(BF16)
    o_early = jnp.dot(s_early, s["vb"][early], preferred_element_type=F32)
    o_late = jnp.dot(s_late, s["vb"], preferred_element_type=F32)
    o_intra = jnp.concatenate([o_early, o_late] if forward else [o_late, o_early], axis=0)
    return dict(qd=s["qd"], e_all=s["e_all"], o_intra=o_intra, upd=_dot_tn(s["vb"], s["kd"]))


def _fast_stage_state(s, st):
    o = s["o_intra"] + _dot_nt(s["qd"], st.astype(BF16))
    return o, st * s["e_all"] + s["upd"]


def _scan_chunk(q, ng, k, v, big_ref, msk_ref, st, C, forward):
    psum =jnp.dot(big_ref[...], jnp.concatenate(_split3(ng), axis=0), preferred_element_type=F32)
    ex = jnp.exp(-psum)
    e_in = ex[0:C]
    e_out = ex[C:2 * C]
    nlev = msk_ref.shape[0] - 1
    o = _dot_nt((q * e_in).astype(BF16), st.astype(BF16))
    scores = msk_ref[0] * _dot_nt(q.astype(BF16), k.astype(BF16))
    for l in range(nlev):
        el = ex[(2 + l) * C:(3 + l) * C]
        scores = scores + msk_ref[1 + l] * _dot_nt((q * el).astype(BF16), (k * el).astype(BF16))
    o = o + jnp.dot(scores.astype(BF16), v.astype(BF16), preferred_element_type=F32)
    e_all = e_in[C - 1:C, :] if forward else e_in[0:1, :]
    st_new = st * e_all + _dot_tn(v.astype(BF16), (k * e_out).astype(BF16))
    return o, st_new


def _hgrn_kernel(*refs, L, C, U, with_out):
    if with_out:
        (q_ref, ngf_ref, kf_ref, ngb_ref, kb_ref, iv_ref, ga_ref, gw_ref, bigf_ref, mskf_ref,
         bigb_ref, mskb_ref, s0f_ref, s0b_ref, o_ref, sf_ref, sb_ref, sf_in, sb_in, of_scr, ob_scr) = refs
    else:
        (q_ref, ngf_ref, kf_ref, ngb_ref, kb_ref, iv_ref, bigf_ref, mskf_ref, bigb_ref, mskb_ref,
         s0f_ref, s0b_ref, sf_ref, sb_ref, sf_in, sb_in) = refs
    n = L // C
    sf_ref[...] = s0f_ref[...]
    sb_ref[...] = s0b_ref[...]
    trif_ref = bigf_ref.at[0:C, :]
    trib_ref = bigb_ref.at[0:C, :]

    def rows_of(it, u):
        c = it * U + u
        return (pl.ds(pl.multiple_of(c * C, C), C), pl.ds(pl.multiple_of((n - 1 - c) * C, C), C))

    def put(rf, rb, o_f, o_b):
        if with_out:
            of_scr[rf, :] = o_f
            ob_scr[rb, :] = o_b

    def body(it, carry):
        sf_in[...] = sf_ref[...]
        sb_in[...] = sb_ref[...]
        rows = [rows_of(it, u) for u in range(U)]
        fwd = [_fast_stage_sums(q_ref[rf, :], ngf_ref[rf, :], kf_ref[rf, :], iv_ref[rf, :], trif_ref)
               for rf, _ in rows]
        bwd = [_fast_stage_sums(q_ref[rb, :], ngb_ref[rb, :], kb_ref[rb, :], iv_ref[rb, :], trib_ref)
               for _, rb in rows]
        fwd = [_fast_stage_scores(s, C, True) for s in fwd]
        bwd = [_fast_stage_scores(s, C, False) for s in bwd]
        guard = jnp.max(functools.reduce(jnp.maximum, [s["guard"] for s in fwd + bwd]))
        fwd = [_fast_stage_intra(s, C, True) for s in fwd]
        bwd = [_fast_stage_intra(s, C, False) for s in bwd]
        st_f, st_b = sf_ref[...], sb_ref[...]
        for u in range(U):
            o_f, st_f = _fast_stage_state(fwd[u], st_f)
            o_b, st_b = _fast_stage_state(bwd[u], st_b)
            put(rows[u][0], rows[u][1], o_f, o_b)
        sf_ref[...] = st_f
        sb_ref[...] = st_b

        @pl.when(jnp.logical_not(guard <= SCAN_GUARD))
        def _():
            sf_ref[...] = sf_in[...]
            sb_ref[...] = sb_in[...]

            def redo(u, carry2):
                rf, rb = rows_of(it, u)
                o_f, s_f = _scan_chunk(q_ref[rf, :], ngf_ref[rf, :], kf_ref[rf, :], iv_ref[rf, :],
                                       bigf_ref, mskf_ref, sf_ref[...], C, True)
                sf_ref[...] = s_f
                o_b, s_b = _scan_chunk(q_ref[rb, :], ngb_ref[rb, :], kb_ref[rb, :], iv_ref[rb, :],
                                       bigb_ref, mskb_ref, sb_ref[...], C, False)
                sb_ref[...] = s_b
                put(rf, rb, o_f, o_b)
                return carry2

            lax.fori_loop(0, U, redo, 0)

        return carry

    lax.fori_loop(0, n // U, body, 0)

    if with_out:
        gw = gw_ref[...]
        R = min(L, 512)

        def fin(r, carry):
            rows = pl.ds(pl.multiple_of(r * R, R), R)
            o = of_scr[rows, :] + ob_scr[rows, :]
            o = o * lax.rsqrt(jnp.mean(o * o, axis=-1, keepdims=True) + EPS) * gw
            ga = ga_ref[rows, :]
            o_ref[rows, :] = (o * (ga * jax.nn.sigmoid(ga))).astype(o_ref.dtype)
            return carry

        lax.fori_loop(0, L // R, fin, 0)


def _hgrn(px, g_norm_w, s0f, s0b, with_out):
    B, L, _ = px.shape
    C = SCAN_CHUNK
    H = A_HEADS
    bigf, mskf = _scan_consts(C, True)
    bigb, mskb = _scan_consts(C, False)
    bigf, bigb = jnp.asarray(bigf, BF16), jnp.asarray(bigb, BF16)
    mskf, mskb = jnp.asarray(mskf, F32), jnp.asarray(mskb, F32)

    def col(j):
        return pl.BlockSpec((None, L, LANES), lambda b, h, j=j: (b, 0, j * H + h))

    def const(a):
        return pl.BlockSpec(a.shape, lambda b, h, nd=a.ndim: (0,) * nd)

    st_spec = pl.BlockSpec((None, None, HEAD_DIM, HEAD_DIM), lambda b, h: (b, h, 0, 0))
    st_shape = jax.ShapeDtypeStruct((B, H, HEAD_DIM, HEAD_DIM), F32)
    in_specs = [col(j) for j in range(6)]
    args = [px] * 6
    if with_out:
        in_specs += [col(6), pl.BlockSpec((1, LANES), lambda b, h: (0, h))]
        args += [px, g_norm_w.reshape(1, D_A)]
    in_specs += [const(bigf), const(mskf), const(bigb), const(mskb), st_spec, st_spec]
    args += [bigf, mskf, bigb, mskb, s0f, s0b]
    out_specs = [st_spec, st_spec]
    out_shape = [st_shape, st_shape]
    scratch = [pltpu.VMEM((HEAD_DIM, HEAD_DIM), F32), pltpu.VMEM((HEAD_DIM, HEAD_DIM), F32)]
    if with_out:
        out_specs = [pl.BlockSpec((None, L, LANES), lambda b, h: (b, 0, h))] + out_specs
        out_shape = [jax.ShapeDtypeStruct((B, L, D_A), BF16)] + out_shape
        scratch += [pltpu.VMEM((L, LANES), F32), pltpu.VMEM((L, LANES), F32)]
    res = pl.pallas_call(
        functools.partial(_hgrn_kernel, L=L, C=C, U=min(SCAN_UNROLL, L // C), with_out=with_out),
        grid=(B, H),
        in_specs=in_specs,
        out_specs=out_specs,
        out_shape=out_shape,
        scratch_shapes=scratch,
        compiler_params=_cparams("arbitrary", "arbitrary"),
        name="hgrn2",
    )(*args)
    if with_out:
        return res[0], res[1], res[2]
    return None, res[0], res[1]


def _conv3(ref, r0, R, L, cp):
    T = 2 * SUBLANES
    cur = ref[pl.ds(r0, R), :].astype(F32)
    prev_tile = ref[pl.ds(pl.multiple_of(jnp.maximum(r0 - T, 0), T), T), :].astype(F32)
    next_tile = ref[pl.ds(pl.multiple_of(jnp.minimum(r0 + R, L - T), T), T), :].astype(F32)
    prow = jnp.where(r0 > 0, prev_tile[T - 1:T, :], 0.0)
    nrow = jnp.where(r0 + R < L, next_tile[0:1, :], 0.0)
    rid = lax.broadcasted_iota(jnp.int32, cur.shape, 0)
    up = jnp.where(rid == 0, prow, pltpu.roll(cur, 1, 0))
    dn = jnp.where(rid == R - 1, nrow, pltpu.roll(cur, R - 1, 0))
    return cp[3:4, :] + up * cp[0:1, :] + cur * cp[1:2, :] + dn * cp[2:3, :]


def _hyena_pre_kernel(x0_ref, x1_ref, v_ref, gb_ref, c0_ref, c1_ref, c2_ref, vin_ref, m_ref, *, L):
    R = min(L, 256 * LANES // x0_ref.shape[-1])
    c0 = c0_ref[...]
    c1 = c1_ref[...]
    c2 = c2_ref[...]

    def body(r, carry):
        r0 = pl.multiple_of(r * R, R)
        rows = pl.ds(r0, R)
        x0 = _conv3(x0_ref, r0, R, L, c0)
        x1 = _conv3(x1_ref, r0, R, L, c1)
        vv = _conv3(v_ref, r0, R, L, c2)
        gb = gb_ref[rows, :].astype(F32)
        vin_ref[rows, :] = vv * x1
        m_ref[rows, :] = (x0 * (gb * jax.nn.sigmoid(gb))).astype(m_ref.dtype)
        return carry

    lax.fori_loop(0, L // R, body, 0)


def _hyena_pre(px, conv_par):
    B, L, _ = px.shape
    W = 2 * LANES
    nb = D_B // W

    def col(j):
        return pl.BlockSpec((None, L, W), lambda b, c, j=j: (b, 0, j * nb + c))

    def cpar(j):
        return pl.BlockSpec((SUBLANES, W), lambda b, c, j=j: (0, j * nb + c))

    out_spec = pl.BlockSpec((None, L, W), lambda b, c: (b, 0, c))
    out_shape = jax.ShapeDtypeStruct((B, L, D_B), F32)
    return pl.pallas_call(
        functools.partial(_hyena_pre_kernel, L=L),
        grid=(B, nb),
        in_specs=[col(0), col(1), col(2), col(3), cpar(0), cpar(1), cpar(2)],
        out_specs=[out_spec, out_spec],
        out_shape=[out_shape, jax.ShapeDtypeStruct((B, L, D_B), BF16)],
        compiler_params=_cparams("arbitrary", "arbitrary"),
        name="hyena_pre",
    )(px, px, px, px, conv_par, conv_par, conv_par)


def _pos_rows(L, order):
    f32 = np.float32
    t = np.linspace(0.0, 1.0, L, dtype=f32)[:, None]
    w = (f32(2.0 * math.pi) * np.arange(L, dtype=f32)[:, None]) / f32(L)
    f = np.linspace(1e-4, HY_BANDS - 1, HY_BANDS, dtype=f32)[None, :]
    z = np.concatenate([t, np.cos(f * w), -np.sin(f * w)], axis=-1).astype(f32)
    lag = np.arange(2 * L)
    src = np.where(lag < L, lag, 2 * L - lag) % L
    fwd = (lag < L).astype(f32)
    valid = (lag != L).astype(f32)
    rows = np.concatenate([z[src], fwd[:, None], valid[:, None],
                           np.zeros((2 * L, HY_WIDTH - HY_EMB - 2), f32)], axis=-1)
    return jnp.asarray(rows[order])


def _filter_kernel(z_ref, zt_ref, w1_ref, w2_ref, w3_ref, vec_ref, w4h_ref, w4l_ref, dl_ref, k_ref,
                   *, n_rows):
    R = min(n_rows, 512)
    S = min(R, LANES)
    dl = dl_ref[...]
    vec = vec_ref[...]
    b1, b2, b3, fr = vec[:, 0:1], vec[:, 1:2], vec[:, 2:3], vec[:, 3:4]

    def body(r, acc):
        h = jnp.sin(fr * (jnp.dot(w1_ref[...], zt_ref[r], precision=HIGHEST,
                                  preferred_element_type=F32) + b1))
        h = jnp.sin(fr * (jnp.dot(w2_ref[...], h, precision=HIGHEST, preferred_element_type=F32) + b2))
        h = jnp.sin(fr * (jnp.dot(w3_ref[...], h, precision=HIGHEST, preferred_element_type=F32) + b3))
        for s0 in range(0, R, S):
            rows = pl.ds(pl.multiple_of(r * R + s0, S), S)
            z = z_ref[rows, :]
            hs = h[:, s0:s0 + S].T
            hh = hs.astype(BF16)
            hl = (hs - hh.astype(F32)).astype(BF16)
            hfb = (jnp.dot(hh, w4h_ref[...], preferred_element_type=F32)
                   + jnp.dot(hl, w4h_ref[...], preferred_element_type=F32)
                   + jnp.dot(hh, w4l_ref[...], preferred_element_type=F32))
            fwd = z[:, HY_EMB:HY_EMB + 1]
            valid = z[:, HY_EMB + 1:HY_EMB + 2]
            k = (jnp.where(fwd > 0.5, hfb[:, :D_B], hfb[:, D_B:])
                 * jnp.exp(-z[:, 0:1] * dl) * valid)
            k_ref[rows, :] = k
            acc = acc + jnp.sum(jnp.abs(k), axis=0, keepdims=True)
        return acc

    tot = lax.fori_loop(0, n_rows // R, body, jnp.zeros((1, D_B), F32))

    def scale(r, carry):
        rows = pl.ds(pl.multiple_of(r * R, R), R)
        k_ref[rows, :] = k_ref[rows, :] / tot
        return carry

    lax.fori_loop(0, n_rows // R, scale, 0)


def _hyena_filter(zrows, w1, b1, freq, w2, b2, w3, b3, w4):
    n_rows = zrows.shape[0]
    R = min(n_rows, 512)
    zt = zrows.reshape(n_rows // R, R, HY_WIDTH).transpose(0, 2, 1)
    w1t = jnp.concatenate([w1, jnp.zeros((HY_WIDTH - HY_EMB, HY_WIDTH), F32)], axis=0).T
    vec = jnp.concatenate([b1[:, None], b2[:, None], b3[:, None], freq[:, None],
                           jnp.zeros((HY_WIDTH, SUBLANES - 4), F32)], axis=1)
    deltas = jnp.abs(jnp.linspace(HY_MIN_DECAY, HY_MAX_DECAY, D_B, dtype=F32))[None, :]
    w4h, w4l = _hi_lo(w4)

    def const(shape):
        return pl.BlockSpec(shape, lambda c, nd=len(shape): (0,) * nd)

    return pl.pallas_call(
        functools.partial(_filter_kernel, n_rows=n_rows),
        grid=(1,),
        in_specs=[const((n_rows, HY_WIDTH)), const(zt.shape), const((HY_WIDTH, HY_WIDTH)),
                  const((HY_WIDTH, HY_WIDTH)), const((HY_WIDTH, HY_WIDTH)), const((HY_WIDTH, SUBLANES)),
                  const((HY_WIDTH, 2 * D_B)), const((HY_WIDTH, 2 * D_B)), const((1, D_B))],
        out_specs=const((n_rows, D_B)),
        out_shape=jax.ShapeDtypeStruct((n_rows, D_B), F32),
        compiler_params=_cparams("arbitrary"),
        name="hyena_filter",
    )(zrows, zt, w1t, w2.T, w3.T, vec, w4h, w4l, deltas)


def _cplx_block(gr, gi):
    return np.block([[gr, -gi], [gi, gr]])


def _interleave(n):
    return np.stack([np.arange(n), n + np.arange(n)], axis=1).reshape(-1)


def _hi_lo(a):
    a = jnp.asarray(a, F32)
    hi = a.astype(BF16)
    return hi, (a - hi.astype(F32)).astype(BF16)


def _dft_consts(L):
    N = 2 * L
    n2 = FFT_MINOR
    n1 = N // n2
    half = n1 // 2
    f1 = np.arange(n1)
    t2 = np.arange(n2)
    t1 = np.arange(n1)
    tt = n2 * t1[None, None, :] + t2[:, None, None]
    ang = -2.0 * np.pi * ((f1[None, :, None] * tt) % N) / N
    gr, gi = np.cos(ang), np.sin(ang)
    m1_data = np.stack([_cplx_block(gr[j][:, :half], gi[j][:, :half]) for j in range(n2)])
    m1_filt = np.concatenate([gr, gi], axis=1)
    m1_inv = np.stack([_cplx_block(gr[j][:, :half].T, -gi[j][:, :half].T) for j in range(n2)]) / N
    ang2 = -2.0 * np.pi * ((t2[:, None] * t2[None, :]) % n2) / n2
    f2 = _cplx_block(np.cos(ang2), np.sin(ang2))
    f2_inv = _cplx_block(np.cos(ang2), -np.sin(ang2))
    il1 = _interleave(n1)
    il2 = _interleave(n2)
    as_bf16 = lambda a: jnp.asarray(a, F32).astype(BF16)
    return dict(n1=n1, n2=n2, m1_filt=_hi_lo(m1_filt), f2=_hi_lo(f2[None]),
                m1_data=as_bf16(m1_data[:, il1, :]), f2_packed=as_bf16(f2[:, il2]),
                f2_inv_packed=as_bf16(f2_inv[il2, :]), m1_inv=as_bf16(m1_inv[:, :, il1]))


def _dft_single_consts(L):
    N = 2 * L
    f = np.arange(N)
    ang = -2.0 * np.pi * ((f[:, None] * f[None, :]) % N) / N
    gr, gi = np.cos(ang), np.sin(ang)
    fwd = _cplx_block(gr[:, :L], gi[:, :L])
    filt = np.concatenate([gr, gi], axis=0)
    inv = _cplx_block(gr[:L, :], -gi[:L, :]) / N
    return dict(fwd=_hi_lo(fwd[None]), filt=_hi_lo(filt[None]), inv=_hi_lo(inv[None]))


def _bmm_kernel(mh_ref, ml_ref, x_ref, o_ref, *, gblk, shared, precise):
    for j in range(gblk):
        jm = 0 if shared else j
        x = x_ref[j]
        xh = x.astype(BF16)
        acc = jnp.dot(mh_ref[jm], xh, preferred_element_type=F32)
        if precise:
            xl = (x - xh.astype(F32)).astype(BF16)
            acc = (acc + jnp.dot(mh_ref[jm], xl, preferred_element_type=F32)
                   + jnp.dot(ml_ref[jm], xh, preferred_element_type=F32))
        o_ref[j] = acc.astype(o_ref.dtype)


def _bmm_left(m, x, gblk, precise, out_dtype):
    mh, ml = m
    P, G, K, C = x.shape
    R = mh.shape[1]
    shared = mh.shape[0] == 1
    gblk = min(gblk, G)
    m_spec = (pl.BlockSpec((1, R, K), lambda g, p: (0, 0, 0)) if shared
              else pl.BlockSpec((gblk, R, K), lambda g, p: (g, 0, 0)))
    return pl.pallas_call(
        functools.partial(_bmm_kernel, gblk=gblk, shared=shared, precise=precise),
        grid=(G // gblk, P),
        in_specs=[m_spec, m_spec, pl.BlockSpec((None, gblk, K, C), lambda g, p: (p, g, 0, 0))],
        out_specs=pl.BlockSpec((None, gblk, R, C), lambda g, p: (p, g, 0, 0)),
        out_shape=jax.ShapeDtypeStruct((P, G, R, C), out_dtype),
        compiler_params=_cparams("arbitrary", "arbitrary"),
        name="dft_stage",
    )(mh, ml, x)


def _mid_kernel(fa_ref, fb_ref, kf_ref, hb_ref, d_ref, o_ref, *, gblk, nf):
    fa = fa_ref[...]
    fb = fb_ref[...]
    hb = hb_ref[...]
    for j in range(gblk):
        xx = jnp.dot(fa, d_ref[j].astype(BF16), preferred_element_type=F32)
        xr, xi = xx[:nf], xx[nf:]
        kr, ki = kf_ref[j, :nf] + hb, kf_ref[j, nf:]
        yy = jnp.concatenate([xr * kr - xi * ki, xr * ki + xi * kr], axis=0)
        o_ref[j] = jnp.dot(fb, yy.astype(BF16), preferred_element_type=F32).astype(o_ref.dtype)


def _spectral_mid(fa, fb, kf, hb, d, gblk, out_dtype):
    P, G, K, C = d.shape
    nf2 = fa.shape[0]
    Ko = fb.shape[0]
    gblk = min(gblk, G)
    return pl.pallas_call(
        functools.partial(_mid_kernel, gblk=gblk, nf=nf2 // 2),
        grid=(G // gblk, P),
        in_specs=[pl.BlockSpec(fa.shape, lambda g, p: (0, 0)),
                  pl.BlockSpec(fb.shape, lambda g, p: (0, 0)),
                  pl.BlockSpec((gblk, nf2, C), lambda g, p: (g, 0, 0)),
                  pl.BlockSpec((1, C), lambda g, p: (0, 0)),
                  pl.BlockSpec((None, gblk, K, C), lambda g, p: (p, g, 0, 0))],
        out_specs=pl.BlockSpec((None, gblk, Ko, C), lambda g, p: (p, g, 0, 0)),
        out_shape=jax.ShapeDtypeStruct((P, G, Ko, C), out_dtype),
        compiler_params=_cparams("arbitrary", "arbitrary"),
        name="dft_mid",
    )(fa, fb, kf, hb, d)


def _filter_first_kernel(mh_ref, ml_ref, x_ref, o_ref, *, gblk, n1):
    for j in range(gblk):
        x = x_ref[j]
        xh = x.astype(BF16)
        xl = (x - xh.astype(F32)).astype(BF16)
        acc = (jnp.dot(mh_ref[j], xh, preferred_element_type=F32)
               + jnp.dot(mh_ref[j], xl, preferred_element_type=F32)
               + jnp.dot(ml_ref[j], xh, preferred_element_type=F32))
        o_ref[:, 0, j, :] = acc[:n1]
        o_ref[:, 1, j, :] = acc[n1:]


def _filter_first_stage(m, x, n1, n2):
    mh, ml = m
    C = x.shape[-1]
    gblk = min(DFT_SLAB, n2)
    m_spec = pl.BlockSpec((gblk,) + mh.shape[1:], lambda g: (g, 0, 0))
    return pl.pallas_call(
        functools.partial(_filter_first_kernel, gblk=gblk, n1=n1),
        grid=(n2 // gblk,),
        in_specs=[m_spec, m_spec, pl.BlockSpec((gblk, n1, C), lambda g: (g, 0, 0))],
        out_specs=pl.BlockSpec((n1, 2, gblk, C), lambda g: (0, 0, g, 0)),
        out_shape=jax.ShapeDtypeStruct((n1, 2, n2, C), F32),
        compiler_params=_cparams("arbitrary"),
        name="dft_filter_first",
    )(mh, ml, x)


def _first_stage_kernel(m_ref, x_ref, o_ref, *, gblk):
    for j in range(gblk):
        x = jnp.concatenate([x_ref[0, :, j, :], x_ref[1, :, j, :]], axis=0).astype(BF16)
        acc = jnp.dot(m_ref[j], x, preferred_element_type=F32)
        o_ref[:, j, :] = pltpu.bitcast(acc.astype(BF16), jnp.uint32)


def _first_stage(m, x, n1, n2):
    _, P, half, _, C = x.shape
    gblk = min(DFT_SLAB, n2)
    return pl.pallas_call(
        functools.partial(_first_stage_kernel, gblk=gblk),
        grid=(n2 // gblk, P),
        in_specs=[pl.BlockSpec((gblk,) + m.shape[1:], lambda g, p: (g, 0, 0)),
                  pl.BlockSpec((2, None, half, gblk, C), lambda g, p: (0, p, 0, g, 0))],
        out_specs=pl.BlockSpec((None, n1, gblk, C), lambda g, p: (p, 0, g, 0)),
        out_shape=jax.ShapeDtypeStruct((P, n1, n2, C), jnp.uint32),
        compiler_params=_cparams("arbitrary", "arbitrary"),
        name="dft_first",
    )(m, x)


def _packed_mid_kernel(fa_ref, fb_ref, kf_ref, hb_ref, d_ref, o_ref, *, gblk, nf):
    fa = fa_ref[...]
    fb = fb_ref[...]
    hb = hb_ref[...]
    for j in range(gblk):
        xx = jnp.dot(fa, pltpu.bitcast(d_ref[j], BF16), preferred_element_type=F32)
        xr, xi = xx[:nf], xx[nf:]
        kr, ki = kf_ref[j, :nf] + hb, kf_ref[j, nf:]
        yy = jnp.concatenate([xr * kr - xi * ki, xr * ki + xi * kr], axis=0)
        zz = jnp.dot(fb, yy.astype(BF16), preferred_element_type=F32)
        o_ref[j] = pltpu.bitcast(zz.astype(BF16), jnp.uint32)


def _packed_mid(fa, fb, kf, hb, d, gblk):
    P, G, K, C = d.shape
    gblk = min(gblk, G)
    return pl.pallas_call(
        functools.partial(_packed_mid_kernel, gblk=gblk, nf=K),
        grid=(G // gblk, P),
        in_specs=[pl.BlockSpec(fa.shape, lambda g, p: (0, 0)),
                  pl.BlockSpec(fb.shape, lambda g, p: (0, 0)),
                  pl.BlockSpec((gblk, 2 * K, C), lambda g, p: (g, 0, 0)),
                  pl.BlockSpec((1, C), lambda g, p: (0, 0)),
                  pl.BlockSpec((None, gblk, K, C), lambda g, p: (p, g, 0, 0))],
        out_specs=pl.BlockSpec((None, gblk, K, C), lambda g, p: (p, g, 0, 0)),
        out_shape=jax.ShapeDtypeStruct((P, G, K, C), jnp.uint32),
        compiler_params=_cparams("arbitrary", "arbitrary"),
        name="dft_mid",
    )(fa, fb, kf, hb, d)


def _last_stage_kernel(m_ref, z_ref, y_ref, *, gblk, half):
    for j in range(gblk):
        z = pltpu.bitcast(z_ref[:, j, :], BF16)
        y = jnp.dot(m_ref[j], z, preferred_element_type=F32)
        y_ref[0, :, j, :] = y[:half]
        y_ref[1, :, j, :] = y[half:]


def _last_stage(m, z, half):
    P, n1, n2, C = z.shape
    gblk = min(DFT_SLAB, n2)
    return pl.pallas_call(
        functools.partial(_last_stage_kernel, gblk=gblk, half=half),
        grid=(n2 // gblk, P),
        in_specs=[pl.BlockSpec((gblk,) + m.shape[1:], lambda g, p: (g, 0, 0)),
                  pl.BlockSpec((None, n1, gblk, C), lambda g, p: (p, 0, g, 0))],
        out_specs=pl.BlockSpec((2, None, half, gblk, C), lambda g, p: (0, p, 0, g, 0)),
        out_shape=jax.ShapeDtypeStruct((2, P, half, n2, C), F32),
        compiler_params=_cparams("arbitrary", "arbitrary"),
        name="dft_last",
    )(m, z)


def _long_conv_two_stage(vin, filt_rows, hy_bias, dc):
    B, L, C = vin.shape
    n1, n2 = dc["n1"], dc["n2"]
    half = n1 // 2
    P = B // 2
    ka = _filter_first_stage(dc["m1_filt"], filt_rows.reshape(n2, n1, C), n1, n2)
    kf = _bmm_left(dc["f2"], ka.reshape(1, n1, 2 * n2, C), 16, True, F32)[0]
    a = _first_stage(dc["m1_data"], vin.reshape(2, P, half, n2, C), n1, n2)
    z = _packed_mid(dc["f2_packed"], dc["f2_inv_packed"], kf, hy_bias, a, 16)
    y = _last_stage(dc["m1_inv"], z, half)
    return y.reshape(B, L, C)


def _long_conv_single(vin, filt, hy_bias, dc):
    B, L, C = vin.shape
    P = B // 2
    kf = _bmm_left(dc["filt"], filt.reshape(1, 1, 2 * L, C), 1, True, F32)[0]
    d = vin.reshape(2, P, L, C).transpose(1, 0, 2, 3).reshape(P, 1, 2 * L, C)
    y = _spectral_mid(dc["fwd"][0][0], dc["inv"][0][0], kf, hy_bias, d, 1, F32)
    return y.reshape(P, 2, L, C).transpose(1, 0, 2, 3).reshape(B, L, C)


def _outproj_kernel(*refs, final):
    if final:
        oa_ref, y_ref, m_ref, x_ref, gt_ref, w_ref, fw_ref, o_ref = refs
    else:
        oa_ref, y_ref, m_ref, x_ref, gt_ref, w_ref, o_ref = refs
    ob = y_ref[...] * m_ref[...]
    r = (jnp.dot(oa_ref[...], w_ref[0:D_A, :], preferred_element_type=F32)
         + jnp.dot(ob.astype(BF16), w_ref[D_A:D_A + D_B, :], preferred_element_type=F32))
    xn = x_ref[...] + gt_ref[...] * r
    if final:
        ms = jnp.mean(xn * xn, axis=-1, keepdims=True)
        xn = xn * lax.rsqrt(ms + EPS) * fw_ref[...]
    o_ref[...] = xn


def _out_proj(oa, y, m, x, gt, w_bf16, final_w):
    B, L, D = x.shape
    tm = min(L, 1024)
    final = final_w is not None
    half = lambda: pl.BlockSpec((None, tm, D_B), lambda b, i: (b, i, 0))
    in_specs = [half(), half(), half(),
                pl.BlockSpec((None, tm, D), lambda b, i: (b, i, 0)),
                pl.BlockSpec((None, 1, D), lambda b, i: (b, 0, 0)),
                pl.BlockSpec((D, D), lambda b, i: (0, 0))]
    args = [oa, y, m, x, gt, w_bf16]
    if final:
        in_specs.append(pl.BlockSpec((1, D), lambda b, i: (0, 0)))
        args.append(final_w.reshape(1, D))
    return pl.pallas_call(
        functools.partial(_outproj_kernel, final=final),
        grid=(B, L // tm),
        in_specs=in_specs,
        out_specs=pl.BlockSpec((None, tm, D), lambda b, i: (b, i, 0)),
        out_shape=jax.ShapeDtypeStruct((B, L, D), F32),
        compiler_params=_cparams("arbitrary", "arbitrary"),
        name="out_proj",
    )(*args)


def _gate_params(lb):
    rows = []
    for d in range(2):
        rows += [jnp.log(lb[d]), jnp.log1p(-lb[d]), 1.0 - lb[d]]
    rows += [jnp.zeros_like(lb[0])] * (SUBLANES - len(rows))
    return jnp.stack(rows, axis=0)


def kernel(x, c, ctx, c_ctx, norm_w, w_ada, b_ada, w_in, w_out, lb_logits, g_norm_w,
           conv_w, conv_b, hy_w1, hy_b1, hy_freq, hy_w2, hy_b2, hy_w3, hy_b3, hy_w4,
           hy_bias, final_norm_w):
    B, L_lat, D = x.shape
    L_ctx = ctx.shape[1]
    p_lb = jax.nn.softmax(lb_logits.astype(F32), axis=0)
    lbs = jnp.cumsum(p_lb, axis=0)
    lbs = lbs - lbs[0:1]

    n_rows = 2 * SUBLANES
    cc = jnp.zeros((n_rows, D), F32).at[:B].set(c).at[B].set(c_ctx)
    mod = _modulation(cc, w_ada, b_ada)

    dc_lat = _dft_consts(L_lat)
    dc_ctx = _dft_single_consts(L_ctx)
    n1, n2 = dc_lat["n1"], dc_lat["n2"]
    order_lat = (np.arange(n2)[:, None] + n2 * np.arange(n1)[None, :]).reshape(-1)
    z_lat = _pos_rows(L_lat, order_lat)
    z_ctx = _pos_rows(L_ctx, np.arange(2 * L_ctx))
    zero_state = jnp.zeros((B, A_HEADS, HEAD_DIM, HEAD_DIM), F32)

    for l in range(DEPTH):
        last = l == DEPTH - 1
        sh_x, sc_x, gt_x = [mod[l, :B, j * D:(j + 1) * D].reshape(B, 1, D) for j in range(3)]
        sh_c, sc_c, gt_c = [mod[l, B, j * D:(j + 1) * D].reshape(1, 1, D) for j in range(3)]
        w_in_l = w_in[l].astype(BF16)
        w_out_l = w_out[l].astype(BF16)
        gate_par = _gate_params(lbs[l])
        conv_par = jnp.concatenate([conv_w[l], conv_b[l][None],
                                    jnp.zeros((SUBLANES - 4, 3 * D_B), F32)], axis=0)
        filt_args = (hy_w1[l], hy_b1[l], hy_freq[l], hy_w2[l], hy_b2[l], hy_w3[l], hy_b3[l], hy_w4[l])

        flat = lambda a: a.reshape(1, B * L_ctx, a.shape[-1])
        unflat = lambda a: a.reshape(B, L_ctx, a.shape[-1])
        if last:
            pc, _ = _in_proj(flat(ctx), norm_w[l], sc_c, sh_c, gate_par, w_in_l[:, :4 * D_A])
            _, s_f, s_b = _hgrn(unflat(pc), g_norm_w[l], zero_state, zero_state, with_out=False)
        else:
            pc, pc_hy = _in_proj(flat(ctx), norm_w[l], sc_c, sh_c, gate_par, w_in_l)
            oa_c, s_f, s_b = _hgrn(unflat(pc), g_norm_w[l], zero_state, zero_state, with_out=True)
            vin_c, m_c = _hyena_pre(unflat(pc_hy), conv_par)
            filt_c = _hyena_filter(z_ctx, *filt_args)
            y_c = _long_conv_single(vin_c, filt_c, hy_bias[l].reshape(1, D_B), dc_ctx)
            ctx = unflat(_out_proj(flat(oa_c), flat(y_c), flat(m_c), flat(ctx), gt_c, w_out_l, None))

        px, px_hy = _in_proj(x, norm_w[l], sc_x, sh_x, gate_par, w_in_l)
        oa, _, _ = _hgrn(px, g_norm_w[l], s_f, s_b, with_out=True)
        vin, m = _hyena_pre(px_hy, conv_par)
        filt_x = _hyena_filter(z_lat, *filt_args)
        y = _long_conv_two_stage(vin, filt_x, hy_bias[l].reshape(1, D_B), dc_lat)
        x = _out_proj(oa, y, m, x, gt_x, w_out_l, final_norm_w if last else None)

    return x
```

```python
import functools
import math

import numpy as np
import jax
import jax.numpy as jnp
from jax import lax
from jax.experimental import pallas as pl
from jax.experimental.pallas import tpu as pltpu

F32 = jnp.float32
BF16 = jnp.bfloat16
HIGHEST = lax.Precision.HIGHEST

DEPTH = 2
D_A = 512
D_B = 512
A_HEADS = 4
HEAD_DIM = 128
HY_EMB = 33
HY_BANDS = 16
HY_WIDTH = 64
HY_MIN_DECAY = math.log(1e-2) / 1.5
HY_MAX_DECAY = math.log(1e-2) / 0.3
EPS = 1e-6

LANES = 128
SUBLANES = 8
SCAN_CHUNK = 64
SCAN_UNROLL = 8
SCAN_GUARD = 80.0
DFT_SLAB = 64
DFT_F1_PER_STEP = 16
FFT_MINOR = 128
MOD_COLS = 768
IN_PROJ_ROWS = 512
OUT_PROJ_ROWS = 1024
ROW_CHUNK = 512
PRE_TILE_VREGS = 32
VMEM_LIMIT = 56 * 1024 * 1024


def _cparams(*sem):
    return pltpu.CompilerParams(dimension_semantics=sem, vmem_limit_bytes=VMEM_LIMIT)


def _mod_kernel(c_ref, w_ref, b_ref, o_ref):
    cc = c_ref[...]
    s = cc * jax.nn.sigmoid(cc)
    o_ref[...] = jnp.dot(s, w_ref[...], precision=HIGHEST,
                         preferred_element_type=F32) + b_ref[...]


def _modulation(cc, w_ada, b_ada):
    R, D = cc.shape
    N = w_ada.shape[-1]
    tn = MOD_COLS
    return pl.pallas_call(
        _mod_kernel,
        grid=(DEPTH, N // tn),
        in_specs=[
            pl.BlockSpec((R, D), lambda l, j: (0, 0)),
            pl.BlockSpec((None, D, tn), lambda l, j: (l, 0, j)),
            pl.BlockSpec((None, 1, tn), lambda l, j: (l, 0, j)),
        ],
        out_specs=pl.BlockSpec((None, R, tn), lambda l, j: (l, 0, j)),
        out_shape=jax.ShapeDtypeStruct((DEPTH, R, N), F32),
        compiler_params=_cparams("arbitrary", "arbitrary"),
        name="modulation",
    )(cc, w_ada, b_ada.reshape(DEPTH, 1, N))


def _gates(f, par):
    loglb = par[0:1, :]
    log1mlb = par[1:2, :]
    onemlb = par[2:3, :]
    s1 = jnp.log(1.0 + jnp.exp(-jnp.abs(f)))
    b = log1mlb + (jnp.minimum(f, 0.0) - s1)
    ng = -jnp.maximum(loglb, b) - jnp.log(1.0 + jnp.exp(-jnp.abs(loglb - b)))
    k = onemlb * jnp.exp(-(jnp.maximum(f, 0.0) + s1))
    return ng, k


N_SCAN_GROUPS = 5


def _inproj_kernel(x_ref, nw_ref, sc_ref, sh_ref, gp_ref, w_ref, oa_ref, *rest):
    ob_ref = rest[0] if len(rest) == 2 else None
    hx_ref = rest[-1]
    x = x_ref[...]
    ms = jnp.mean(x * x, axis=-1, keepdims=True)
    y = x * lax.rsqrt(ms + EPS) * nw_ref[...]
    hx_ref[...] = (y * (1.0 + sc_ref[...]) + sh_ref[...]).astype(BF16)
    gp = gp_ref[...]
    w = D_A
    for g in range(w_ref.shape[-1] // w):
        r = jnp.dot(hx_ref[...], w_ref[:, g * w:(g + 1) * w], preferred_element_type=F32)
        if g == 0:
            oa_ref[:, 0:w] = r * jax.nn.sigmoid(r)
        elif g in (1, 2):
            ng, k = _gates(r, gp[3 * (g - 1):3 * g])
            oa_ref[:, (2 * g - 1) * w:2 * g * w] = ng
            oa_ref[:, 2 * g * w:(2 * g + 1) * w] = k
        elif g < N_SCAN_GROUPS:
            oa_ref[:, (g + 2) * w:(g + 3) * w] = r
        else:
            ob_ref[:, (g - N_SCAN_GROUPS) * w:(g - N_SCAN_GROUPS + 1) * w] = r.astype(ob_ref.dtype)


def _in_proj(x, norm_w, sc, sh, gate_par, w_bf16):
    B, L, D = x.shape
    N = w_bf16.shape[1]
    n_a = min(N // D_A, N_SCAN_GROUPS)
    Na = (n_a + 2) * D_A
    Nb = N - n_a * D_A
    tm = min(L, IN_PROJ_ROWS)
    out_specs = [pl.BlockSpec((None, tm, Na), lambda b, i: (b, i, 0))]
    out_shape = [jax.ShapeDtypeStruct((B, L, Na), F32)]
    if Nb:
        out_specs.append(pl.BlockSpec((None, tm, Nb), lambda b, i: (b, i, 0)))
        out_shape.append(jax.ShapeDtypeStruct((B, L, Nb), BF16))
    res = pl.pallas_call(
        _inproj_kernel,
        grid=(B, L // tm),
        in_specs=[
            pl.BlockSpec((None, tm, D), lambda b, i: (b, i, 0)),
            pl.BlockSpec((1, D), lambda b, i: (0, 0)),
            pl.BlockSpec((None, 1, D), lambda b, i: (b, 0, 0)),
            pl.BlockSpec((None, 1, D), lambda b, i: (b, 0, 0)),
            pl.BlockSpec((SUBLANES, D_A), lambda b, i: (0, 0)),
            pl.BlockSpec((D, N), lambda b, i: (0, 0)),
        ],
        out_specs=out_specs,
        out_shape=out_shape,
        scratch_shapes=[pltpu.VMEM((tm, D), BF16)],
        compiler_params=_cparams("arbitrary", "arbitrary"),
        name="in_proj",
    )(x, norm_w.reshape(1, D), sc, sh, gate_par, w_bf16)
    return (res[0], res[1]) if Nb else (res[0], None)


def _scan_consts(C, forward):
    idx = np.arange(C)
    i = idx[:, None]
    t = idx[None, :]
    if forward:
        mats = [t <= i, t > i]
    else:
        mats = [t >= i, t < i]
    masks = [i == t]
    h = 1
    while h < C:
        P = 2 * h
        p = i % P
        m = i - p + h
        upper = p >= h
        same = (i // P) == (t // P)
        if forward:
            mats.append(np.where(upper, (t >= m) & (t <= i), (t >= i + 1) & (t <= m - 1)))
            masks.append(same & upper & ((t % P) < h))
        else:
            mats.append(np.where(upper, (t >= m) & (t <= i - 1), (t >= i) & (t <= m - 1)))
            masks.append(same & (~upper) & ((t % P) >= h))
        h = P
    big = np.concatenate([m_.astype(np.float32) for m_ in mats], axis=0)
    big = np.concatenate([big, big, big], axis=1)
    msk = np.stack([m_.astype(np.float32) for m_ in masks], axis=0)
    return big, msk


def _dot_nt(a, b):
    return lax.dot_general(a, b, (((1,), (1,)), ((), ())), preferred_element_type=F32)


def _dot_tn(a, b):
    return lax.dot_general(a, b, (((0,), (0,)), ((), ())), preferred_element_type=F32)


def _split3(a):
    hi = a.astype(BF16)
    r1 = a - hi.astype(F32)
    mid = r1.astype(BF16)
    lo = (r1 - mid.astype(F32)).astype(BF16)
    return hi, mid, lo


def _cumulative(tri_ref, ng):
    return jnp.dot(tri_ref[...], jnp.concatenate(_split3(ng), axis=0), preferred_element_type=F32)


def _fast_stage_sums(q, ng, k, v, tri_ref):
    return dict(q=q, k=k, vb=v.astype(BF16), cum=_cumulative(tri_ref, ng))


def _fast_stage_scores(s, C, forward):
    h = C // 2
    q, k, cum = s["q"], s["k"], s["cum"]
    if forward:
        early, late = slice(0, h), slice(h, C)
        ref, tot = cum[h - 1:h, :], cum[C - 1:C, :]
    else:
        early, late = slice(h, C), slice(0, h)
        ref, tot = cum[h:h + 1, :], cum[0:1, :]
    qd = (q * jnp.exp(-cum)).astype(BF16)
    dl = cum - ref
    k_early = (k[early] * jnp.exp(cum[early])).astype(BF16)
    q_late = (q[late] * jnp.exp(-dl[late])).astype(BF16)
    k_all = (k * jnp.exp(dl)).astype(BF16)
    kd = (k * jnp.exp(cum - tot)).astype(BF16)
    return dict(vb=s["vb"], qd=qd, kd=kd, e_all=jnp.exp(-tot),
                guard=jnp.maximum(cum[early], dl[late]),
                s_early=_dot_nt(qd[early], k_early), s_late=_dot_nt(q_late, k_all))


def _fast_stage_intra(s, C, forward):
    h = C // 2
    early = slice(0, h) if forward else slice(h, C)
    ri = lax.broadcasted_iota(jnp.int32, (h, h), 0)
    ci = lax.broadcasted_iota(jnp.int32, (h, h), 1)
    rl = lax.broadcasted_iota(jnp.int32, (h, C), 0) + (h if forward else 0)
    cl = lax.broadcasted_iota(jnp.int32, (h, C), 1)
    keep_e = (ci <= ri) if forward else (ci >= ri)
    keep_l = (cl <= rl) if forward else (cl >= rl)
    s_early = jnp.where(keep_e, s["s_early"], 0.0).astype(BF16)
    s_late = jnp.where(keep_l, s["s_late"], 0.0).astype(BF16)
    o_early = jnp.dot(s_early, s["vb"][early], preferred_element_type=F32)
    o_late = jnp.dot(s_late, s["vb"], preferred_element_type=F32)
    o_intra = jnp.concatenate([o_early, o_late] if forward else [o_late, o_early], axis=0)
    return dict(qd=s["qd"], e_all=s["e_all"], o_intra=o_intra, upd=_dot_tn(s["vb"], s["kd"]))


def _fast_stage_state(s, st):
    o = s["o_intra"] + _dot_nt(s["qd"], st.astype(BF16))
    return o, st * s["e_all"] + s["upd"]


def _scan_chunk(q, ng, k, v, big_ref, msk_ref, st, C, forward):
    psum = jnp.dot(big_ref[...], jnp.concatenate(_split3(ng), axis=0), preferred_element_type=F32)
    ex = jnp.exp(-psum)
    e_in = ex[0:C]
    e_out = ex[C:2 * C]
    nlev = msk_ref.shape[0] - 1
    o = _dot_nt((q * e_in).astype(BF16), st.astype(BF16))
    scores = msk_ref[0] * _dot_nt(q.astype(BF16), k.astype(BF16))
    for l in range(nlev):
        el = ex[(2 + l) * C:(3 + l) * C]
        scores = scores + msk_ref[1 + l] * _dot_nt((q * el).astype(BF16), (k * el).astype(BF16))
    o = o + jnp.dot(scores.astype(BF16), v.astype(BF16), preferred_element_type=F32)
    e_all = e_in[C - 1:C, :] if forward else e_in[0:1, :]
    st_new = st * e_all + _dot_tn(v.astype(BF16), (k * e_out).astype(BF16))
    return o, st_new


def _hgrn_kernel(*refs, L, C, U, with_out):
    if with_out:
        (q_ref, ngf_ref, kf_ref, ngb_ref, kb_ref, iv_ref, ga_ref, gw_ref, bigf_ref, mskf_ref,
         bigb_ref, mskb_ref, s0f_ref, s0b_ref, o_ref, sf_ref, sb_ref, sf_in, sb_in, of_scr, ob_scr) = refs
    else:
        (q_ref, ngf_ref, kf_ref, ngb_ref, kb_ref, iv_ref, bigf_ref, mskf_ref, bigb_ref, mskb_ref,
         s0f_ref, s0b_ref, sf_ref, sb_ref, sf_in, sb_in) = refs
    n = L // C
    sf_ref[...] = s0f_ref[...]
    sb_ref[...] = s0b_ref[...]
    trif_ref = bigf_ref.at[0:C, :]
    trib_ref = bigb_ref.at[0:C, :]

    def rows_of(it, u):
        c = it * U + u
        return (pl.ds(pl.multiple_of(c * C, C), C), pl.ds(pl.multiple_of((n - 1 - c) * C, C), C))

    def put(rf, rb, o_f, o_b):
        if with_out:
            of_scr[rf, :] = o_f
            ob_scr[rb, :] = o_b

    def body(it, carry):
        sf_in[...] = sf_ref[...]
        sb_in[...] = sb_ref[...]
        rows = [rows_of(it, u) for u in range(U)]
        fwd = [_fast_stage_sums(q_ref[rf, :], ngf_ref[rf, :], kf_ref[rf, :], iv_ref[rf, :], trif_ref)
               for rf, _ in rows]
        bwd = [_fast_stage_sums(q_ref[rb, :], ngb_ref[rb, :], kb_ref[rb, :], iv_ref[rb, :], trib_ref)
               for _, rb in rows]
        fwd = [_fast_stage_scores(s, C, True) for s in fwd]
        bwd = [_fast_stage_scores(s, C, False) for s in bwd]
        guard = jnp.max(functools.reduce(jnp.maximum, [s["guard"] for s in fwd + bwd]))
        fwd = [_fast_stage_intra(s, C, True) for s in fwd]
        bwd = [_fast_stage_intra(s, C, False) for s in bwd]
        st_f, st_b = sf_ref[...], sb_ref[...]
        for u in range(U):
            o_f, st_f = _fast_stage_state(fwd[u], st_f)
            o_b, st_b = _fast_stage_state(bwd[u], st_b)
            put(rows[u][0], rows[u][1], o_f, o_b)
        sf_ref[...] = st_f
        sb_ref[...] = st_b

        @pl.when(jnp.logical_not(guard <= SCAN_GUARD))
        def _():
            sf_ref[...] = sf_in[...]
            sb_ref[...] = sb_in[...]

            def redo(u, carry2):
                rf, rb = rows_of(it, u)
                o_f, s_f = _scan_chunk(q_ref[rf, :], ngf_ref[rf, :], kf_ref[rf, :], iv_ref[rf, :],
                                       bigf_ref, mskf_ref, sf_ref[...], C, True)
                sf_ref[...] = s_f
                o_b, s_b = _scan_chunk(q_ref[rb, :], ngb_ref[rb, :], kb_ref[rb, :], iv_ref[rb, :],
                                       bigb_ref, mskb_ref, sb_ref[...], C, False)
                sb_ref[...] = s_b
                put(rf, rb, o_f, o_b)
                return carry2

            lax.fori_loop(0, U, redo, 0)

        return carry

    lax.fori_loop(0, n // U, body, 0)

    if with_out:
        gw = gw_ref[...]
        R = min(L, ROW_CHUNK)

        def fin(r, carry):
            rows = pl.ds(pl.multiple_of(r * R, R), R)
            o = of_scr[rows, :] + ob_scr[rows, :]
            o = o * lax.rsqrt(jnp.mean(o * o, axis=-1, keepdims=True) + EPS) * gw
            ga = ga_ref[rows, :]
            o_ref[rows, :] = (o * (ga * jax.nn.sigmoid(ga))).astype(o_ref.dtype)
            return carry

        lax.fori_loop(0, L // R, fin, 0)


def _hgrn(px, g_norm_w, s0f, s0b, with_out):
    B, L, _ = px.shape
    C = SCAN_CHUNK
    H = A_HEADS
    bigf, mskf = _scan_consts(C, True)
    bigb, mskb = _scan_consts(C, False)
    bigf, bigb = jnp.asarray(bigf, BF16), jnp.asarray(bigb, BF16)
    mskf, mskb = jnp.asarray(mskf, F32), jnp.asarray(mskb, F32)

    def col(j):
        return pl.BlockSpec((None, L, LANES), lambda b, h, j=j: (b, 0, j * H + h))

    def const(a):
        return pl.BlockSpec(a.shape, lambda b, h, nd=a.ndim: (0,) * nd)

    st_spec = pl.BlockSpec((None, None, HEAD_DIM, HEAD_DIM), lambda b, h: (b, h, 0, 0))
    st_shape = jax.ShapeDtypeStruct((B, H, HEAD_DIM, HEAD_DIM), F32)
    in_specs = [col(j) for j in range(6)]
    args = [px] * 6
    if with_out:
        in_specs += [col(6), pl.BlockSpec((1, LANES), lambda b, h: (0, h))]
        args += [px, g_norm_w.reshape(1, D_A)]
    in_specs += [const(bigf), const(mskf), const(bigb), const(mskb), st_spec, st_spec]
    args += [bigf, mskf, bigb, mskb, s0f, s0b]
    out_specs = [st_spec, st_spec]
    out_shape = [st_shape, st_shape]
    scratch = [pltpu.VMEM((HEAD_DIM, HEAD_DIM), F32), pltpu.VMEM((HEAD_DIM, HEAD_DIM), F32)]
    if with_out:
        out_specs = [pl.BlockSpec((None, L, LANES), lambda b, h: (b, 0, h))] + out_specs
        out_shape = [jax.ShapeDtypeStruct((B, L, D_A), BF16)] + out_shape
        scratch += [pltpu.VMEM((L, LANES), F32), pltpu.VMEM((L, LANES), F32)]
    res = pl.pallas_call(
        functools.partial(_hgrn_kernel, L=L, C=C, U=min(SCAN_UNROLL, L // C), with_out=with_out),
        grid=(B, H),
        in_specs=in_specs,
        out_specs=out_specs,
        out_shape=out_shape,
        scratch_shapes=scratch,
        compiler_params=_cparams("arbitrary", "arbitrary"),
        name="hgrn2",
    )(*args)
    if with_out:
        return res[0], res[1], res[2]
    return None, res[0], res[1]


def _conv3(ref, r0, R, L, cp):
    T = 2 * SUBLANES
    cur = ref[pl.ds(r0, R), :].astype(F32)
    prev_tile = ref[pl.ds(pl.multiple_of(jnp.maximum(r0 - T, 0), T), T), :].astype(F32)
    next_tile = ref[pl.ds(pl.multiple_of(jnp.minimum(r0 + R, L - T), T), T), :].astype(F32)
    prow = jnp.where(r0 > 0, prev_tile[T - 1:T, :], 0.0)
    nrow = jnp.where(r0 + R < L, next_tile[0:1, :], 0.0)
    rid = lax.broadcasted_iota(jnp.int32, cur.shape, 0)
    up = jnp.where(rid == 0, prow, pltpu.roll(cur, 1, 0))
    dn = jnp.where(rid == R - 1, nrow, pltpu.roll(cur, R - 1, 0))
    return cp[3:4, :] + up * cp[0:1, :] + cur * cp[1:2, :] + dn * cp[2:3, :]


def _hyena_pre_kernel(x0_ref, x1_ref, v_ref, gb_ref, c0_ref, c1_ref, c2_ref, vin_ref, m_ref, *, L):
    R = min(L, PRE_TILE_VREGS * SUBLANES * LANES // x0_ref.shape[-1])
    c0 = c0_ref[...]
    c1 = c1_ref[...]
    c2 = c2_ref[...]

    def body(r, carry):
        r0 = pl.multiple_of(r * R, R)
        rows = pl.ds(r0, R)
        x0 = _conv3(x0_ref, r0, R, L, c0)
        x1 = _conv3(x1_ref, r0, R, L, c1)
        vv = _conv3(v_ref, r0, R, L, c2)
        gb = gb_ref[rows, :].astype(F32)
        vin_ref[rows, :] = vv * x1
        m_ref[rows, :] = (x0 * (gb * jax.nn.sigmoid(gb))).astype(m_ref.dtype)
        return carry

    lax.fori_loop(0, L // R, body, 0)


def _hyena_pre(px, conv_par):
    B, L, _ = px.shape
    W = 2 * LANES
    nb = D_B // W

    def col(j):
        return pl.BlockSpec((None, L, W), lambda b, c, j=j: (b, 0, j * nb + c))

    def cpar(j):
        return pl.BlockSpec((SUBLANES, W), lambda b, c, j=j: (0, j * nb + c))

    out_spec = pl.BlockSpec((None, L, W), lambda b, c: (b, 0, c))
    out_shape = jax.ShapeDtypeStruct((B, L, D_B), F32)
    return pl.pallas_call(
        functools.partial(_hyena_pre_kernel, L=L),
        grid=(B, nb),
        in_specs=[col(0), col(1), col(2), col(3), cpar(0), cpar(1), cpar(2)],
        out_specs=[out_spec, out_spec],
        out_shape=[out_shape, jax.ShapeDtypeStruct((B, L, D_B), BF16)],
        compiler_params=_cparams("arbitrary", "arbitrary"),
        name="hyena_pre",
    )(px, px, px, px, conv_par, conv_par, conv_par)


def _pos_rows(L, order):
    f32 = np.float32
    t = np.linspace(0.0, 1.0, L, dtype=f32)[:, None]
    w = (f32(2.0 * math.pi) * np.arange(L, dtype=f32)[:, None]) / f32(L)
    f = np.linspace(1e-4, HY_BANDS - 1, HY_BANDS, dtype=f32)[None, :]
    z = np.concatenate([t, np.cos(f * w), -np.sin(f * w)], axis=-1).astype(f32)
    lag = np.arange(2 * L)
    src = np.where(lag < L, lag, 2 * L - lag) % L
    fwd = (lag < L).astype(f32)
    valid = (lag != L).astype(f32)
    rows = np.concatenate([z[src], fwd[:, None], valid[:, None],
                           np.zeros((2 * L, HY_WIDTH - HY_EMB - 2), f32)], axis=-1)
    return jnp.asarray(rows[order])


def _filter_kernel(z_ref, zt_ref, w1_ref, w2_ref, w3_ref, vec_ref, w4h_ref, w4l_ref, dl_ref, k_ref,
                   *, n_rows):
    R = min(n_rows, ROW_CHUNK)
    S = min(R, LANES)
    dl = dl_ref[...]
    vec = vec_ref[...]
    b1, b2, b3, fr = vec[:, 0:1], vec[:, 1:2], vec[:, 2:3], vec[:, 3:4]

    def body(r, acc):
        h = jnp.sin(fr * (jnp.dot(w1_ref[...], zt_ref[r], precision=HIGHEST,
                                  preferred_element_type=F32) + b1))
        h = jnp.sin(fr * (jnp.dot(w2_ref[...], h, precision=HIGHEST, preferred_element_type=F32) + b2))
        h = jnp.sin(fr * (jnp.dot(w3_ref[...], h, precision=HIGHEST, preferred_element_type=F32) + b3))
        for s0 in range(0, R, S):
            rows = pl.ds(pl.multiple_of(r * R + s0, S), S)
            z = z_ref[rows, :]
            hs = h[:, s0:s0 + S].T
            hh = hs.astype(BF16)
            hl = (hs - hh.astype(F32)).astype(BF16)
            hfb = (jnp.dot(hh, w4h_ref[...], preferred_element_type=F32)
                   + jnp.dot(hl, w4h_ref[...], preferred_element_type=F32)
                   + jnp.dot(hh, w4l_ref[...], preferred_element_type=F32))
            fwd = z[:, HY_EMB:HY_EMB + 1]
            valid = z[:, HY_EMB + 1:HY_EMB + 2]
            k = (jnp.where(fwd > 0.5, hfb[:, :D_B], hfb[:, D_B:])
                 * jnp.exp(-z[:, 0:1] * dl) * valid)
            k_ref[rows, :] = k
            acc = acc + jnp.sum(jnp.abs(k), axis=0, keepdims=True)
        return acc

    tot = lax.fori_loop(0, n_rows // R, body, jnp.zeros((1, D_B), F32))

    def scale(r, carry):
        rows = pl.ds(pl.multiple_of(r * R, R), R)
        k_ref[rows, :] = k_ref[rows, :] / tot
        return carry

    lax.fori_loop(0, n_rows // R, scale, 0)


def _hyena_filter(zrows, w1, b1, freq, w2, b2, w3, b3, w4):
    n_rows = zrows.shape[0]
    R = min(n_rows, ROW_CHUNK)
    zt = zrows.reshape(n_rows // R, R, HY_WIDTH).transpose(0, 2, 1)
    w1t = jnp.concatenate([w1, jnp.zeros((HY_WIDTH - HY_EMB, HY_WIDTH), F32)], axis=0).T
    vec = jnp.concatenate([b1[:, None], b2[:, None], b3[:, None], freq[:, None],
                           jnp.zeros((HY_WIDTH, SUBLANES - 4), F32)], axis=1)
    deltas = jnp.abs(jnp.linspace(HY_MIN_DECAY, HY_MAX_DECAY, D_B, dtype=F32))[None, :]
    w4h, w4l = _hi_lo(w4)

    def const(shape):
        return pl.BlockSpec(shape, lambda c, nd=len(shape): (0,) * nd)

    return pl.pallas_call(
        functools.partial(_filter_kernel, n_rows=n_rows),
        grid=(1,),
        in_specs=[const((n_rows, HY_WIDTH)), const(zt.shape), const((HY_WIDTH, HY_WIDTH)),
                  const((HY_WIDTH, HY_WIDTH)), const((HY_WIDTH, HY_WIDTH)), const((HY_WIDTH, SUBLANES)),
                  const((HY_WIDTH, 2 * D_B)), const((HY_WIDTH, 2 * D_B)), const((1, D_B))],
        out_specs=const((n_rows, D_B)),
        out_shape=jax.ShapeDtypeStruct((n_rows, D_B), F32),
        compiler_params=_cparams("arbitrary"),
        name="hyena_filter",
    )(zrows, zt, w1t, w2.T, w3.T, vec, w4h, w4l, deltas)


def _cplx_block(gr, gi):
    return np.block([[gr, -gi], [gi, gr]])


def _interleave(n):
    return np.stack([np.arange(n), n + np.arange(n)], axis=1).reshape(-1)


def _hi_lo(a):
    a = jnp.asarray(a, F32)
    hi = a.astype(BF16)
    return hi, (a - hi.astype(F32)).astype(BF16)


def _dft_consts(L):
    N = 2 * L
    n2 = FFT_MINOR
    n1 = N // n2
    half = n1 // 2
    f1 = np.arange(n1)
    t2 = np.arange(n2)
    t1 = np.arange(n1)
    tt = n2 * t1[None, None, :] + t2[:, None, None]
    ang = -2.0 * np.pi * ((f1[None, :, None] * tt) % N) / N
    gr, gi = np.cos(ang), np.sin(ang)
    m1_data = np.stack([_cplx_block(gr[j][:, :half], gi[j][:, :half]) for j in range(n2)])
    m1_filt = np.concatenate([gr, gi], axis=1)
    m1_inv = np.stack([_cplx_block(gr[j][:, :half].T, -gi[j][:, :half].T) for j in range(n2)]) / N
    ang2 = -2.0 * np.pi * ((t2[:, None] * t2[None, :]) % n2) / n2
    f2 = _cplx_block(np.cos(ang2), np.sin(ang2))
    f2_inv = _cplx_block(np.cos(ang2), -np.sin(ang2))
    il1 = _interleave(n1)
    il2 = _interleave(n2)
    as_bf16 = lambda a: jnp.asarray(a, F32).astype(BF16)
    return dict(n1=n1, n2=n2, m1_filt=_hi_lo(m1_filt), f2=_hi_lo(f2[None]),
                m1_data=as_bf16(m1_data[:, il1, :]), f2_packed=as_bf16(f2[:, il2]),
                f2_inv_packed=as_bf16(f2_inv[il2, :]), m1_inv=as_bf16(m1_inv[:, :, il1]))


def _dft_single_consts(L):
    N = 2 * L
    f = np.arange(N)
    ang = -2.0 * np.pi * ((f[:, None] * f[None, :]) % N) / N
    gr, gi = np.cos(ang), np.sin(ang)
    fwd = _cplx_block(gr[:, :L], gi[:, :L])
    filt = np.concatenate([gr, gi], axis=0)
    inv = _cplx_block(gr[:L, :], -gi[:L, :]) / N
    return dict(fwd=_hi_lo(fwd[None]), filt=_hi_lo(filt[None]), inv=_hi_lo(inv[None]))


def _bmm_kernel(mh_ref, ml_ref, x_ref, o_ref, *, gblk, shared, precise):
    for j in range(gblk):
        jm = 0 if shared else j
        x = x_ref[j]
        xh = x.astype(BF16)
        acc = jnp.dot(mh_ref[jm], xh, preferred_element_type=F32)
        if precise:
            xl = (x - xh.astype(F32)).astype(BF16)
            acc = (acc + jnp.dot(mh_ref[jm], xl, preferred_element_type=F32)
                   + jnp.dot(ml_ref[jm], xh, preferred_element_type=F32))
        o_ref[j] = acc.astype(o_ref.dtype)


def _bmm_left(m, x, gblk, precise, out_dtype):
    mh, ml = m
    P, G, K, C = x.shape
    R = mh.shape[1]
    shared = mh.shape[0] == 1
    gblk = min(gblk, G)
    m_spec = (pl.BlockSpec((1, R, K), lambda g, p: (0, 0, 0)) if shared
              else pl.BlockSpec((gblk, R, K), lambda g, p: (g, 0, 0)))
    return pl.pallas_call(
        functools.partial(_bmm_kernel, gblk=gblk, shared=shared, precise=precise),
        grid=(G // gblk, P),
        in_specs=[m_spec, m_spec, pl.BlockSpec((None, gblk, K, C), lambda g, p: (p, g, 0, 0))],
        out_specs=pl.BlockSpec((None, gblk, R, C), lambda g, p: (p, g, 0, 0)),
        out_shape=jax.ShapeDtypeStruct((P, G, R, C), out_dtype),
        compiler_params=_cparams("arbitrary", "arbitrary"),
        name="dft_stage",
    )(mh, ml, x)


def _mid_kernel(fa_ref, fb_ref, kf_ref, hb_ref, d_ref, o_ref, *, gblk, nf):
    fa = fa_ref[...]
    fb = fb_ref[...]
    hb = hb_ref[...]
    for j in range(gblk):
        xx = jnp.dot(fa, d_ref[j].astype(BF16), preferred_element_type=F32)
        xr, xi = xx[:nf], xx[nf:]
        kr, ki = kf_ref[j, :nf] + hb, kf_ref[j, nf:]
        yy = jnp.concatenate([xr * kr - xi * ki, xr * ki + xi * kr], axis=0)
        o_ref[j] = jnp.dot(fb, yy.astype(BF16), preferred_element_type=F32).astype(o_ref.dtype)


def _spectral_mid(fa, fb, kf, hb, d, gblk, out_dtype):
    P, G, K, C = d.shape
    nf2 = fa.shape[0]
    Ko = fb.shape[0]
    gblk = min(gblk, G)
    return pl.pallas_call(
        functools.partial(_mid_kernel, gblk=gblk, nf=nf2 // 2),
        grid=(G // gblk, P),
        in_specs=[pl.BlockSpec(fa.shape, lambda g, p: (0, 0)),
                  pl.BlockSpec(fb.shape, lambda g, p: (0, 0)),
                  pl.BlockSpec((gblk, nf2, C), lambda g, p: (g, 0, 0)),
                  pl.BlockSpec((1, C), lambda g, p: (0, 0)),
                  pl.BlockSpec((None, gblk, K, C), lambda g, p: (p, g, 0, 0))],
        out_specs=pl.BlockSpec((None, gblk, Ko, C), lambda g, p: (p, g, 0, 0)),
        out_shape=jax.ShapeDtypeStruct((P, G, Ko, C), out_dtype),
        compiler_params=_cparams("arbitrary", "arbitrary"),
        name="dft_mid",
    )(fa, fb, kf, hb, d)


def _filter_first_kernel(mh_ref, ml_ref, x_ref, o_ref, *, gblk, n1):
    for j in range(gblk):
        x = x_ref[j]
        xh = x.astype(BF16)
        xl = (x - xh.astype(F32)).astype(BF16)
        acc = (jnp.dot(mh_ref[j], xh, preferred_element_type=F32)
               + jnp.dot(mh_ref[j], xl, preferred_element_type=F32)
               + jnp.dot(ml_ref[j], xh, preferred_element_type=F32))
        o_ref[:, 0, j, :] = acc[:n1]
        o_ref[:, 1, j, :] = acc[n1:]


def _filter_first_stage(m, x, n1, n2):
    mh, ml = m
    C = x.shape[-1]
    gblk = min(DFT_SLAB, n2)
    m_spec = pl.BlockSpec((gblk,) + mh.shape[1:], lambda g: (g, 0, 0))
    return pl.pallas_call(
        functools.partial(_filter_first_kernel, gblk=gblk, n1=n1),
        grid=(n2 // gblk,),
        in_specs=[m_spec, m_spec, pl.BlockSpec((gblk, n1, C), lambda g: (g, 0, 0))],
        out_specs=pl.BlockSpec((n1, 2, gblk, C), lambda g: (0, 0, g, 0)),
        out_shape=jax.ShapeDtypeStruct((n1, 2, n2, C), F32),
        compiler_params=_cparams("arbitrary"),
        name="dft_filter_first",
    )(mh, ml, x)


def _first_stage_kernel(m_ref, x_ref, o_ref, *, gblk):
    for j in range(gblk):
        x = jnp.concatenate([x_ref[0, :, j, :], x_ref[1, :, j, :]], axis=0).astype(BF16)
        acc = jnp.dot(m_ref[j], x, preferred_element_type=F32)
        o_ref[:, j, :] = pltpu.bitcast(acc.astype(BF16), jnp.uint32)


def _first_stage(m, x, n1, n2):
    _, P, half, _, C = x.shape
    gblk = min(DFT_SLAB, n2)
    return pl.pallas_call(
        functools.partial(_first_stage_kernel, gblk=gblk),
        grid=(n2 // gblk, P),
        in_specs=[pl.BlockSpec((gblk,) + m.shape[1:], lambda g, p: (g, 0, 0)),
                  pl.BlockSpec((2, None, half, gblk, C), lambda g, p: (0, p, 0, g, 0))],
        out_specs=pl.BlockSpec((None, n1, gblk, C), lambda g, p: (p, 0, g, 0)),
        out_shape=jax.ShapeDtypeStruct((P, n1, n2, C), jnp.uint32),
        compiler_params=_cparams("arbitrary", "arbitrary"),
        name="dft_first",
    )(m, x)


def _packed_mid_kernel(fa_ref, fb_ref, kf_ref, hb_ref, d_ref, o_ref, *, gblk, nf):
    fa = fa_ref[...]
    fb = fb_ref[...]
    hb = hb_ref[...]
    for j in range(gblk):
        xx = jnp.dot(fa, pltpu.bitcast(d_ref[j], BF16), preferred_element_type=F32)
        xr, xi = xx[:nf], xx[nf:]
        kr, ki = kf_ref[j, :nf] + hb, kf_ref[j, nf:]
        yy = jnp.concatenate([xr * kr - xi * ki, xr * ki + xi * kr], axis=0)
        zz = jnp.dot(fb, yy.astype(BF16), preferred_element_type=F32)
        o_ref[j] = pltpu.bitcast(zz.astype(BF16), jnp.uint32)


def _packed_mid(fa, fb, kf, hb, d, gblk):
    P, G, K, C = d.shape
    gblk = min(gblk, G)
    return pl.pallas_call(
        functools.partial(_packed_mid_kernel, gblk=gblk, nf=K),
        grid=(G // gblk, P),
        in_specs=[pl.BlockSpec(fa.shape, lambda g, p: (0, 0)),
                  pl.BlockSpec(fb.shape, lambda g, p: (0, 0)),
                  pl.BlockSpec((gblk, 2 * K, C), lambda g, p: (g, 0, 0)),
                  pl.BlockSpec((1, C), lambda g, p: (0, 0)),
                  pl.BlockSpec((None, gblk, K, C), lambda g, p: (p, g, 0, 0))],
        out_specs=pl.BlockSpec((None, gblk, K, C), lambda g, p: (p, g, 0, 0)),
        out_shape=jax.ShapeDtypeStruct((P, G, K, C), jnp.uint32),
        compiler_params=_cparams("arbitrary", "arbitrary"),
        name="dft_mid",
    )(fa, fb, kf, hb, d)


def _last_stage_kernel(m_ref, z_ref, y_ref, *, gblk, half):
    for j in range(gblk):
        z = pltpu.bitcast(z_ref[:, j, :], BF16)
        y = jnp.dot(m_ref[j], z, preferred_element_type=F32)
        y_ref[0, :, j, :] = y[:half]
        y_ref[1, :, j, :] = y[half:]


def _last_stage(m, z, half):
    P, n1, n2, C = z.shape
    gblk = min(DFT_SLAB, n2)
    return pl.pallas_call(
        functools.partial(_last_stage_kernel, gblk=gblk, half=half),
        grid=(n2 // gblk, P),
        in_specs=[pl.BlockSpec((gblk,) + m.shape[1:], lambda g, p: (g, 0, 0)),
                  pl.BlockSpec((None, n1, gblk, C), lambda g, p: (p, 0, g, 0))],
        out_specs=pl.BlockSpec((2, None, half, gblk, C), lambda g, p: (0, p, 0, g, 0)),
        out_shape=jax.ShapeDtypeStruct((2, P, half, n2, C), F32),
        compiler_params=_cparams("arbitrary", "arbitrary"),
        name="dft_last",
    )(m, z)


def _long_conv_two_stage(vin, filt_rows, hy_bias, dc):
    B, L, C = vin.shape
    n1, n2 = dc["n1"], dc["n2"]
    half = n1 // 2
    P = B // 2
    ka = _filter_first_stage(dc["m1_filt"], filt_rows.reshape(n2, n1, C), n1, n2)
    kf = _bmm_left(dc["f2"], ka.reshape(1, n1, 2 * n2, C), DFT_F1_PER_STEP, True, F32)[0]
    a = _first_stage(dc["m1_data"], vin.reshape(2, P, half, n2, C), n1, n2)
    z = _packed_mid(dc["f2_packed"], dc["f2_inv_packed"], kf, hy_bias, a, DFT_F1_PER_STEP)
    y = _last_stage(dc["m1_inv"], z, half)
    return y.reshape(B, L, C)


def _long_conv_single(vin, filt, hy_bias, dc):
    B, L, C = vin.shape
    P = B // 2
    kf = _bmm_left(dc["filt"], filt.reshape(1, 1, 2 * L, C), 1, True, F32)[0]
    d = vin.reshape(2, P, L, C).transpose(1, 0, 2, 3).reshape(P, 1, 2 * L, C)
    y = _spectral_mid(dc["fwd"][0][0], dc["inv"][0][0], kf, hy_bias, d, 1, F32)
    return y.reshape(P, 2, L, C).transpose(1, 0, 2, 3).reshape(B, L, C)


def _outproj_kernel(*refs, final):
    if final:
        oa_ref, y_ref, m_ref, x_ref, gt_ref, w_ref, fw_ref, o_ref = refs
    else:
        oa_ref, y_ref, m_ref, x_ref, gt_ref, w_ref, o_ref = refs
    ob = y_ref[...] * m_ref[...]
    r = (jnp.dot(oa_ref[...], w_ref[0:D_A, :], preferred_element_type=F32)
         + jnp.dot(ob.astype(BF16), w_ref[D_A:D_A + D_B, :], preferred_element_type=F32))
    xn = x_ref[...] + gt_ref[...] * r
    if final:
        ms = jnp.mean(xn * xn, axis=-1, keepdims=True)
        xn = xn * lax.rsqrt(ms + EPS) * fw_ref[...]
    o_ref[...] = xn


def _out_proj(oa, y, m, x, gt, w_bf16, final_w):
    B, L, D = x.shape
    tm = min(L, OUT_PROJ_ROWS)
    final = final_w is not None
    half = lambda: pl.BlockSpec((None, tm, D_B), lambda b, i: (b, i, 0))
    in_specs = [half(), half(), half(),
                pl.BlockSpec((None, tm, D), lambda b, i: (b, i, 0)),
                pl.BlockSpec((None, 1, D), lambda b, i: (b, 0, 0)),
                pl.BlockSpec((D, D), lambda b, i: (0, 0))]
    args = [oa, y, m, x, gt, w_bf16]
    if final:
        in_specs.append(pl.BlockSpec((1, D), lambda b, i: (0, 0)))
        args.append(final_w.reshape(1, D))
    return pl.pallas_call(
        functools.partial(_outproj_kernel, final=final),
        grid=(B, L // tm),
        in_specs=in_specs,
        out_specs=pl.BlockSpec((None, tm, D), lambda b, i: (b, i, 0)),
        out_shape=jax.ShapeDtypeStruct((B, L, D), F32),
        compiler_params=_cparams("arbitrary", "arbitrary"),
        name="out_proj",
    )(*args)


def _gate_params(lb):
    rows = []
    for d in range(2):
        rows += [jnp.log(lb[d]), jnp.log1p(-lb[d]), 1.0 - lb[d]]
    rows += [jnp.zeros_like(lb[0])] * (SUBLANES - len(rows))
    return jnp.stack(rows, axis=0)


def kernel(x, c, ctx, c_ctx, norm_w, w_ada, b_ada, w_in, w_out, lb_logits, g_norm_w,
           conv_w, conv_b, hy_w1, hy_b1, hy_freq, hy_w2, hy_b2, hy_w3, hy_b3, hy_w4,
           hy_bias, final_norm_w):
    B, L_lat, D = x.shape
    L_ctx = ctx.shape[1]
    p_lb = jax.nn.softmax(lb_logits.astype(F32), axis=0)
    lbs = jnp.cumsum(p_lb, axis=0)
    lbs = lbs - lbs[0:1]

    n_rows = 2 * SUBLANES
    cc = jnp.zeros((n_rows, D), F32).at[:B].set(c).at[B].set(c_ctx)
    mod = _modulation(cc, w_ada, b_ada)

    dc_lat = _dft_consts(L_lat)
    dc_ctx = _dft_single_consts(L_ctx)
    n1, n2 = dc_lat["n1"], dc_lat["n2"]
    order_lat = (np.arange(n2)[:, None] + n2 * np.arange(n1)[None, :]).reshape(-1)
    z_lat = _pos_rows(L_lat, order_lat)
    z_ctx = _pos_rows(L_ctx, np.arange(2 * L_ctx))
    zero_state = jnp.zeros((B, A_HEADS, HEAD_DIM, HEAD_DIM), F32)

    for l in range(DEPTH):
        last = l == DEPTH - 1
        sh_x, sc_x, gt_x = [mod[l, :B, j * D:(j + 1) * D].reshape(B, 1, D) for j in range(3)]
        sh_c, sc_c, gt_c = [mod[l, B, j * D:(j + 1) * D].reshape(1, 1, D) for j in range(3)]
        w_in_l = w_in[l].astype(BF16)
        w_out_l = w_out[l].astype(BF16)
        gate_par = _gate_params(lbs[l])
        conv_par = jnp.concatenate([conv_w[l], conv_b[l][None],
                                    jnp.zeros((SUBLANES - 4, 3 * D_B), F32)], axis=0)
        filt_args = (hy_w1[l], hy_b1[l], hy_freq[l], hy_w2[l], hy_b2[l], hy_w3[l], hy_b3[l], hy_w4[l])

        flat = lambda a: a.reshape(1, B * L_ctx, a.shape[-1])
        unflat = lambda a: a.reshape(B, L_ctx, a.shape[-1])
        if last:
            pc, _ = _in_proj(flat(ctx), norm_w[l], sc_c, sh_c, gate_par, w_in_l[:, :4 * D_A])
            _, s_f, s_b = _hgrn(unflat(pc), g_norm_w[l], zero_state, zero_state, with_out=False)
        else:
            pc, pc_hy = _in_proj(flat(ctx), norm_w[l], sc_c, sh_c, gate_par, w_in_l)
            oa_c, s_f, s_b = _hgrn(unflat(pc), g_norm_w[l], zero_state, zero_state, with_out=True)
            vin_c, m_c = _hyena_pre(unflat(pc_hy), conv_par)
            filt_c = _hyena_filter(z_ctx, *filt_args)
            y_c = _long_conv_single(vin_c, filt_c, hy_bias[l].reshape(1, D_B), dc_ctx)
            ctx = unflat(_out_proj(flat(oa_c), flat(y_c), flat(m_c), flat(ctx), gt_c, w_out_l, None))

        px, px_hy = _in_proj(x, norm_w[l], sc_x, sh_x, gate_par, w_in_l)
        oa, _, _ = _hgrn(px, g_norm_w[l], s_f, s_b, with_out=True)
        vin, m = _hyena_pre(px_hy, conv_par)
        filt_x = _hyena_filter(z_lat, *filt_args)
        y = _long_conv_two_stage(vin, filt_x, hy_bias[l].reshape(1, D_B), dc_lat)
        x = _out_proj(oa, y, m, x, gt_x, w_out_l, final_norm_w if last else None)

    return x
```
